```python
import math
import jax, jax.numpy as jnp
from jax import lax
import numpy as np

D_MODEL = 2048
BATCH = 1
SEQ = 8192
DEPTH = 4

N_MIXERS = 2
N_POOL_LAYERS = (DEPTH + 1) // 2
N_RWKV_LAYERS = DEPTH // 2
POOL_GROUPS = 4
POOL_WINDOWS = (2, 4, 8, 16)
POOL_GROUP_DIM = D_MODEL // POOL_GROUPS
RWKV_HEAD_DIM = 64
RWKV_HEADS = D_MODEL // RWKV_HEAD_DIM
DECAY_LORA = max(32, int(round(1.8 * D_MODEL ** 0.5 / 32)) * 32)
AAA_LORA = max(32, int(round(1.8 * D_MODEL ** 0.5 / 32)) * 32)
GATE_LORA = max(32, int(round(0.6 * D_MODEL ** 0.8 / 32)) * 32)
N_SHIFT_MIX = 6
GN_EPS = 64e-5
MEM_LEN = 256
XATTN_HEADS = 4
XATTN_HEAD_DIM = D_MODEL // XATTN_HEADS
N_GROUPS = 8
EXPERTS_PER_GROUP = 8
N_EXPERTS = N_GROUPS * EXPERTS_PER_GROUP
TOP_K = 2
D_EXPERT = D_MODEL // 8
ROW_BLOCK = 128
RMS_EPS = 1e-6

kernel_name = "hybrid_pool_rwkv7_memxattn_hmoe"


def rms_norm(x, w):
    xf = x.astype(jnp.float32)
    y = xf * lax.rsqrt(jnp.mean(xf * xf, axis=-1, keepdims=True) + RMS_EPS)
    return (y * w.astype(jnp.float32)).astype(x.dtype)


def pool_mix(h, pool_w, pool_scale):
    B, S, D = h.shape
    hf = h.astype(jnp.float32)
    c = jnp.cumsum(hf, axis=1)
    t = jnp.arange(S)
    outs = []
    for g, win in enumerate(POOL_WINDOWS):
        sl = slice(g * POOL_GROUP_DIM, (g + 1) * POOL_GROUP_DIM)
        cg = c[..., sl]
        lag = jnp.pad(cg, ((0, 0), (win, 0), (0, 0)))[:, :S]
        cnt = jnp.minimum(t + 1, win).astype(jnp.float32)[None, :, None]
        outs.append((cg - lag) / cnt - hf[..., sl])
    pooled = jnp.stack(outs, axis=2).astype(h.dtype)
    y = jnp.einsum('bsgc,gcd->bsgd', pooled, pool_w).reshape(B, S, D)
    return y * pool_scale


def _rwkv7_scan(r, decay, k, v, kk, a):
    _, B, H, N = r.shape

    def step(state, inp):
        r_t, w_t, k_t, v_t, kk_t, a_t = inp
        sa = jnp.einsum('bhvk,bhk->bhv', state, -kk_t)
        state = (state * w_t[:, :, None, :]
                 + sa[..., None] * (kk_t * a_t)[:, :, None, :]
                 + v_t[..., None] * k_t[:, :, None, :])
        return state, jnp.einsum('bhvk,bhk->bhv', state, r_t)

    s0 = jnp.zeros((B, H, N, N), jnp.float32)
    _, y = lax.scan(step, s0, (r, decay, k, v, kk, a))
    return y


def rwkv7_mix(h, mu, w_rkv, w_o, w0, w1, w2, a0, a1, a2, g1, g2, k_k, k_a, r_k, ln_w, ln_b):
    B, S, D = h.shape
    H, N = RWKV_HEADS, RWKV_HEAD_DIM
    f32 = jnp.float32
    xx = jnp.pad(h, ((0, 0), (1, 0), (0, 0)))[:, :S] - h
    xs = h[None] + xx[None] * mu[:, None, None, :]
    rkv = jnp.einsum('nbsd,nde->nbse', xs[:3], w_rkv)
    r, k, v = rkv[0], rkv[1], rkv[2]
    w_log = -jax.nn.softplus(-(w0.astype(f32) + (jnp.tanh(xs[3] @ w1) @ w2).astype(f32))) - 0.5
    decay = jnp.exp(-jnp.exp(w_log))
    a = jax.nn.sigmoid(a0.astype(f32) + ((xs[4] @ a1) @ a2).astype(f32))
    g = jax.nn.sigmoid(xs[5] @ g1) @ g2

    def heads(t):
        return t.astype(f32).reshape(B, S, H, N)

    r_h, k_h, v_h, a_h, w_h = heads(r), heads(k), heads(v), heads(a), heads(decay)
    kk = k_h * k_k.astype(f32).reshape(H, N)
    kk = kk / jnp.maximum(jnp.sqrt(jnp.sum(kk * kk, axis=-1, keepdims=True)), 1e-12)
    k_h = k_h * (1.0 + (a_h - 1.0) * k_a.astype(f32).reshape(H, N))

    def tm(t):
        return jnp.moveaxis(t, 1, 0)

    y = jnp.moveaxis(_rwkv7_scan(tm(r_h), tm(w_h), tm(k_h), tm(v_h), tm(kk), tm(a_h)), 0, 1)
    mean = jnp.mean(y, axis=-1, keepdims=True)
    var = jnp.mean(jnp.square(y - mean), axis=-1, keepdims=True)
    y = ((y - mean) * lax.rsqrt(var + GN_EPS)).reshape(B, S, D) * ln_w.astype(f32) + ln_b.astype(f32)
    bonus = jnp.sum(r_h * k_h * r_k.astype(f32), axis=-1, keepdims=True) * v_h
    y = y + bonus.reshape(B, S, D)
    return (y * g.astype(f32)).astype(h.dtype) @ w_o


def memory_cross_attn(hn, mn, wq, wkv, wo):
    B, S, D = hn.shape
    M = mn.shape[1]
    q = (hn @ wq).reshape(B, S, XATTN_HEADS, XATTN_HEAD_DIM)
    kv = mn @ wkv
    k = kv[..., :D].reshape(B, M, XATTN_HEADS, XATTN_HEAD_DIM)
    v = kv[..., D:].reshape(B, M, XATTN_HEADS, XATTN_HEAD_DIM)
    s = jnp.einsum('bshd,bmhd->bhsm', q, k).astype(jnp.float32) * (XATTN_HEAD_DIM ** -0.5)
    p = jax.nn.softmax(s, axis=-1).astype(v.dtype)
    o = jnp.einsum('bhsm,bmhd->bshd', p, v).reshape(B, S, D)
    return o @ wo


def routed_experts(ht, expert_idx, gate, w_gu, w_down):
    T, D = ht.shape
    A = T * TOP_K
    n_blocks = -(-A // ROW_BLOCK) + N_EXPERTS
    n_rows = n_blocks * ROW_BLOCK
    flat_e = expert_idx.reshape(A).astype(jnp.int32)
    flat_tok = jnp.repeat(jnp.arange(T, dtype=jnp.int32), TOP_K)
    flat_gate = gate.reshape(A)
    order = jnp.argsort(flat_e)
    e_sorted = flat_e[order]
    counts = jnp.bincount(flat_e, length=N_EXPERTS).astype(jnp.int32)
    padded = (counts + ROW_BLOCK - 1) // ROW_BLOCK * ROW_BLOCK
    pad_end = jnp.cumsum(padded)
    pad_start = pad_end - padded
    start = jnp.cumsum(counts) - counts
    dest = pad_start[e_sorted] + jnp.arange(A, dtype=jnp.int32) - start[e_sorted]
    row_tok = jnp.full((n_rows,), T, jnp.int32).at[dest].set(flat_tok[order])
    row_gate = jnp.zeros((n_rows,), flat_gate.dtype).at[dest].set(flat_gate[order])
    block_start = jnp.arange(n_blocks, dtype=jnp.int32) * ROW_BLOCK
    block_e = jnp.minimum(jnp.searchsorted(pad_end, block_start, side='right'), N_EXPERTS - 1)
    h_pad = jnp.concatenate([ht, jnp.zeros((1, D), ht.dtype)], axis=0)
    xb = h_pad[row_tok].reshape(n_blocks, ROW_BLOCK, D)

    def expert_block(args):
        xblk, e = args
        gu = xblk @ w_gu[e]
        gg, uu = gu[:, :D_EXPERT], gu[:, D_EXPERT:]
        return (jax.nn.silu(gg) * uu) @ w_down[e]

    yb = lax.map(expert_block, (xb, block_e)).reshape(n_rows, D)
    y = jax.ops.segment_sum(yb * row_gate[:, None].astype(yb.dtype), row_tok, num_segments=T + 1)
    return y[:T]


def hier_moe(h, wg, bg, we, be, w_gu, w_down):
    B, S, D = h.shape
    T = B * S
    ht = h.reshape(T, D)
    p_g = jax.nn.softmax((ht @ wg).astype(jnp.float32) + bg.astype(jnp.float32), axis=-1)
    pg_top, g_idx = lax.top_k(p_g, 1)
    logit_e = ((ht @ we).astype(jnp.float32) + be.astype(jnp.float32)).reshape(T, N_GROUPS, EXPERTS_PER_GROUP)
    logit_in = jnp.take_along_axis(logit_e, g_idx[:, :, None], axis=1)[:, 0]
    pe_top, e_local = lax.top_k(jax.nn.softmax(logit_in, axis=-1), TOP_K)
    gate = pg_top * pe_top / jnp.sum(pe_top, axis=-1, keepdims=True)
    expert_idx = g_idx * EXPERTS_PER_GROUP + e_local
    return routed_experts(ht, expert_idx, gate, w_gu, w_down).reshape(B, S, D)


def setup_inputs(seed: int = 0) -> dict:
    key = jax.random.key(seed)
    keys = iter(jax.random.split(key, 48))
    f32 = jnp.float32
    L, P, R, D = DEPTH, N_POOL_LAYERS, N_RWKV_LAYERS, D_MODEL

    def nrm(shape, scale):
        return jax.random.normal(next(keys), shape, f32) * scale

    def gain(shape):
        return 1.0 + nrm(shape, 0.02)

    return {
        "x": nrm((BATCH, SEQ, D), 1.0),
        "mem": nrm((BATCH, MEM_LEN, D), 1.0),
        "ln_mix_w": gain((L, D)),
        "ln_xattn_w": gain((L, D)),
        "ln_mem_w": gain((L, D)),
        "xattn_wq": nrm((L, D, D), D ** -0.5),
        "xattn_wkv": nrm((L, D, 2 * D), D ** -0.5),
        "xattn_wo": nrm((L, D, D), D ** -0.5),
        "ln_ffn_w": gain((L, D)),
        "router_group_w": nrm((L, D, N_GROUPS), D ** -0.5),
        "router_group_b": nrm((L, N_GROUPS), 0.01),
        "router_expert_w": nrm((L, D, N_EXPERTS), D ** -0.5),
        "router_expert_b": nrm((L, N_EXPERTS), 0.01),
        "moe_w_gate_up": nrm((L, N_EXPERTS, D, 2 * D_EXPERT), D ** -0.5),
        "moe_w_down": nrm((L, N_EXPERTS, D_EXPERT, D), D_EXPERT ** -0.5),
        "pool_w": nrm((P, POOL_GROUPS, POOL_GROUP_DIM, POOL_GROUP_DIM), POOL_GROUP_DIM ** -0.5),
        "pool_scale": gain((P, D)),
        "rwkv_mu": jax.random.uniform(next(keys), (R, N_SHIFT_MIX, D), f32),
        "rwkv_w_rkv": nrm((R, 3, D, D), D ** -0.5),
        "rwkv_w_o": nrm((R, D, D), D ** -0.5),
        "rwkv_w0": jax.random.uniform(next(keys), (R, D), f32, -6.0, 1.0),
        "rwkv_w1": nrm((R, D, DECAY_LORA), D ** -0.5),
        "rwkv_w2": nrm((R, DECAY_LORA, D), 0.1 * DECAY_LORA ** -0.5),
        "rwkv_a0": nrm((R, D), 0.1),
        "rwkv_a1": nrm((R, D, AAA_LORA), D ** -0.5),
        "rwkv_a2": nrm((R, AAA_LORA, D), 0.1 * AAA_LORA ** -0.5),
        "rwkv_g1": nrm((R, D, GATE_LORA), D ** -0.5),
        "rwkv_g2": nrm((R, GATE_LORA, D), GATE_LORA ** -0.5),
        "rwkv_k_k": 0.85 + nrm((R, D), 0.1),
        "rwkv_k_a": 1.0 + nrm((R, D), 0.1),
        "rwkv_r_k": nrm((R, RWKV_HEADS, RWKV_HEAD_DIM), 0.1),
        "rwkv_ln_w": gain((R, D)),
        "rwkv_ln_b": nrm((R, D), 0.02),
        "ln_out_w": gain((D,)),
    }


def reference(x, mem, ln_mix_w, ln_xattn_w, ln_mem_w, xattn_wq, xattn_wkv, xattn_wo, ln_ffn_w,
              router_group_w, router_group_b, router_expert_w, router_expert_b, moe_w_gate_up, moe_w_down,
              pool_w, pool_scale, rwkv_mu, rwkv_w_rkv, rwkv_w_o, rwkv_w0, rwkv_w1, rwkv_w2, rwkv_a0, rwkv_a1,
              rwkv_a2, rwkv_g1, rwkv_g2, rwkv_k_k, rwkv_k_a, rwkv_r_k, rwkv_ln_w, rwkv_ln_b, ln_out_w):
    h = x
    for i in range(DEPTH):
        j = i // N_MIXERS
        hn = rms_norm(h, ln_mix_w[i])
        if i % N_MIXERS == 0:
            h = h + pool_mix(hn, pool_w[j], pool_scale[j])
        else:
            h = h + rwkv7_mix(hn, rwkv_mu[j], rwkv_w_rkv[j], rwkv_w_o[j], rwkv_w0[j], rwkv_w1[j], rwkv_w2[j],
                              rwkv_a0[j], rwkv_a1[j], rwkv_a2[j], rwkv_g1[j], rwkv_g2[j], rwkv_k_k[j],
                              rwkv_k_a[j], rwkv_r_k[j], rwkv_ln_w[j], rwkv_ln_b[j])
        h = h + memory_cross_attn(rms_norm(h, ln_xattn_w[i]), rms_norm(mem, ln_mem_w[i]),
                                  xattn_wq[i], xattn_wkv[i], xattn_wo[i])
        h = h + hier_moe(rms_norm(h, ln_ffn_w[i]), router_group_w[i], router_group_b[i],
                         router_expert_w[i], router_expert_b[i], moe_w_gate_up[i], moe_w_down[i])
    return rms_norm(h, ln_out_w)
```

```python
import functools

import jax
import jax.numpy as jnp
from jax import lax
from jax.experimental import pallas as pl
from jax.experimental.pallas import tpu as pltpu

F32 = jnp.float32
BF16 = jnp.bfloat16

D_MODEL = 2048
POOL_WINDOWS = (2, 4, 8, 16)
POOL_GROUP_DIM = D_MODEL // len(POOL_WINDOWS)
MAX_WINDOW = max(POOL_WINDOWS)
RWKV_HEAD_DIM = 64
GN_EPS = 64e-5
XATTN_HEADS = 4
XATTN_HEAD_DIM = D_MODEL // XATTN_HEADS
N_GROUPS = 8
EXPERTS_PER_GROUP = 8
N_EXPERTS = N_GROUPS * EXPERTS_PER_GROUP
TOP_K = 2
D_EXPERT = D_MODEL // 8
ROW_BLOCK = 128
RMS_EPS = 1e-6

LANES = 128
SUBLANES = 8
VMEM_LIMIT = 56 * 1024 * 1024

CHUNK = 64
PAIR = 2 * RWKV_HEAD_DIM
SCAN_ROWS = 512
SCAN_LANES = 256

NN = (((1,), (0,)), ((), ()))
NT = (((1,), (1,)), ((), ()))


def _params(*sem):
    return pltpu.CompilerParams(dimension_semantics=sem, vmem_limit_bytes=VMEM_LIMIT)


def _rms(x, w):
    return x * lax.rsqrt(jnp.mean(x * x, axis=-1, keepdims=True) + RMS_EPS) * w


def _dot(a, b, dims=NN):
    return lax.dot_general(a, b, dims, preferred_element_type=F32)


def _split(x):
    hi = x.astype(BF16)
    return hi, (x - hi.astype(F32)).astype(BF16)


def _mm3(a, b, dims=NN):
    ah, al = _split(a)
    bh, bl = _split(b)
    return _dot(ah, bh, dims) + (_dot(ah, bl, dims) + _dot(al, bh, dims))


def _mm_exact_rhs(a, b_bf16, dims=NN):
    hi = a.astype(BF16)
    r1 = a - hi.astype(F32)
    mid = r1.astype(BF16)
    lo = (r1 - mid.astype(F32)).astype(BF16)
    return _dot(hi, b_bf16, dims) + (_dot(mid, b_bf16, dims) + _dot(lo, b_bf16, dims))


def _bmm_body(x_ref, w_ref, o_ref):
    o_ref[...] = _dot(x_ref[...], w_ref[...]).astype(o_ref.dtype)


def _bmm(xs, ws, tm=512, tn=1024):
    B, M, K = xs.shape
    N = ws.shape[2]
    return pl.pallas_call(
        _bmm_body,
        grid=(B, N // tn, M // tm),
        in_specs=[pl.BlockSpec((None, tm, K), lambda b, j, i: (b, i, 0)),
                  pl.BlockSpec((None, K, tn), lambda b, j, i: (b, 0, j))],
        out_specs=pl.BlockSpec((None, tm, tn), lambda b, j, i: (b, i, j)),
        out_shape=jax.ShapeDtypeStruct((B, M, N), F32),
        compiler_params=_params("arbitrary", "arbitrary", "arbitrary"),
        name="rwkv_rkv_proj",
    )(xs, ws)


def _mm_res_body(x_ref, w_ref, r_ref, o_ref):
    o_ref[...] = r_ref[...] + _dot(x_ref[...], w_ref[...])


def _mm_residual(x, w, res, tm=512, tn=1024):
    M, K = x.shape
    N = w.shape[1]
    return pl.pallas_call(
        _mm_res_body,
        grid=(N // tn, M // tm),
        in_specs=[pl.BlockSpec((tm, K), lambda j, i: (i, 0)),
                  pl.BlockSpec((K, tn), lambda j, i: (0, j)),
                  pl.BlockSpec((tm, tn), lambda j, i: (i, j))],
        out_specs=pl.BlockSpec((tm, tn), lambda j, i: (i, j)),
        out_shape=jax.ShapeDtypeStruct((M, N), F32),
        compiler_params=_params("arbitrary", "arbitrary"),
        name="rwkv_out_proj",
    )(x, w, res)


def _pool_body(h_ref, halo_ref, lnw_ref, pw_ref, ps_ref, o_ref):
    i = pl.program_id(0)
    ts = h_ref.shape[0]
    h = h_ref[...]
    lnw = lnw_ref[...]
    x = _rms(h, lnw)
    xh = _rms(halo_ref[...], lnw) * (i > 0).astype(F32)
    xe = jnp.concatenate([xh, x], axis=0)
    t = i * ts + lax.broadcasted_iota(jnp.int32, (ts, 1), 0)
    outs = []
    for g, win in enumerate(POOL_WINDOWS):
        sl = slice(g * POOL_GROUP_DIM, (g + 1) * POOL_GROUP_DIM)
        acc = xe[:, sl]
        span = 1
        while span < win:
            acc = acc + pltpu.roll(acc, span, axis=0)
            span *= 2
        cnt = jnp.minimum(t + 1, win).astype(F32)
        pooled = acc[MAX_WINDOW:, :] / cnt - x[:, sl]
        outs.append(_dot(pooled.astype(BF16), pw_ref[g]))
    y = jnp.concatenate(outs, axis=-1)
    o_ref[...] = h + y * ps_ref[...]


def _pool_layer(h, ln_w, pool_w, pool_scale, ts=512):
    S, D = h.shape
    G = pool_w.shape[0]
    halo_blocks = ts // MAX_WINDOW
    return pl.pallas_call(
        _pool_body,
        grid=(S // ts,),
        in_specs=[pl.BlockSpec((ts, D), lambda i: (i, 0)),
                  pl.BlockSpec((MAX_WINDOW, D), lambda i: (jnp.maximum(i * halo_blocks - 1, 0), 0)),
                  pl.BlockSpec((1, D), lambda i: (0, 0)),
                  pl.BlockSpec((G, POOL_GROUP_DIM, POOL_GROUP_DIM), lambda i: (0, 0, 0)),
                  pl.BlockSpec((1, D), lambda i: (0, 0))],
        out_specs=pl.BlockSpec((ts, D), lambda i: (i, 0)),
        out_shape=jax.ShapeDtypeStruct((S, D), F32),
        compiler_params=_params("arbitrary"),
        name="pool_layer",
    )(h, h, ln_w.reshape(1, D), pool_w.astype(BF16), pool_scale.reshape(1, D))


def _memkv_body(mem_ref, lnw_ref, w_ref, o_ref):
    mn = _rms(mem_ref[...], lnw_ref[...])
    o_ref[...] = _dot(mn.astype(BF16), w_ref[...]).astype(o_ref.dtype)


def _mem_kv(mem, ln_w, wkv, tn=1024):
    M, D = mem.shape
    N = wkv.shape[1]
    return pl.pallas_call(
        _memkv_body,
        grid=(N // tn,),
        in_specs=[pl.BlockSpec((M, D), lambda j: (0, 0)),
                  pl.BlockSpec((1, D), lambda j: (0, 0)),
                  pl.BlockSpec((D, tn), lambda j: (0, j))],
        out_specs=pl.BlockSpec((M, tn), lambda j: (0, j)),
        out_shape=jax.ShapeDtypeStruct((M, N), BF16),
        compiler_params=_params("arbitrary"),
        name="xattn_mem_kv",
    )(mem, ln_w.reshape(1, D), wkv)


def _xattn_body(h_ref, lnw_ref, wq_ref, k_ref, v_ref, wo_ref, o_ref):
    h = h_ref[...]
    hn = _rms(h, lnw_ref[...]).astype(BF16)
    q = _dot(hn, wq_ref[...])
    scale = XATTN_HEAD_DIM ** -0.5
    heads = []
    for hd in range(XATTN_HEADS):
        sl = slice(hd * XATTN_HEAD_DIM, (hd + 1) * XATTN_HEAD_DIM)
        s = _dot(q[:, sl].astype(BF16), k_ref[:, sl], NT) * scale
        s = s - jnp.max(s, axis=-1, keepdims=True)
        e = jnp.exp(s)
        p = e / jnp.sum(e, axis=-1, keepdims=True)
        heads.append(_dot(p.astype(BF16), v_ref[:, sl]))
    o = jnp.concatenate(heads, axis=-1).astype(BF16)
    o_ref[...] = h + _dot(o, wo_ref[...])


def _xattn_layer(h, ln_w, wq, k, v, wo, ts=256):
    S, D = h.shape
    M = k.shape[0]
    once = dict(pipeline_mode=pl.Buffered(1))
    return pl.pallas_call(
        _xattn_body,
        grid=(S // ts,),
        in_specs=[pl.BlockSpec((ts, D), lambda i: (i, 0)),
                  pl.BlockSpec((1, D), lambda i: (0, 0)),
                  pl.BlockSpec((D, D), lambda i: (0, 0), **once),
                  pl.BlockSpec((M, D), lambda i: (0, 0), **once),
                  pl.BlockSpec((M, D), lambda i: (0, 0), **once),
                  pl.BlockSpec((D, D), lambda i: (0, 0), **once)],
        out_specs=pl.BlockSpec((ts, D), lambda i: (i, 0)),
        out_shape=jax.ShapeDtypeStruct((S, D), F32),
        compiler_params=_params("arbitrary"),
        name="xattn_layer",
    )(h, ln_w.reshape(1, D), wq, k, v, wo)


def _router_body(h_ref, lnw_ref, whi_ref, wlo_ref, b_ref, hn_ref, r_ref):
    hn = _rms(h_ref[...], lnw_ref[...])
    hn_ref[...] = hn
    xh, xl = _split(hn)
    whi = whi_ref[...]
    lg = _dot(xh, whi) + (_dot(xh, wlo_ref[...]) + _dot(xl, whi)) + b_ref[...]
    ts = lg.shape[0]
    lane = lax.broadcasted_iota(jnp.int32, (ts, LANES), 1)
    lanef = lane.astype(F32)
    neg = jnp.float32(-jnp.inf)
    gl = jnp.where(lane < N_GROUPS, lg, neg)
    gmax = jnp.max(gl, axis=-1, keepdims=True)
    pg_top = 1.0 / jnp.sum(jnp.exp(gl - gmax), axis=-1, keepdims=True)
    g_idx = jnp.min(jnp.where(gl == gmax, lanef, float(LANES)), axis=-1, keepdims=True)
    lo_lane = g_idx * EXPERTS_PER_GROUP + N_GROUPS
    el = jnp.where(lanef >= lo_lane, jnp.where(lanef < lo_lane + EXPERTS_PER_GROUP, lg, neg), neg)
    m1 = jnp.max(el, axis=-1, keepdims=True)
    i1 = jnp.min(jnp.where(el == m1, lanef, float(LANES)), axis=-1, keepdims=True)
    el2 = jnp.where(lanef == i1, neg, el)
    m2 = jnp.max(el2, axis=-1, keepdims=True)
    i2 = jnp.min(jnp.where(el2 == m2, lanef, float(LANES)), axis=-1, keepdims=True)
    e2 = jnp.exp(m2 - m1)
    g1 = pg_top / (1.0 + e2)
    g2 = pg_top * e2 / (1.0 + e2)
    out = jnp.where(lane == 0, g1, 0.0)
    out = jnp.where(lane == 1, g2, out)
    out = jnp.where(lane == 2, i1 - N_GROUPS, out)
    out = jnp.where(lane == 3, i2 - N_GROUPS, out)
    r_ref[...] = out


def _router(h, ln_w, wg, bg, we, be, ts=512):
    S, D = h.shape
    w = jnp.zeros((D, LANES), F32).at[:, :N_GROUPS].set(wg).at[:, N_GROUPS:N_GROUPS + N_EXPERTS].set(we)
    b = jnp.zeros((1, LANES), F32).at[0, :N_GROUPS].set(bg).at[0, N_GROUPS:N_GROUPS + N_EXPERTS].set(be)
    whi, wlo = _split(w)
    return pl.pallas_call(
        _router_body,
        grid=(S // ts,),
        in_specs=[pl.BlockSpec((ts, D), lambda i: (i, 0)),
                  pl.BlockSpec((1, D), lambda i: (0, 0)),
                  pl.BlockSpec((D, LANES), lambda i: (0, 0)),
                  pl.BlockSpec((D, LANES), lambda i: (0, 0)),
                  pl.BlockSpec((1, LANES), lambda i: (0, 0))],
        out_specs=[pl.BlockSpec((ts, D), lambda i: (i, 0)),
                   pl.BlockSpec((ts, LANES), lambda i: (i, 0))],
        out_shape=[jax.ShapeDtypeStruct((S, D), F32), jax.ShapeDtypeStruct((S, LANES), F32)],
        compiler_params=_params("arbitrary"),
        name="moe_router",
    )(h, ln_w.reshape(1, D), whi, wlo, b)


def _row_copy(src_hbm, row, dst, slot, r, sem):
    return pltpu.make_async_copy(src_hbm.at[pl.ds(row, 1), :], dst.at[slot, pl.ds(r, 1), :], sem.at[slot])


def _start_row_gather(idx_ref, base, src_hbm, dst, slot, sem, n_rows):
    def issue(r, carry):
        _row_copy(src_hbm, idx_ref[base + r], dst, slot, r, sem).start()
        return carry
    lax.fori_loop(0, n_rows, issue, 0)


def _wait_row_gather(src_hbm, dst, slot, sem, n_rows):
    pltpu.make_async_copy(src_hbm.at[pl.ds(0, n_rows), :], dst.at[slot], sem.at[slot]).wait()


def _experts_body(nblk_ref, be_ref, tok_ref, hn_hbm, gate_ref, wgu_ref, wd_ref, o_ref, xbuf, sem):
    i = pl.program_id(0)
    n_used = nblk_ref[0]
    slot = i % 2

    @pl.when(i == 0)
    def _():
        _start_row_gather(tok_ref, 0, hn_hbm, xbuf, 0, sem, ROW_BLOCK)

    @pl.when(i + 1 < n_used)
    def _():
        _start_row_gather(tok_ref, (i + 1) * ROW_BLOCK, hn_hbm, xbuf, 1 - slot, sem, ROW_BLOCK)

    @pl.when(i < n_used)
    def _():
        _wait_row_gather(hn_hbm, xbuf, slot, sem, ROW_BLOCK)
        x = xbuf[slot].astype(BF16)
        gu = _dot(x, wgu_ref[...])
        gg = gu[:, :D_EXPERT]
        uu = gu[:, D_EXPERT:]
        act = (gg * (1.0 / (1.0 + jnp.exp(-gg))) * uu).astype(BF16)
        o_ref[...] = _dot(act, wd_ref[...]) * gate_ref[...]

    @pl.when(i >= n_used)
    def _():
        o_ref[...] = jnp.zeros_like(o_ref)


def _experts(n_used, block_e, row_tok, hn, row_gate, w_gu, w_down):
    n_rows = row_tok.shape[0]
    n_blocks = n_rows // ROW_BLOCK
    D = hn.shape[1]
    F2 = w_gu.shape[2]
    return pl.pallas_call(
        _experts_body,
        grid_spec=pltpu.PrefetchScalarGridSpec(
            num_scalar_prefetch=3,
            grid=(n_blocks,),
            in_specs=[pl.BlockSpec(memory_space=pl.ANY),
                      pl.BlockSpec((ROW_BLOCK, 1), lambda i, nb, be, tok: (i, 0)),
                      pl.BlockSpec((None, D, F2), lambda i, nb, be, tok: (be[i], 0, 0)),
                      pl.BlockSpec((None, F2 // 2, D), lambda i, nb, be, tok: (be[i], 0, 0))],
            out_specs=pl.BlockSpec((ROW_BLOCK, D), lambda i, nb, be, tok: (i, 0)),
            scratch_shapes=[pltpu.VMEM((2, ROW_BLOCK, D), F32), pltpu.SemaphoreType.DMA((2,))]),
        out_shape=jax.ShapeDtypeStruct((n_rows, D), F32),
        compiler_params=_params("arbitrary"),
        name="moe_experts",
    )(n_used, block_e, row_tok, hn, row_gate.reshape(n_rows, 1), w_gu, w_down)


def _combine_body(final_norm, pos_ref, yb_hbm, h_ref, lnw_ref, o_ref, ybuf, sem):
    i = pl.program_id(0)
    n = pl.num_programs(0)
    tb = h_ref.shape[0]
    slot = i % 2

    @pl.when(i == 0)
    def _():
        _start_row_gather(pos_ref, 0, yb_hbm, ybuf, 0, sem, TOP_K * tb)

    @pl.when(i + 1 < n)
    def _():
        _start_row_gather(pos_ref, (i + 1) * TOP_K * tb, yb_hbm, ybuf, 1 - slot, sem, TOP_K * tb)

    _wait_row_gather(yb_hbm, ybuf, slot, sem, TOP_K * tb)
    out = h_ref[...] + (ybuf[slot, :tb, :] + ybuf[slot, tb:, :])
    if final_norm:
        out = _rms(out, lnw_ref[...])
    o_ref[...] = out


def _combine(pos, yb, h, ln_out_w, final_norm, tb=128):
    S, D = h.shape
    return pl.pallas_call(
        functools.partial(_combine_body, final_norm),
        grid_spec=pltpu.PrefetchScalarGridSpec(
            num_scalar_prefetch=1,
            grid=(S // tb,),
            in_specs=[pl.BlockSpec(memory_space=pl.ANY),
                      pl.BlockSpec((tb, D), lambda i, pos: (i, 0)),
                      pl.BlockSpec((1, D), lambda i, pos: (0, 0))],
            out_specs=pl.BlockSpec((tb, D), lambda i, pos: (i, 0)),
            scratch_shapes=[pltpu.VMEM((2, TOP_K * tb, D), F32), pltpu.SemaphoreType.DMA((2,))]),
        out_shape=jax.ShapeDtypeStruct((S, D), F32),
        compiler_params=_params("arbitrary"),
        name="moe_combine",
    )(pos, yb, h, ln_out_w.reshape(1, D))


def _moe_layer(h, ln_w, wg, bg, we, be, w_gu, w_down, ln_out_w, final_norm, tb=128):
    T, D = h.shape
    hn, routed = _router(h, ln_w, wg, bg, we, be)
    gate = routed[:, :TOP_K]
    expert_idx = routed[:, TOP_K:2 * TOP_K].astype(jnp.int32)
    A = T * TOP_K
    n_blocks = -(-A // ROW_BLOCK) + N_EXPERTS
    n_rows = n_blocks * ROW_BLOCK
    flat_e = expert_idx.reshape(A)
    flat_tok = jnp.repeat(jnp.arange(T, dtype=jnp.int32), TOP_K)
    order = jnp.argsort(flat_e)
    e_sorted = flat_e[order]
    counts = jnp.bincount(flat_e, length=N_EXPERTS).astype(jnp.int32)
    padded = (counts + ROW_BLOCK - 1) // ROW_BLOCK * ROW_BLOCK
    pad_end = jnp.cumsum(padded)
    pad_start = pad_end - padded
    start = jnp.cumsum(counts) - counts
    dest = pad_start[e_sorted] + jnp.arange(A, dtype=jnp.int32) - start[e_sorted]
    row_tok = jnp.zeros((n_rows,), jnp.int32).at[dest].set(flat_tok[order])
    row_gate = jnp.zeros((n_rows,), F32).at[dest].set(gate.reshape(A)[order])
    block_start = jnp.arange(n_blocks, dtype=jnp.int32) * ROW_BLOCK
    block_e = jnp.minimum(jnp.searchsorted(pad_end, block_start, side='right'), N_EXPERTS - 1).astype(jnp.int32)
    n_used = (pad_end[-1] // ROW_BLOCK).astype(jnp.int32).reshape(1)
    yb = _experts(n_used, block_e, row_tok, hn, row_gate, w_gu, w_down)
    pos = jnp.zeros((A,), jnp.int32).at[order].set(dest)
    pos = pos.reshape(T // tb, tb, TOP_K).transpose(0, 2, 1).reshape(A)
    return _combine(pos, yb, h, ln_out_w, final_norm, tb)


def _rwkv_prep_body(h_ref, halo_ref, lnw_ref, mu_ref, xs_ref):
    i = pl.program_id(0)
    lnw = lnw_ref[...]
    hn = _rms(h_ref[...], lnw)
    ts = hn.shape[0]
    last = _rms(halo_ref[...], lnw)[SUBLANES - 1:SUBLANES, :] * (i > 0).astype(F32)
    row = lax.broadcasted_iota(jnp.int32, (ts, 1), 0)
    prev = jnp.where(row == 0, last, pltpu.roll(hn, 1, axis=0))
    xx = prev - hn
    for n in range(xs_ref.shape[0]):
        xs_ref[n] = (hn + xx * mu_ref[n:n + 1, :]).astype(xs_ref.dtype)


def _rwkv_prep(h, ln_w, mu, ts=256):
    S, D = h.shape
    n_mix = mu.shape[0]
    halo_blocks = ts // SUBLANES
    return pl.pallas_call(
        _rwkv_prep_body,
        grid=(S // ts,),
        in_specs=[pl.BlockSpec((ts, D), lambda i: (i, 0)),
                  pl.BlockSpec((SUBLANES, D), lambda i: (jnp.maximum(i * halo_blocks - 1, 0), 0)),
                  pl.BlockSpec((1, D), lambda i: (0, 0)),
                  pl.BlockSpec((n_mix, D), lambda i: (0, 0))],
        out_specs=pl.BlockSpec((n_mix, ts, D), lambda i: (0, i, 0)),
        out_shape=jax.ShapeDtypeStruct((n_mix, S, D), BF16),
        compiler_params=_params("arbitrary"),
        name="rwkv_prep",
    )(h, h, ln_w.reshape(1, D), mu)


def _sigmoid(x):
    return 1.0 / (1.0 + jnp.exp(-x))


def _rwkv_lora_body(xw_ref, xa_ref, xg_ref, w0_ref, w1_ref, w2_ref, a0_ref, a1_ref, a2_ref, g1_ref, g2_ref,
                    lw_ref, a_ref, g_ref):
    z = w0_ref[...] + _dot(jnp.tanh(_dot(xw_ref[...], w1_ref[...])).astype(BF16), w2_ref[...])
    u = -z
    softplus = jnp.maximum(u, 0.0) + jnp.log1p(jnp.exp(-jnp.abs(u)))
    lw_ref[...] = -jnp.exp(-softplus - 0.5)
    a_ref[...] = _sigmoid(a0_ref[...] + _dot(_dot(xa_ref[...], a1_ref[...]).astype(BF16), a2_ref[...]))
    g_ref[...] = _dot(_sigmoid(_dot(xg_ref[...], g1_ref[...])).astype(BF16), g2_ref[...])


def _rwkv_lora(xs, w0, w1, w2, a0, a1, a2, g1, g2, ts=256):
    _, S, D = xs.shape

    def pad_lora(wa, wb):
        r = wa.shape[1]
        rp = -(-r // LANES) * LANES
        return (jnp.pad(wa, ((0, 0), (0, rp - r))).astype(BF16), jnp.pad(wb, ((0, rp - r), (0, 0))).astype(BF16))

    w1p, w2p = pad_lora(w1, w2)
    a1p, a2p = pad_lora(a1, a2)
    g1p, g2p = pad_lora(g1, g2)
    full = lambda arr: pl.BlockSpec(arr.shape, lambda i: (0,) * arr.ndim)
    vec = pl.BlockSpec((1, D), lambda i: (0, 0))
    row = pl.BlockSpec((ts, D), lambda i: (i, 0))
    return pl.pallas_call(
        _rwkv_lora_body,
        grid=(S // ts,),
        in_specs=[pl.BlockSpec((None, ts, D), lambda i: (3, i, 0)),
                  pl.BlockSpec((None, ts, D), lambda i: (4, i, 0)),
                  pl.BlockSpec((None, ts, D), lambda i: (5, i, 0)),
                  vec, full(w1p), full(w2p), vec, full(a1p), full(a2p), full(g1p), full(g2p)],
        out_specs=[row, row, row],
        out_shape=[jax.ShapeDtypeStruct((S, D), F32)] * 3,
        compiler_params=_params("arbitrary"),
        name="rwkv_lora",
    )(xs, xs, xs, w0.reshape(1, D), w1p, w2p, a0.reshape(1, D), a1p, a2p, g1p, g2p)


def _scan_chunk(r, k, v, lw, a, g, kkw, kaw, rkw, lnw, lnb, state, c):
    m0, m1, seg, tri, strict, incl, eye = c
    stack = lambda x: jnp.concatenate([x * m0, x * m1], axis=0)

    kk = k * kkw
    kk = kk / jnp.maximum(jnp.sqrt(_mm_exact_rhs(kk * kk, seg)), 1e-12)
    k2 = k * (1.0 + (a - 1.0) * kaw)
    bv = kk * a
    cw = _mm_exact_rhs_lhs(tri, lw)
    cl = cw[CHUNK - 1:CHUNK, :]
    at = -kk * jnp.exp(cw - lw)
    rt = r * jnp.exp(cw)
    dinv = jnp.exp(-cw)
    drem = jnp.exp(cl - cw)
    vs = stack(v)
    lhs = jnp.concatenate([stack(at), stack(rt)], axis=0)
    rhs = jnp.concatenate([stack(bv * dinv), stack(k2 * dinv)], axis=0)
    gram = _mm3(lhs, rhs, NT)
    n = 2 * CHUNK
    a_ab = jnp.where(strict, gram[:n, :n], 0.0)
    a_ak = jnp.where(strict, gram[:n, n:], 0.0)
    a_rb = jnp.where(incl, gram[n:, :n], 0.0)
    a_rk = jnp.where(incl, gram[n:, n:], 0.0)
    inv = eye + a_ab
    pw = a_ab
    span = 2
    while span < CHUNK:
        pw = _mm3(pw, pw)
        inv = inv + _mm3(inv, pw)
        span *= 2
    x = _mm3(stack(at), state, NT) + _mm3(a_ak, vs)
    u = _mm3(inv, x)
    y = _mm3(stack(rt), state, NT) + _mm3(a_rb, u) + _mm3(a_rk, vs)
    new_state = (state * jnp.exp(cl)
                 + _mm3(u.T, stack(bv * drem)) + _mm3(vs.T, stack(k2 * drem)))
    y = y[:CHUNK, :] + y[CHUNK:, :]
    inv_n = 1.0 / RWKV_HEAD_DIM
    mean = _mm_exact_rhs(y, seg) * inv_n
    yc = y - mean
    var = _mm_exact_rhs(yc * yc, seg) * inv_n
    yn = yc * lax.rsqrt(var + GN_EPS) * lnw + lnb
    bonus = _mm_exact_rhs(r * k2 * rkw, seg) * v
    return ((yn + bonus) * g), new_state


def _mm_exact_rhs_lhs(a_bf16, b):
    hi = b.astype(BF16)
    r1 = b - hi.astype(F32)
    mid = r1.astype(BF16)
    lo = (r1 - mid.astype(F32)).astype(BF16)
    return _dot(a_bf16, hi) + (_dot(a_bf16, mid) + _dot(a_bf16, lo))


def _scan_consts():
    n = 2 * CHUNK
    lane = lax.broadcasted_iota(jnp.int32, (1, PAIR), 1)
    m0 = (lane < RWKV_HEAD_DIM).astype(F32)
    m1 = 1.0 - m0
    ri = lax.broadcasted_iota(jnp.int32, (n, n), 0)
    ci = lax.broadcasted_iota(jnp.int32, (n, n), 1)
    head_r = (ri >= RWKV_HEAD_DIM).astype(F32)
    head_c = (ci >= RWKV_HEAD_DIM).astype(F32)
    seg = (head_r * head_c + (1.0 - head_r) * (1.0 - head_c)).astype(BF16)
    tri = (lax.broadcasted_iota(jnp.int32, (CHUNK, CHUNK), 1)
           <= lax.broadcasted_iota(jnp.int32, (CHUNK, CHUNK), 0)).astype(F32).astype(BF16)
    return (m0, m1, seg, tri, ci < ri, ci <= ri, (ci == ri).astype(F32))


def _scan_body(r_ref, k_ref, v_ref, lw_ref, a_ref, g_ref, kkw_ref, kaw_ref, rkw_ref, lnw_ref, lnb_ref,
               o_ref, state_ref):
    @pl.when(pl.program_id(1) == 0)
    def _():
        state_ref[...] = jnp.zeros_like(state_ref)

    n_pairs = o_ref.shape[1] // PAIR

    def chunk_step(j, carry):
        consts = _scan_consts()
        rows = pl.ds(pl.multiple_of(j * CHUNK, CHUNK), CHUNK)
        for p in range(n_pairs):
            ln = slice(p * PAIR, (p + 1) * PAIR)
            out, new_state = _scan_chunk(
                r_ref[rows, ln], k_ref[rows, ln], v_ref[rows, ln], lw_ref[rows, ln], a_ref[rows, ln],
                g_ref[rows, ln], kkw_ref[:, ln], kaw_ref[:, ln], rkw_ref[:, ln], lnw_ref[:, ln],
                lnb_ref[:, ln], state_ref[p], consts)
            state_ref[p] = new_state
            o_ref[rows, ln] = out.astype(o_ref.dtype)
        return carry

    lax.fori_loop(0, o_ref.shape[0] // CHUNK, chunk_step, 0)


def _rwkv_scan(rkv, lw, a, g, k_k, k_a, r_k, ln_w, ln_b):
    _, S, D = rkv.shape
    tr, tl = min(SCAN_ROWS, S), SCAN_LANES
    blk = lambda n: pl.BlockSpec((None, tr, tl), lambda p, i: (n, i, p))
    row = pl.BlockSpec((tr, tl), lambda p, i: (i, p))
    vec = pl.BlockSpec((1, tl), lambda p, i: (0, p))
    as_row = lambda w: w.reshape(1, D)
    return pl.pallas_call(
        _scan_body,
        grid=(D // tl, S // tr),
        in_specs=[blk(0), blk(1), blk(2), row, row, row, vec, vec, vec, vec, vec],
        out_specs=row,
        out_shape=jax.ShapeDtypeStruct((S, D), BF16),
        scratch_shapes=[pltpu.VMEM((tl // PAIR, PAIR, PAIR), F32)],
        compiler_params=_params("arbitrary", "arbitrary"),
        name="rwkv_scan",
    )(rkv, rkv, rkv, lw, a, g, as_row(k_k), as_row(k_a), as_row(r_k), as_row(ln_w), as_row(ln_b))


def _rwkv_layer(h, ln_w, mu, w_rkv, w_o, w0, w1, w2, a0, a1, a2, g1, g2, k_k, k_a, r_k, ln_gn_w, ln_gn_b):
    xs = _rwkv_prep(h, ln_w, mu)
    rkv = _bmm(xs[:3], w_rkv.astype(BF16))
    lw, a, g = _rwkv_lora(xs, w0, w1, w2, a0, a1, a2, g1, g2)
    z = _rwkv_scan(rkv, lw, a, g, k_k, k_a, r_k, ln_gn_w, ln_gn_b)
    return _mm_residual(z, w_o.astype(BF16), h)


def kernel(x, mem, ln_mix_w, ln_xattn_w, ln_mem_w, xattn_wq, xattn_wkv, xattn_wo, ln_ffn_w, router_group_w, router_group_b, router_expert_w, router_expert_b, moe_w_gate_up, moe_w_down, pool_w, pool_scale, rwkv_mu, rwkv_w_rkv, rwkv_w_o, rwkv_w0, rwkv_w1, rwkv_w2, rwkv_a0, rwkv_a1, rwkv_a2, rwkv_g1, rwkv_g2, rwkv_k_k, rwkv_k_a, rwkv_r_k, rwkv_ln_w, rwkv_ln_b, ln_out_w):
    B, S, D = x.shape
    depth = ln_mix_w.shape[0]
    outs = []
    for b in range(B):
        h = x[b]
        for i in range(depth):
            j = i // 2
            if i % 2 == 0:
                h = _pool_layer(h, ln_mix_w[i], pool_w[j], pool_scale[j])
            else:
                h = _rwkv_layer(h, ln_mix_w[i], rwkv_mu[j], rwkv_w_rkv[j], rwkv_w_o[j], rwkv_w0[j], rwkv_w1[j],
                                rwkv_w2[j], rwkv_a0[j], rwkv_a1[j], rwkv_a2[j], rwkv_g1[j], rwkv_g2[j],
                                rwkv_k_k[j], rwkv_k_a[j], rwkv_r_k[j].reshape(D), rwkv_ln_w[j], rwkv_ln_b[j])
            kv = _mem_kv(mem[b], ln_mem_w[i], xattn_wkv[i].astype(BF16))
            h = _xattn_layer(h, ln_xattn_w[i], xattn_wq[i].astype(BF16), kv[:, :D], kv[:, D:],
                             xattn_wo[i].astype(BF16))
            h = _moe_layer(h, ln_ffn_w[i], router_group_w[i], router_group_b[i], router_expert_w[i],
                           router_expert_b[i], moe_w_gate_up[i].astype(BF16), moe_w_down[i].astype(BF16),
                           ln_out_w, final_norm=(i == depth - 1))
        outs.append(h)
    return jnp.stack(outs, axis=0)
```

```python
import functools

import jax
import jax.numpy as jnp
from jax import lax
from jax.experimental import pallas as pl
from jax.experimental.pallas import tpu as pltpu

F32 = jnp.float32
BF16 = jnp.bfloat16

D_MODEL = 2048
POOL_WINDOWS = (2, 4, 8, 16)
POOL_GROUP_DIM = D_MODEL // len(POOL_WINDOWS)
MAX_WINDOW = max(POOL_WINDOWS)
RWKV_HEAD_DIM = 64
GN_EPS = 64e-5
XATTN_HEADS = 4
XATTN_HEAD_DIM = D_MODEL // XATTN_HEADS
N_GROUPS = 8
EXPERTS_PER_GROUP = 8
N_EXPERTS = N_GROUPS * EXPERTS_PER_GROUP
TOP_K = 2
D_EXPERT = D_MODEL // 8
ROW_BLOCK = 128
RMS_EPS = 1e-6

LANES = 128
SUBLANES = 8
VMEM_LIMIT = 56 * 1024 * 1024

CHUNK = 64
PAIR = 2 * RWKV_HEAD_DIM
SCAN_ROWS = 512
SCAN_LANES = 512

NN = (((1,), (0,)), ((), ()))
NT = (((1,), (1,)), ((), ()))


def _params(*sem):
    return pltpu.CompilerParams(dimension_semantics=sem, vmem_limit_bytes=VMEM_LIMIT)


def _rms(x, w):
    return x * lax.rsqrt(jnp.mean(x * x, axis=-1, keepdims=True) + RMS_EPS) * w


def _dot(a, b, dims=NN):
    return lax.dot_general(a, b, dims, preferred_element_type=F32)


def _split(x):
    hi = x.astype(BF16)
    return hi, (x - hi.astype(F32)).astype(BF16)


def _mm3(a, b, dims=NN):
    ah, al = _split(a)
    bh, bl = _split(b)
    return _dot(ah, bh, dims) + (_dot(ah, bl, dims) + _dot(al, bh, dims))


def _mm_exact_rhs(a, b_bf16, dims=NN):
    hi = a.astype(BF16)
    r1 = a - hi.astype(F32)
    mid = r1.astype(BF16)
    lo = (r1 - mid.astype(F32)).astype(BF16)
    return _dot(hi, b_bf16, dims) + (_dot(mid, b_bf16, dims) + _dot(lo, b_bf16, dims))


def _bmm_body(x_ref, w_ref, o_ref):
    o_ref[...] = _dot(x_ref[...], w_ref[...]).astype(o_ref.dtype)


def _bmm(xs, ws, tm=512, tn=1024):
    B, M, K = xs.shape
    N = ws.shape[2]
    return pl.pallas_call(
        _bmm_body,
        grid=(B, N // tn, M // tm),
        in_specs=[pl.BlockSpec((None, tm, K), lambda b, j, i: (b, i, 0)),
                  pl.BlockSpec((None, K, tn), lambda b, j, i: (b, 0, j))],
        out_specs=pl.BlockSpec((None, tm, tn), lambda b, j, i: (b, i, j)),
        out_shape=jax.ShapeDtypeStruct((B, M, N), F32),
        compiler_params=_params("arbitrary", "arbitrary", "arbitrary"),
        name="rwkv_rkv_proj",
    )(xs, ws)


def _mm_res_body(x_ref, w_ref, r_ref, o_ref):
    o_ref[...] = r_ref[...] + _dot(x_ref[...], w_ref[...])


def _mm_residual(x, w, res, tm=512, tn=1024):
    M, K = x.shape
    N = w.shape[1]
    return pl.pallas_call(
        _mm_res_body,
        grid=(N // tn, M // tm),
        in_specs=[pl.BlockSpec((tm, K), lambda j, i: (i, 0)),
                  pl.BlockSpec((K, tn), lambda j, i: (0, j)),
                  pl.BlockSpec((tm, tn), lambda j, i: (i, j))],
        out_specs=pl.BlockSpec((tm, tn), lambda j, i: (i, j)),
        out_shape=jax.ShapeDtypeStruct((M, N), F32),
        compiler_params=_params("arbitrary", "arbitrary"),
        name="rwkv_out_proj",
    )(x, w, res)


def _pool_body(h_ref, halo_ref, lnw_ref, pw_ref, ps_ref, o_ref):
    i = pl.program_id(0)
    ts = h_ref.shape[0]
    h = h_ref[...]
    lnw = lnw_ref[...]
    x = _rms(h, lnw)
    xh = _rms(halo_ref[...], lnw) * (i > 0).astype(F32)
    xe = jnp.concatenate([xh, x], axis=0)
    t = i * ts + lax.broadcasted_iota(jnp.int32, (ts, 1), 0)
    outs = []
    for g, win in enumerate(POOL_WINDOWS):
        sl = slice(g * POOL_GROUP_DIM, (g + 1) * POOL_GROUP_DIM)
        acc = xe[:, sl]
        span = 1
        while span < win:
            acc = acc + pltpu.roll(acc, span, axis=0)
            span *= 2
        cnt = jnp.minimum(t + 1, win).astype(F32)
        pooled = acc[MAX_WINDOW:, :] / cnt - x[:, sl]
        outs.append(_dot(pooled.astype(BF16), pw_ref[g]))
    y = jnp.concatenate(outs, axis=-1)
    o_ref[...] = h + y * ps_ref[...]


def _pool_layer(h, ln_w, pool_w, pool_scale, ts=512):
    S, D = h.shape
    G = pool_w.shape[0]
    halo_blocks = ts // MAX_WINDOW
    return pl.pallas_call(
        _pool_body,
        grid=(S // ts,),
        in_specs=[pl.BlockSpec((ts, D), lambda i: (i, 0)),
                  pl.BlockSpec((MAX_WINDOW, D), lambda i: (jnp.maximum(i * halo_blocks - 1, 0), 0)),
                  pl.BlockSpec((1, D), lambda i: (0, 0)),
                  pl.BlockSpec((G, POOL_GROUP_DIM, POOL_GROUP_DIM), lambda i: (0, 0, 0)),
                  pl.BlockSpec((1, D), lambda i: (0, 0))],
        out_specs=pl.BlockSpec((ts, D), lambda i: (i, 0)),
        out_shape=jax.ShapeDtypeStruct((S, D), F32),
        compiler_params=_params("arbitrary"),
        name="pool_layer",
    )(h, h, ln_w.reshape(1, D), pool_w.astype(BF16), pool_scale.reshape(1, D))


def _memkv_body(mem_ref, lnw_ref, w_ref, o_ref):
    mn = _rms(mem_ref[...], lnw_ref[...])
    o_ref[...] = _dot(mn.astype(BF16), w_ref[...]).astype(o_ref.dtype)


def _mem_kv(mem, ln_w, wkv, tn=1024):
    M, D = mem.shape
    N = wkv.shape[1]
    return pl.pallas_call(
        _memkv_body,
        grid=(N // tn,),
        in_specs=[pl.BlockSpec((M, D), lambda j: (0, 0)),
                  pl.BlockSpec((1, D), lambda j: (0, 0)),
                  pl.BlockSpec((D, tn), lambda j: (0, j))],
        out_specs=pl.BlockSpec((M, tn), lambda j: (0, j)),
        out_shape=jax.ShapeDtypeStruct((M, N), BF16),
        compiler_params=_params("arbitrary"),
        name="xattn_mem_kv",
    )(mem, ln_w.reshape(1, D), wkv)


def _xattn_body(h_ref, lnw_ref, wq_ref, k_ref, v_ref, wo_ref, o_ref):
    h = h_ref[...]
    hn = _rms(h, lnw_ref[...]).astype(BF16)
    q = _dot(hn, wq_ref[...])
    scale = XATTN_HEAD_DIM ** -0.5
    heads = []
    for hd in range(XATTN_HEADS):
        sl = slice(hd * XATTN_HEAD_DIM, (hd + 1) * XATTN_HEAD_DIM)
        s = _dot(q[:, sl].astype(BF16), k_ref[:, sl], NT) * scale
        s = s - jnp.max(s, axis=-1, keepdims=True)
        e = jnp.exp(s)
        p = e / jnp.sum(e, axis=-1, keepdims=True)
        heads.append(_dot(p.astype(BF16), v_ref[:, sl]))
    o = jnp.concatenate(heads, axis=-1).astype(BF16)
    o_ref[...] = h + _dot(o, wo_ref[...])


def _xattn_layer(h, ln_w, wq, k, v, wo, ts=256):
    S, D = h.shape
    M = k.shape[0]
    once = dict(pipeline_mode=pl.Buffered(1))
    return pl.pallas_call(
        _xattn_body,
        grid=(S // ts,),
        in_specs=[pl.BlockSpec((ts, D), lambda i: (i, 0)),
                  pl.BlockSpec((1, D), lambda i: (0, 0)),
                  pl.BlockSpec((D, D), lambda i: (0, 0), **once),
                  pl.BlockSpec((M, D), lambda i: (0, 0), **once),
                  pl.BlockSpec((M, D), lambda i: (0, 0), **once),
                  pl.BlockSpec((D, D), lambda i: (0, 0), **once)],
        out_specs=pl.BlockSpec((ts, D), lambda i: (i, 0)),
        out_shape=jax.ShapeDtypeStruct((S, D), F32),
        compiler_params=_params("arbitrary"),
        name="xattn_layer",
    )(h, ln_w.reshape(1, D), wq, k, v, wo)


def _router_body(h_ref, lnw_ref, whi_ref, wlo_ref, b_ref, hn_ref, r_ref):
    hn = _rms(h_ref[...], lnw_ref[...])
    hn_ref[...] = hn
    xh, xl = _split(hn)
    whi = whi_ref[...]
    lg = _dot(xh, whi) + (_dot(xh, wlo_ref[...]) + _dot(xl, whi)) + b_ref[...]
    ts = lg.shape[0]
    lane = lax.broadcasted_iota(jnp.int32, (ts, LANES), 1)
    lanef = lane.astype(F32)
    neg = jnp.float32(-jnp.inf)
    gl = jnp.where(lane < N_GROUPS, lg, neg)
    gmax = jnp.max(gl, axis=-1, keepdims=True)
    pg_top = 1.0 / jnp.sum(jnp.exp(gl - gmax), axis=-1, keepdims=True)
    g_idx = jnp.min(jnp.where(gl == gmax, lanef, float(LANES)), axis=-1, keepdims=True)
    lo_lane = g_idx * EXPERTS_PER_GROUP + N_GROUPS
    el = jnp.where(lanef >= lo_lane, jnp.where(lanef < lo_lane + EXPERTS_PER_GROUP, lg, neg), neg)
    m1 = jnp.max(el, axis=-1, keepdims=True)
    i1 = jnp.min(jnp.where(el == m1, lanef, float(LANES)), axis=-1, keepdims=True)
    el2 = jnp.where(lanef == i1, neg, el)
    m2 = jnp.max(el2, axis=-1, keepdims=True)
    i2 = jnp.min(jnp.where(el2 == m2, lanef, float(LANES)), axis=-1, keepdims=True)
    e2 = jnp.exp(m2 - m1)
    g1 = pg_top / (1.0 + e2)
    g2 = pg_top * e2 / (1.0 + e2)
    out = jnp.where(lane == 0, g1, 0.0)
    out = jnp.where(lane == 1, g2, out)
    out = jnp.where(lane == 2, i1 - N_GROUPS, out)
    out = jnp.where(lane == 3, i2 - N_GROUPS, out)
    r_ref[...] = out


def _router(h, ln_w, wg, bg, we, be, ts=512):
    S, D = h.shape
    w = jnp.zeros((D, LANES), F32).at[:, :N_GROUPS].set(wg).at[:, N_GROUPS:N_GROUPS + N_EXPERTS].set(we)
    b = jnp.zeros((1, LANES), F32).at[0, :N_GROUPS].set(bg).at[0, N_GROUPS:N_GROUPS + N_EXPERTS].set(be)
    whi, wlo = _split(w)
    return pl.pallas_call(
        _router_body,
        grid=(S // ts,),
        in_specs=[pl.BlockSpec((ts, D), lambda i: (i, 0)),
                  pl.BlockSpec((1, D), lambda i: (0, 0)),
                  pl.BlockSpec((D, LANES), lambda i: (0, 0)),
                  pl.BlockSpec((D, LANES), lambda i: (0, 0)),
                  pl.BlockSpec((1, LANES), lambda i: (0, 0))],
        out_specs=[pl.BlockSpec((ts, D), lambda i: (i, 0)),
                   pl.BlockSpec((ts, LANES), lambda i: (i, 0))],
        out_shape=[jax.ShapeDtypeStruct((S, D), F32), jax.ShapeDtypeStruct((S, LANES), F32)],
        compiler_params=_params("arbitrary"),
        name="moe_router",
    )(h, ln_w.reshape(1, D), whi, wlo, b)


def _row_copy(src_hbm, row, dst, slot, r, sem):
    return pltpu.make_async_copy(src_hbm.at[pl.ds(row, 1), :], dst.at[slot, pl.ds(r, 1), :], sem.at[slot])


def _start_row_gather(idx_ref, base, src_hbm, dst, slot, sem, n_rows):
    def issue(r, carry):
        _row_copy(src_hbm, idx_ref[base + r], dst, slot, r, sem).start()
        return carry
    lax.fori_loop(0, n_rows, issue, 0)


def _wait_row_gather(src_hbm, dst, slot, sem, n_rows):
    pltpu.make_async_copy(src_hbm.at[pl.ds(0, n_rows), :], dst.at[slot], sem.at[slot]).wait()


def _experts_body(nblk_ref, be_ref, tok_ref, hn_hbm, gate_ref, wgu_ref, wd_ref, o_ref, xbuf, sem):
    i = pl.program_id(0)
    n_used = nblk_ref[0]
    slot = i % 2

    @pl.when(i == 0)
    def _():
        _start_row_gather(tok_ref, 0, hn_hbm, xbuf, 0, sem, ROW_BLOCK)

    @pl.when(i + 1 < n_used)
    def _():
        _start_row_gather(tok_ref, (i + 1) * ROW_BLOCK, hn_hbm, xbuf, 1 - slot, sem, ROW_BLOCK)

    @pl.when(i < n_used)
    def _():
        _wait_row_gather(hn_hbm, xbuf, slot, sem, ROW_BLOCK)
        x = xbuf[slot].astype(BF16)
        gu = _dot(x, wgu_ref[...])
        gg = gu[:, :D_EXPERT]
        uu = gu[:, D_EXPERT:]
        act = (gg * (1.0 / (1.0 + jnp.exp(-gg))) * uu).astype(BF16)
        o_ref[...] = _dot(act, wd_ref[...]) * gate_ref[...]

    @pl.when(i >= n_used)
    def _():
        o_ref[...] = jnp.zeros_like(o_ref)


def _experts(n_used, block_e, row_tok, hn, row_gate, w_gu, w_down):
    n_rows = row_tok.shape[0]
    n_blocks = n_rows // ROW_BLOCK
    D = hn.shape[1]
    F2 = w_gu.shape[2]
    return pl.pallas_call(
        _experts_body,
        grid_spec=pltpu.PrefetchScalarGridSpec(
            num_scalar_prefetch=3,
            grid=(n_blocks,),
            in_specs=[pl.BlockSpec(memory_space=pl.ANY),
                      pl.BlockSpec((ROW_BLOCK, 1), lambda i, nb, be, tok: (i, 0)),
                      pl.BlockSpec((None, D, F2), lambda i, nb, be, tok: (be[i], 0, 0)),
                      pl.BlockSpec((None, F2 // 2, D), lambda i, nb, be, tok: (be[i], 0, 0))],
            out_specs=pl.BlockSpec((ROW_BLOCK, D), lambda i, nb, be, tok: (i, 0)),
            scratch_shapes=[pltpu.VMEM((2, ROW_BLOCK, D), F32), pltpu.SemaphoreType.DMA((2,))]),
        out_shape=jax.ShapeDtypeStruct((n_rows, D), F32),
        compiler_params=_params("arbitrary"),
        name="moe_experts",
    )(n_used, block_e, row_tok, hn, row_gate.reshape(n_rows, 1), w_gu, w_down)


def _combine_body(final_norm, pos_ref, yb_hbm, h_ref, lnw_ref, o_ref, ybuf, sem):
    i = pl.program_id(0)
    n = pl.num_programs(0)
    tb = h_ref.shape[0]
    slot = i % 2

    @pl.when(i == 0)
    def _():
        _start_row_gather(pos_ref, 0, yb_hbm, ybuf, 0, sem, TOP_K * tb)

    @pl.when(i + 1 < n)
    def _():
        _start_row_gather(pos_ref, (i + 1) * TOP_K * tb, yb_hbm, ybuf, 1 - slot, sem, TOP_K * tb)

    _wait_row_gather(yb_hbm, ybuf, slot, sem, TOP_K * tb)
    out = h_ref[...] + (ybuf[slot, :tb, :] + ybuf[slot, tb:, :])
    if final_norm:
        out = _rms(out, lnw_ref[...])
    o_ref[...] = out


def _combine(pos, yb, h, ln_out_w, final_norm, tb=128):
    S, D = h.shape
    return pl.pallas_call(
        functools.partial(_combine_body, final_norm),
        grid_spec=pltpu.PrefetchScalarGridSpec(
            num_scalar_prefetch=1,
            grid=(S // tb,),
            in_specs=[pl.BlockSpec(memory_space=pl.ANY),
                      pl.BlockSpec((tb, D), lambda i, pos: (i, 0)),
                      pl.BlockSpec((1, D), lambda i, pos: (0, 0))],
            out_specs=pl.BlockSpec((tb, D), lambda i, pos: (i, 0)),
            scratch_shapes=[pltpu.VMEM((2, TOP_K * tb, D), F32), pltpu.SemaphoreType.DMA((2,))]),
        out_shape=jax.ShapeDtypeStruct((S, D), F32),
        compiler_params=_params("arbitrary"),
        name="moe_combine",
    )(pos, yb, h, ln_out_w.reshape(1, D))


def _moe_layer(h, ln_w, wg, bg, we, be, w_gu, w_down, ln_out_w, final_norm, tb=128):
    T, D = h.shape
    hn, routed = _router(h, ln_w, wg, bg, we, be)
    gate = routed[:, :TOP_K]
    expert_idx = routed[:, TOP_K:2 * TOP_K].astype(jnp.int32)
    A = T * TOP_K
    n_blocks = -(-A // ROW_BLOCK) + N_EXPERTS
    n_rows = n_blocks * ROW_BLOCK
    flat_e = expert_idx.reshape(A)
    flat_tok = jnp.repeat(jnp.arange(T, dtype=jnp.int32), TOP_K)
    order = jnp.argsort(flat_e)
    e_sorted = flat_e[order]
    counts = jnp.bincount(flat_e, length=N_EXPERTS).astype(jnp.int32)
    padded = (counts + ROW_BLOCK - 1) // ROW_BLOCK * ROW_BLOCK
    pad_end = jnp.cumsum(padded)
    pad_start = pad_end - padded
    start = jnp.cumsum(counts) - counts
    dest = pad_start[e_sorted] + jnp.arange(A, dtype=jnp.int32) - start[e_sorted]
    row_tok = jnp.zeros((n_rows,), jnp.int32).at[dest].set(flat_tok[order])
    row_gate = jnp.zeros((n_rows,), F32).at[dest].set(gate.reshape(A)[order])
    block_start = jnp.arange(n_blocks, dtype=jnp.int32) * ROW_BLOCK
    block_e = jnp.minimum(jnp.searchsorted(pad_end, block_start, side='right'), N_EXPERTS - 1).astype(jnp.int32)
    n_used = (pad_end[-1] // ROW_BLOCK).astype(jnp.int32).reshape(1)
    yb = _experts(n_used, block_e, row_tok, hn, row_gate, w_gu, w_down)
    pos = jnp.zeros((A,), jnp.int32).at[order].set(dest)
    pos = pos.reshape(T // tb, tb, TOP_K).transpose(0, 2, 1).reshape(A)
    return _combine(pos, yb, h, ln_out_w, final_norm, tb)


def _rwkv_prep_body(h_ref, halo_ref, lnw_ref, mu_ref, xs_ref):
    i = pl.program_id(0)
    lnw = lnw_ref[...]
    hn = _rms(h_ref[...], lnw)
    ts = hn.shape[0]
    last = _rms(halo_ref[...], lnw)[SUBLANES - 1:SUBLANES, :] * (i > 0).astype(F32)
    row = lax.broadcasted_iota(jnp.int32, (ts, 1), 0)
    prev = jnp.where(row == 0, last, pltpu.roll(hn, 1, axis=0))
    xx = prev - hn
    for n in range(xs_ref.shape[0]):
        xs_ref[n] = (hn + xx * mu_ref[n:n + 1, :]).astype(xs_ref.dtype)


def _rwkv_prep(h, ln_w, mu, ts=256):
    S, D = h.shape
    n_mix = mu.shape[0]
    halo_blocks = ts // SUBLANES
    return pl.pallas_call(
        _rwkv_prep_body,
        grid=(S // ts,),
        in_specs=[pl.BlockSpec((ts, D), lambda i: (i, 0)),
                  pl.BlockSpec((SUBLANES, D), lambda i: (jnp.maximum(i * halo_blocks - 1, 0), 0)),
                  pl.BlockSpec((1, D), lambda i: (0, 0)),
                  pl.BlockSpec((n_mix, D), lambda i: (0, 0))],
        out_specs=pl.BlockSpec((n_mix, ts, D), lambda i: (0, i, 0)),
        out_shape=jax.ShapeDtypeStruct((n_mix, S, D), BF16),
        compiler_params=_params("arbitrary"),
        name="rwkv_prep",
    )(h, h, ln_w.reshape(1, D), mu)


def _sigmoid(x):
    return 1.0 / (1.0 + jnp.exp(-x))


def _rwkv_lora_body(xw_ref, xa_ref, xg_ref, w0_ref, w1_ref, w2_ref, a0_ref, a1_ref, a2_ref, g1_ref, g2_ref,
                    lw_ref, a_ref, g_ref):
    z = w0_ref[...] + _dot(jnp.tanh(_dot(xw_ref[...], w1_ref[...])).astype(BF16), w2_ref[...])
    u = -z
    softplus = jnp.maximum(u, 0.0) + jnp.log1p(jnp.exp(-jnp.abs(u)))
    lw_ref[...] = -jnp.exp(-softplus - 0.5)
    a_ref[...] = _sigmoid(a0_ref[...] + _dot(_dot(xa_ref[...], a1_ref[...]).astype(BF16), a2_ref[...]))
    g_ref[...] = _dot(_sigmoid(_dot(xg_ref[...], g1_ref[...])).astype(BF16), g2_ref[...])


def _rwkv_lora(xs, w0, w1, w2, a0, a1, a2, g1, g2, ts=256):
    _, S, D = xs.shape

    def pad_lora(wa, wb):
        r = wa.shape[1]
        rp = -(-r // LANES) * LANES
        return (jnp.pad(wa, ((0, 0), (0, rp - r))).astype(BF16), jnp.pad(wb, ((0, rp - r), (0, 0))).astype(BF16))

    w1p, w2p = pad_lora(w1, w2)
    a1p, a2p = pad_lora(a1, a2)
    g1p, g2p = pad_lora(g1, g2)
    full = lambda arr: pl.BlockSpec(arr.shape, lambda i: (0,) * arr.ndim)
    vec = pl.BlockSpec((1, D), lambda i: (0, 0))
    row = pl.BlockSpec((ts, D), lambda i: (i, 0))
    return pl.pallas_call(
        _rwkv_lora_body,
        grid=(S // ts,),
        in_specs=[pl.BlockSpec((None, ts, D), lambda i: (3, i, 0)),
                  pl.BlockSpec((None, ts, D), lambda i: (4, i, 0)),
                  pl.BlockSpec((None, ts, D), lambda i: (5, i, 0)),
                  vec, full(w1p), full(w2p), vec, full(a1p), full(a2p), full(g1p), full(g2p)],
        out_specs=[row, row, row],
        out_shape=[jax.ShapeDtypeStruct((S, D), F32)] * 3,
        compiler_params=_params("arbitrary"),
        name="rwkv_lora",
    )(xs, xs, xs, w0.reshape(1, D), w1p, w2p, a0.reshape(1, D), a1p, a2p, g1p, g2p)


def _scan_chunk(r, k, v, lw, a, g, kkw, kaw, rkw, lnw, lnb, state, c):
    m0, m1, seg_f, seg_b, tri, strict, incl, eye2 = c
    stack = lambda x: jnp.concatenate([x * m0, x * m1], axis=0)
    rows = lambda *xs: jnp.concatenate(xs, axis=0)
    cols = lambda *xs: jnp.concatenate(xs, axis=1)

    kk = k * kkw
    k2 = k * (1.0 + (a - 1.0) * kaw)
    sums = _segsum(rows(kk * kk, r * k2 * rkw), seg_b)
    cw = _cumsum_rows(tri, lw)
    yield
    kk = kk / jnp.maximum(jnp.sqrt(sums[:CHUNK]), 1e-12)
    bonus = sums[CHUNK:] * v
    bv = kk * a
    cl = cw[CHUNK - 1:CHUNK, :]
    at = -kk * jnp.exp(cw - lw)
    rt = r * jnp.exp(cw)
    dinv = jnp.exp(-cw)
    drem = jnp.exp(cl - cw)
    vs = stack(v)
    gram = _mm_p_nt(rows(at, rt), rows(stack(bv * dinv), stack(k2 * dinv)))
    bk_t = rows(bv * drem, k2 * drem).T
    decay_rows = jnp.broadcast_to(jnp.exp(cl), (PAIR, PAIR)).T
    yield
    a_ab = jnp.where(strict, gram[:CHUNK, :PAIR], 0.0)
    a_ak = jnp.where(strict, gram[:CHUNK, PAIR:], 0.0)
    a_rb = jnp.where(incl, gram[CHUNK:, :PAIR], 0.0)
    a_rk = jnp.where(incl, gram[CHUNK:, PAIR:], 0.0)
    inv = eye2 + a_ab
    pw = _mm_p(a_ab, stack(a_ab))
    xy = _mm_p(rows(cols(at, a_ak), cols(rt, a_rk)), rows(state, vs))
    yield
    span = 2
    while span < CHUNK // 2:
        both = _mm_p(rows(inv, pw), stack(pw))
        yield
        inv = inv + both[:CHUNK]
        pw = both[CHUNK:]
        span *= 2
    inv = inv + _mm_p(inv, stack(pw))
    yield
    u = _mm_p(inv, stack(xy[:CHUNK]))
    yield
    y = xy[CHUNK:] + _mm_p(a_rb, stack(u))
    new_state = state * decay_rows + seg_f * _mm_p(bk_t, rows(u, v))
    yield
    inv_n = 1.0 / RWKV_HEAD_DIM
    mean = _segsum(y, seg_b) * inv_n
    yield
    yc = y - mean
    var = _segsum(yc * yc, seg_b) * inv_n
    yield
    yn = yc * lax.rsqrt(var + GN_EPS) * lnw + lnb
    return ((yn + bonus) * g), new_state


def _run_interleaved(gens):
    results = [None] * len(gens)
    live = list(range(len(gens)))
    while live:
        for i in list(live):
            try:
                next(gens[i])
            except StopIteration as done:
                results[i] = done.value
                live.remove(i)
    return results


def _pack_lhs(a):
    hi, lo = _split(a)
    return jnp.concatenate([hi, lo], axis=1)


def _fold(x):
    n = x.shape[1] // 2
    return x[:, :n] + x[:, n:]


def _mm_p(a, b):
    hi, lo = _split(b)
    row = jnp.concatenate([hi, lo], axis=1)
    return _fold(_dot(_pack_lhs(a), jnp.concatenate([row, row], axis=0)))


def _mm_p_nt(a, b):
    hi, lo = _split(b)
    packed = jnp.concatenate([jnp.concatenate([hi, hi], axis=1), jnp.concatenate([lo, lo], axis=1)], axis=0)
    return _fold(_dot(_pack_lhs(a), packed, NT))


def _segsum(x, seg_b):
    hi = x.astype(BF16)
    mid = (x - hi.astype(F32)).astype(BF16)
    return _dot(jnp.concatenate([hi, mid], axis=1), jnp.concatenate([seg_b, seg_b], axis=0))


def _cumsum_rows(tri, x):
    hi = x.astype(BF16)
    r1 = x - hi.astype(F32)
    mid = r1.astype(BF16)
    lo = (r1 - mid.astype(F32)).astype(BF16)
    return _dot(jnp.concatenate([tri, tri, tri], axis=1), jnp.concatenate([hi, mid, lo], axis=0))


def _scan_consts():
    lane = lax.broadcasted_iota(jnp.int32, (1, PAIR), 1)
    m0 = (lane < RWKV_HEAD_DIM).astype(F32)
    m1 = 1.0 - m0
    ri = lax.broadcasted_iota(jnp.int32, (PAIR, PAIR), 0)
    ci = lax.broadcasted_iota(jnp.int32, (PAIR, PAIR), 1)
    head_r = (ri >= RWKV_HEAD_DIM).astype(F32)
    head_c = (ci >= RWKV_HEAD_DIM).astype(F32)
    seg_f = head_r * head_c + (1.0 - head_r) * (1.0 - head_c)
    tri = (lax.broadcasted_iota(jnp.int32, (CHUNK, CHUNK), 1)
           <= lax.broadcasted_iota(jnp.int32, (CHUNK, CHUNK), 0)).astype(F32).astype(BF16)
    t = lax.broadcasted_iota(jnp.int32, (CHUNK, PAIR), 0)
    s = lax.broadcasted_iota(jnp.int32, (CHUNK, PAIR), 1) & (RWKV_HEAD_DIM - 1)
    return (m0, m1, seg_f, seg_f.astype(BF16), tri, s < t, s <= t, (s == t).astype(F32))


def _scan_body(r_ref, k_ref, v_ref, lw_ref, a_ref, g_ref, kkw_ref, kaw_ref, rkw_ref, lnw_ref, lnb_ref,
               o_ref, state_ref):
    @pl.when(pl.program_id(1) == 0)
    def _():
        state_ref[...] = jnp.zeros_like(state_ref)

    n_pairs = o_ref.shape[1] // PAIR

    def chunk_step(j, carry):
        consts = _scan_consts()
        rows = pl.ds(pl.multiple_of(j * CHUNK, CHUNK), CHUNK)
        lanes = [slice(p * PAIR, (p + 1) * PAIR) for p in range(n_pairs)]
        args = [(r_ref[rows, ln], k_ref[rows, ln], v_ref[rows, ln], lw_ref[rows, ln], a_ref[rows, ln],
                 g_ref[rows, ln], kkw_ref[:, ln], kaw_ref[:, ln], rkw_ref[:, ln], lnw_ref[:, ln],
                 lnb_ref[:, ln], state_ref[p]) for p, ln in enumerate(lanes)]
        results = _run_interleaved([_scan_chunk(*a, consts) for a in args])
        for p, (ln, (out, new_state)) in enumerate(zip(lanes, results)):
            state_ref[p] = new_state
            o_ref[rows, ln] = out.astype(o_ref.dtype)
        return carry

    lax.fori_loop(0, o_ref.shape[0] // CHUNK, chunk_step, 0)


def _rwkv_scan(rkv, lw, a, g, k_k, k_a, r_k, ln_w, ln_b):
    _, S, D = rkv.shape
    tr, tl = min(SCAN_ROWS, S), SCAN_LANES
    blk = lambda n: pl.BlockSpec((None, tr, tl), lambda p, i: (n, i, p))
    row = pl.BlockSpec((tr, tl), lambda p, i: (i, p))
    vec = pl.BlockSpec((1, tl), lambda p, i: (0, p))
    as_row = lambda w: w.reshape(1, D)
    return pl.pallas_call(
        _scan_body,
        grid=(D // tl, S // tr),
        in_specs=[blk(0), blk(1), blk(2), row, row, row, vec, vec, vec, vec, vec],
        out_specs=row,
        out_shape=jax.ShapeDtypeStruct((S, D), BF16),
        scratch_shapes=[pltpu.VMEM((tl // PAIR, PAIR, PAIR), F32)],
        compiler_params=_params("arbitrary", "arbitrary"),
        name="rwkv_scan",
    )(rkv, rkv, rkv, lw, a, g, as_row(k_k), as_row(k_a), as_row(r_k), as_row(ln_w), as_row(ln_b))


def _rwkv_layer(h, ln_w, mu, w_rkv, w_o, w0, w1, w2, a0, a1, a2, g1, g2, k_k, k_a, r_k, ln_gn_w, ln_gn_b):
    xs = _rwkv_prep(h, ln_w, mu)
    rkv = _bmm(xs[:3], w_rkv.astype(BF16))
    lw, a, g = _rwkv_lora(xs, w0, w1, w2, a0, a1, a2, g1, g2)
    z = _rwkv_scan(rkv, lw, a, g, k_k, k_a, r_k, ln_gn_w, ln_gn_b)
    return _mm_residual(z, w_o.astype(BF16), h)


def kernel(x, mem, ln_mix_w, ln_xattn_w, ln_mem_w, xattn_wq, xattn_wkv, xattn_wo, ln_ffn_w, router_group_w, router_group_b, router_expert_w, router_expert_b, moe_w_gate_up, moe_w_down, pool_w, pool_scale, rwkv_mu, rwkv_w_rkv, rwkv_w_o, rwkv_w0, rwkv_w1, rwkv_w2, rwkv_a0, rwkv_a1, rwkv_a2, rwkv_g1, rwkv_g2, rwkv_k_k, rwkv_k_a, rwkv_r_k, rwkv_ln_w, rwkv_ln_b, ln_out_w):
    B, S, D = x.shape
    depth = ln_mix_w.shape[0]
    outs = []
    for b in range(B):
        h = x[b]
        for i in range(depth):
            j = i // 2
            if i % 2 == 0:
                h = _pool_layer(h, ln_mix_w[i], pool_w[j], pool_scale[j])
            else:
                h = _rwkv_layer(h, ln_mix_w[i], rwkv_mu[j], rwkv_w_rkv[j], rwkv_w_o[j], rwkv_w0[j], rwkv_w1[j],
                                rwkv_w2[j], rwkv_a0[j], rwkv_a1[j], rwkv_a2[j], rwkv_g1[j], rwkv_g2[j],
                                rwkv_k_k[j], rwkv_k_a[j], rwkv_r_k[j].reshape(D), rwkv_ln_w[j], rwkv_ln_b[j])
            kv = _mem_kv(mem[b], ln_mem_w[i], xattn_wkv[i].astype(BF16))
            h = _xattn_layer(h, ln_xattn_w[i], xattn_wq[i].astype(BF16), kv[:, :D], kv[:, D:],
                             xattn_wo[i].astype(BF16))
            h = _moe_layer(h, ln_ffn_w[i], router_group_w[i], router_group_b[i], router_expert_w[i],
                           router_expert_b[i], moe_w_gate_up[i].astype(BF16), moe_w_down[i].astype(BF16),
                           ln_out_w, final_norm=(i == depth - 1))
        outs.append(h)
    return jnp.stack(outs, axis=0)
```

```python
import functools

import jax
import jax.numpy as jnp
from jax import lax
from jax.experimental import pallas as pl
from jax.experimental.pallas import tpu as pltpu

F32 = jnp.float32
BF16 = jnp.bfloat16

D_MODEL = 2048
POOL_WINDOWS = (2, 4, 8, 16)
POOL_GROUP_DIM = D_MODEL // len(POOL_WINDOWS)
MAX_WINDOW = max(POOL_WINDOWS)
RWKV_HEAD_DIM = 64
GN_EPS = 64e-5
XATTN_HEADS = 4
XATTN_HEAD_DIM = D_MODEL // XATTN_HEADS
N_GROUPS = 8
EXPERTS_PER_GROUP = 8
N_EXPERTS = N_GROUPS * EXPERTS_PER_GROUP
TOP_K = 2
D_EXPERT = D_MODEL // 8
ROW_BLOCK = 128
RMS_EPS = 1e-6

LANES = 128
SUBLANES = 8
VMEM_LIMIT = 56 * 1024 * 1024

CHUNK = 64
PAIR = 2 * RWKV_HEAD_DIM
WEIGHT_STAGE_ROWS = 256
SCAN_ROWS = 512
SCAN_LANES = 512

NN = (((1,), (0,)), ((), ()))
NT = (((1,), (1,)), ((), ()))


def _params(*sem):
    return pltpu.CompilerParams(dimension_semantics=sem, vmem_limit_bytes=VMEM_LIMIT)


def _rms(x, w):
    return x * lax.rsqrt(jnp.mean(x * x, axis=-1, keepdims=True) + RMS_EPS) * w


def _dot(a, b, dims=NN):
    return lax.dot_general(a, b, dims, preferred_element_type=F32)


def _split(x):
    hi = x.astype(BF16)
    return hi, (x - hi.astype(F32)).astype(BF16)


def _rkv_proj_body(x_ref, w_ref, o_ref, wb_ref):
    @pl.when(pl.program_id(2) == 0)
    def _():
        wb_ref[...] = w_ref[...].astype(BF16)

    o_ref[...] = _dot(x_ref[...], wb_ref[...])


def _rkv_proj(xs, w_rkv, j, tm=512, tn=1024):
    _, M, K = xs.shape
    B, N = w_rkv.shape[1], w_rkv.shape[3]
    return pl.pallas_call(
        _rkv_proj_body,
        grid=(B, N // tn, M // tm),
        in_specs=[pl.BlockSpec((None, tm, K), lambda b, n, i: (b, i, 0)),
                  pl.BlockSpec((None, None, K, tn), lambda b, n, i: (j, b, 0, n))],
        out_specs=pl.BlockSpec((None, tm, tn), lambda b, n, i: (b, i, n)),
        out_shape=jax.ShapeDtypeStruct((B, M, N), F32),
        scratch_shapes=[pltpu.VMEM((K, tn), BF16)],
        compiler_params=_params("arbitrary", "arbitrary", "arbitrary"),
        name="rwkv_rkv_proj",
    )(xs, w_rkv)


def _out_proj_body(x_ref, w_ref, r_ref, o_ref, wb_ref):
    @pl.when(pl.program_id(1) == 0)
    def _():
        wb_ref[...] = w_ref[...].astype(BF16)

    o_ref[...] = r_ref[...] + _dot(x_ref[...], wb_ref[...])


def _out_proj(x, w_o, j, res, tm=512, tn=1024):
    M, K = x.shape
    N = w_o.shape[2]
    return pl.pallas_call(
        _out_proj_body,
        grid=(N // tn, M // tm),
        in_specs=[pl.BlockSpec((tm, K), lambda n, i: (i, 0)),
                  pl.BlockSpec((None, K, tn), lambda n, i: (j, 0, n)),
                  pl.BlockSpec((tm, tn), lambda n, i: (i, n))],
        out_specs=pl.BlockSpec((tm, tn), lambda n, i: (i, n)),
        out_shape=jax.ShapeDtypeStruct((M, N), F32),
        scratch_shapes=[pltpu.VMEM((K, tn), BF16)],
        compiler_params=_params("arbitrary", "arbitrary"),
        name="rwkv_out_proj",
    )(x, w_o, res)


def _pool_body(h_ref, halo_ref, lnw_ref, pw_ref, ps_ref, o_ref):
    i = pl.program_id(0)
    ts = h_ref.shape[0]
    h = h_ref[...]
    lnw = lnw_ref[...]
    x = _rms(h, lnw)
    xh = _rms(halo_ref[...], lnw) * (i > 0).astype(F32)
    xe = jnp.concatenate([xh, x], axis=0)
    t = i * ts + lax.broadcasted_iota(jnp.int32, (ts, 1), 0)
    outs = []
    for g, win in enumerate(POOL_WINDOWS):
        sl = slice(g * POOL_GROUP_DIM, (g + 1) * POOL_GROUP_DIM)
        acc = xe[:, sl]
        span = 1
        while span < win:
            acc = acc + pltpu.roll(acc, span, axis=0)
            span *= 2
        cnt = jnp.minimum(t + 1, win).astype(F32)
        pooled = acc[MAX_WINDOW:, :] / cnt - x[:, sl]
        outs.append(_dot(pooled.astype(BF16), pw_ref[g]))
    y = jnp.concatenate(outs, axis=-1)
    o_ref[...] = h + y * ps_ref[...]


def _pool_layer(h, ln_w, pool_w, pool_scale, ts=512):
    S, D = h.shape
    G = pool_w.shape[0]
    halo_blocks = ts // MAX_WINDOW
    return pl.pallas_call(
        _pool_body,
        grid=(S // ts,),
        in_specs=[pl.BlockSpec((ts, D), lambda i: (i, 0)),
                  pl.BlockSpec((MAX_WINDOW, D), lambda i: (jnp.maximum(i * halo_blocks - 1, 0), 0)),
                  pl.BlockSpec((1, D), lambda i: (0, 0)),
                  pl.BlockSpec((G, POOL_GROUP_DIM, POOL_GROUP_DIM), lambda i: (0, 0, 0)),
                  pl.BlockSpec((1, D), lambda i: (0, 0))],
        out_specs=pl.BlockSpec((ts, D), lambda i: (i, 0)),
        out_shape=jax.ShapeDtypeStruct((S, D), F32),
        compiler_params=_params("arbitrary"),
        name="pool_layer",
    )(h, h, ln_w.reshape(1, D), pool_w.astype(BF16), pool_scale.reshape(1, D))


def _memkv_body(mem_ref, lnw_ref, w_ref, o_ref):
    mn = _rms(mem_ref[...], lnw_ref[...])
    o_ref[...] = _dot(mn.astype(BF16), w_ref[...].astype(BF16)).astype(o_ref.dtype)


def _mem_kv(mem, ln_w, wkv, layer, tn=1024):
    M, D = mem.shape
    N = wkv.shape[2]
    return pl.pallas_call(
        _memkv_body,
        grid=(N // tn,),
        in_specs=[pl.BlockSpec((M, D), lambda n: (0, 0)),
                  pl.BlockSpec((1, D), lambda n: (0, 0)),
                  pl.BlockSpec((None, D, tn), lambda n: (layer, 0, n))],
        out_specs=pl.BlockSpec((M, tn), lambda n: (0, n)),
        out_shape=jax.ShapeDtypeStruct((M, N), BF16),
        compiler_params=_params("arbitrary"),
        name="xattn_mem_kv",
    )(mem, ln_w.reshape(1, D), wkv)


def _load_weight_bf16(w_hbm, dst, stage, sem):
    rows = stage.shape[1]
    n_chunks = w_hbm.shape[0] // rows

    def chunk_copy(c):
        return pltpu.make_async_copy(w_hbm.at[pl.ds(c * rows, rows), :], stage.at[c % 2], sem.at[c % 2])

    chunk_copy(0).start()
    for c in range(n_chunks):
        if c + 1 < n_chunks:
            chunk_copy(c + 1).start()
        chunk_copy(c).wait()
        dst[pl.ds(c * rows, rows), :] = stage[c % 2].astype(BF16)


def _xattn_body(layer, h_ref, lnw_ref, k_ref, v_ref, wq_hbm, wo_hbm, o_ref, wq_ref, wo_ref, stage, sem):
    @pl.when(pl.program_id(0) == 0)
    def _():
        _load_weight_bf16(wq_hbm.at[layer], wq_ref, stage, sem)
        _load_weight_bf16(wo_hbm.at[layer], wo_ref, stage, sem)

    h = h_ref[...]
    hn = _rms(h, lnw_ref[...]).astype(BF16)
    q = _dot(hn, wq_ref[...])
    scale = XATTN_HEAD_DIM ** -0.5
    heads = []
    for hd in range(XATTN_HEADS):
        sl = slice(hd * XATTN_HEAD_DIM, (hd + 1) * XATTN_HEAD_DIM)
        s = _dot(q[:, sl].astype(BF16), k_ref[:, sl], NT) * scale
        s = s - jnp.max(s, axis=-1, keepdims=True)
        e = jnp.exp(s)
        p = e / jnp.sum(e, axis=-1, keepdims=True)
        heads.append(_dot(p.astype(BF16), v_ref[:, sl]))
    o = jnp.concatenate(heads, axis=-1).astype(BF16)
    o_ref[...] = h + _dot(o, wo_ref[...])


def _xattn_layer(h, ln_w, wq, kv, wo, layer, ts=256):
    S, D = h.shape
    M = kv.shape[0]
    return pl.pallas_call(
        functools.partial(_xattn_body, layer),
        grid=(S // ts,),
        in_specs=[pl.BlockSpec((ts, D), lambda i: (i, 0)),
                  pl.BlockSpec((1, D), lambda i: (0, 0)),
                  pl.BlockSpec((M, D), lambda i: (0, 0)),
                  pl.BlockSpec((M, D), lambda i: (0, 1)),
                  pl.BlockSpec(memory_space=pl.ANY),
                  pl.BlockSpec(memory_space=pl.ANY)],
        out_specs=pl.BlockSpec((ts, D), lambda i: (i, 0)),
        out_shape=jax.ShapeDtypeStruct((S, D), F32),
        scratch_shapes=[pltpu.VMEM((D, D), BF16), pltpu.VMEM((D, D), BF16),
                        pltpu.VMEM((2, WEIGHT_STAGE_ROWS, D), F32), pltpu.SemaphoreType.DMA((2,))],
        compiler_params=_params("arbitrary"),
        name="xattn_layer",
    )(h, ln_w.reshape(1, D), kv, kv, wq, wo)


def _router_body(h_ref, lnw_ref, whi_ref, wlo_ref, b_ref, r_ref, cnt_ref):
    i = pl.program_id(0)

    @pl.when(i == 0)
    def _():
        cnt_ref[...] = jnp.zeros_like(cnt_ref)

    hn = _rms(h_ref[...], lnw_ref[...])
    xh, xl = _split(hn)
    whi = whi_ref[...]
    lg = _dot(xh, whi) + (_dot(xh, wlo_ref[...]) + _dot(xl, whi)) + b_ref[...]
    ts = lg.shape[0]
    lane = lax.broadcasted_iota(jnp.int32, (ts, LANES), 1)
    lanef = lane.astype(F32)
    neg = jnp.float32(-jnp.inf)
    gl = jnp.where(lane < N_GROUPS, lg, neg)
    gmax = jnp.max(gl, axis=-1, keepdims=True)
    pg_top = 1.0 / jnp.sum(jnp.exp(gl - gmax), axis=-1, keepdims=True)
    g_idx = jnp.min(jnp.where(gl == gmax, lanef, float(LANES)), axis=-1, keepdims=True)
    lo_lane = g_idx * EXPERTS_PER_GROUP + N_GROUPS
    el = jnp.where(lanef >= lo_lane, jnp.where(lanef < lo_lane + EXPERTS_PER_GROUP, lg, neg), neg)
    m1 = jnp.max(el, axis=-1, keepdims=True)
    i1 = jnp.min(jnp.where(el == m1, lanef, float(LANES)), axis=-1, keepdims=True)
    el2 = jnp.where(lanef == i1, neg, el)
    m2 = jnp.max(el2, axis=-1, keepdims=True)
    i2 = jnp.min(jnp.where(el2 == m2, lanef, float(LANES)), axis=-1, keepdims=True)
    e2 = jnp.exp(m2 - m1)
    g1 = pg_top / (1.0 + e2)
    g2 = pg_top * e2 / (1.0 + e2)
    hit1 = (lanef == i1).astype(F32)
    hit2 = (lanef == i2).astype(F32)
    hits = hit1 + hit2
    earlier = (lax.broadcasted_iota(jnp.int32, (ts, ts), 1)
               < lax.broadcasted_iota(jnp.int32, (ts, ts), 0)).astype(F32).astype(BF16)
    before = _dot(earlier, hits.astype(BF16)) + cnt_ref[...]
    r1 = jnp.sum(before * hit1, axis=-1, keepdims=True)
    r2 = jnp.sum(before * hit2, axis=-1, keepdims=True)
    cnt_ref[...] += jnp.sum(hits, axis=0, keepdims=True)
    out = jnp.where(lane == 0, g1, 0.0)
    out = jnp.where(lane == 1, g2, out)
    out = jnp.where(lane == 2, i1 - N_GROUPS, out)
    out = jnp.where(lane == 3, i2 - N_GROUPS, out)
    out = jnp.where(lane == 4, r1, out)
    out = jnp.where(lane == 5, r2, out)
    r_ref[...] = out


def _router(h, ln_w, wg, bg, we, be, ts=512):
    S, D = h.shape
    w = jnp.zeros((D, LANES), F32).at[:, :N_GROUPS].set(wg).at[:, N_GROUPS:N_GROUPS + N_EXPERTS].set(we)
    b = jnp.zeros((1, LANES), F32).at[0, :N_GROUPS].set(bg).at[0, N_GROUPS:N_GROUPS + N_EXPERTS].set(be)
    whi, wlo = _split(w)
    return pl.pallas_call(
        _router_body,
        grid=(S // ts,),
        in_specs=[pl.BlockSpec((ts, D), lambda i: (i, 0)),
                  pl.BlockSpec((1, D), lambda i: (0, 0)),
                  pl.BlockSpec((D, LANES), lambda i: (0, 0)),
                  pl.BlockSpec((D, LANES), lambda i: (0, 0)),
                  pl.BlockSpec((1, LANES), lambda i: (0, 0))],
        out_specs=[pl.BlockSpec((ts, LANES), lambda i: (i, 0)),
                   pl.BlockSpec((1, LANES), lambda i: (0, 0))],
        out_shape=[jax.ShapeDtypeStruct((S, LANES), F32), jax.ShapeDtypeStruct((1, LANES), F32)],
        compiler_params=_params("arbitrary"),
        name="moe_router",
    )(h, ln_w.reshape(1, D), whi, wlo, b)


def _dispatch_body(pos_ref, fill_lo_ref, fill_hi_ref, nblk_ref, h_ref, lnw_ref, xs_hbm, xn, zeros, sem, zsem):
    i = pl.program_id(0)
    n = pl.num_programs(0)
    tb = h_ref.shape[0]
    slot = i % 2

    def row_copy(r, dst_row):
        return pltpu.make_async_copy(xn.at[slot, pl.ds(r, 1), :], xs_hbm.at[pl.ds(dst_row, 1), :], sem.at[slot])

    def wait_slot(s):
        for _ in range(TOP_K):
            pltpu.make_async_copy(xn.at[s], xs_hbm.at[pl.ds(0, tb), :], sem.at[s]).wait()

    @pl.when(i >= 2)
    def _():
        wait_slot(slot)

    xn[slot] = _rms(h_ref[...], lnw_ref[...])

    def issue(r, carry):
        base = (i * tb + r) * TOP_K
        for k in range(TOP_K):
            row_copy(r, pos_ref[base + k]).start()
        return carry
    lax.fori_loop(0, tb, issue, 0)

    @pl.when(i == n - 1)
    def _():
        zeros[...] = jnp.zeros_like(zeros)

        def zero_row(row):
            return pltpu.make_async_copy(zeros.at[pl.ds(0, 1), :], xs_hbm.at[pl.ds(row, 1), :], zsem)

        def zero_block(b):
            return pltpu.make_async_copy(zeros, xs_hbm.at[pl.ds(b * ROW_BLOCK, ROW_BLOCK), :], zsem)

        def per_expert(e, carry):
            lo, hi = fill_lo_ref[e], fill_hi_ref[e]
            lax.fori_loop(lo, hi, lambda row, c: (zero_row(row).start(), c)[1], 0)
            lax.fori_loop(lo, hi, lambda row, c: (zero_row(row).wait(), c)[1], 0)
            return carry
        lax.fori_loop(0, N_EXPERTS, per_expert, 0)
        n_blocks = xs_hbm.shape[0] // ROW_BLOCK
        lax.fori_loop(nblk_ref[0], n_blocks, lambda b, c: (zero_block(b).start(), c)[1], 0)
        lax.fori_loop(nblk_ref[0], n_blocks, lambda b, c: (zero_block(b).wait(), c)[1], 0)
        wait_slot(slot)

        @pl.when(n >= 2)
        def _():
            wait_slot(1 - slot)


def _dispatch(pos, fill_lo, fill_hi, n_used, h, ln_w, n_rows, tb=256):
    S, D = h.shape
    return pl.pallas_call(
        _dispatch_body,
        grid_spec=pltpu.PrefetchScalarGridSpec(
            num_scalar_prefetch=4,
            grid=(S // tb,),
            in_specs=[pl.BlockSpec((tb, D), lambda i, *_: (i, 0)),
                      pl.BlockSpec((1, D), lambda i, *_: (0, 0))],
            out_specs=pl.BlockSpec(memory_space=pl.ANY),
            scratch_shapes=[pltpu.VMEM((2, tb, D), F32), pltpu.VMEM((ROW_BLOCK, D), F32),
                            pltpu.SemaphoreType.DMA((2,)), pltpu.SemaphoreType.DMA(())]),
        out_shape=jax.ShapeDtypeStruct((n_rows, D), F32),
        compiler_params=_params("arbitrary"),
        name="moe_dispatch",
    )(pos, fill_lo, fill_hi, n_used, h, ln_w.reshape(1, D))


def _experts_body(nblk_ref, be_ref, x_ref, wgu_ref, wd_ref, o_ref, wgu_b, wd_b):
    i = pl.program_id(0)
    n_used = nblk_ref[0]
    fresh = jnp.logical_or(i == 0, be_ref[i] != be_ref[jnp.maximum(i - 1, 0)])

    @pl.when(jnp.logical_and(i < n_used, fresh))
    def _():
        wgu_b[...] = wgu_ref[...].astype(BF16)
        wd_b[...] = wd_ref[...].astype(BF16)

    @pl.when(i < n_used)
    def _():
        gu = _dot(x_ref[...].astype(BF16), wgu_b[...])
        gg = gu[:, :D_EXPERT]
        uu = gu[:, D_EXPERT:]
        act = (gg * (1.0 / (1.0 + jnp.exp(-gg))) * uu).astype(BF16)
        o_ref[...] = _dot(act, wd_b[...])

    @pl.when(i >= n_used)
    def _():
        o_ref[...] = jnp.zeros_like(o_ref)


def _experts(n_used, block_e, xs, w_gu, w_down, layer):
    n_rows, D = xs.shape
    n_blocks = n_rows // ROW_BLOCK
    F2 = w_gu.shape[3]
    return pl.pallas_call(
        _experts_body,
        grid_spec=pltpu.PrefetchScalarGridSpec(
            num_scalar_prefetch=2,
            grid=(n_blocks,),
            in_specs=[pl.BlockSpec((ROW_BLOCK, D), lambda i, nb, be: (jnp.minimum(i, nb[0] - 1), 0)),
                      pl.BlockSpec((None, None, D, F2), lambda i, nb, be: (layer, be[i], 0, 0)),
                      pl.BlockSpec((None, None, F2 // 2, D), lambda i, nb, be: (layer, be[i], 0, 0))],
            out_specs=pl.BlockSpec((ROW_BLOCK, D), lambda i, nb, be: (i, 0)),
            scratch_shapes=[pltpu.VMEM((D, F2), BF16), pltpu.VMEM((F2 // 2, D), BF16)]),
        out_shape=jax.ShapeDtypeStruct((n_rows, D), F32),
        compiler_params=_params("arbitrary"),
        name="moe_experts",
    )(n_used, block_e, xs, w_gu, w_down)


def _start_pair_gather(pos_ref, first_tok, yb_hbm, ybuf, slot, sem, tb):
    D = yb_hbm.shape[1]

    def issue(r, carry):
        base = (first_tok + r) * TOP_K
        for k in range(TOP_K):
            pltpu.make_async_copy(yb_hbm.at[pl.ds(pos_ref[base + k], 1), :],
                                  ybuf.at[slot, pl.ds(r, 1), pl.ds(k * D, D)], sem.at[slot]).start()
        return carry
    lax.fori_loop(0, tb, issue, 0)


def _wait_pair_gather(yb_hbm, ybuf, slot, sem, tb):
    D = yb_hbm.shape[1]
    for k in range(TOP_K):
        pltpu.make_async_copy(yb_hbm.at[pl.ds(0, tb), :], ybuf.at[slot, :, pl.ds(k * D, D)], sem.at[slot]).wait()


def _combine_body(final_norm, pos_ref, yb_hbm, h_ref, r_ref, lnw_ref, o_ref, ybuf, sem):
    i = pl.program_id(0)
    n = pl.num_programs(0)
    tb, D = h_ref.shape
    slot = i % 2

    @pl.when(i == 0)
    def _():
        _start_pair_gather(pos_ref, 0, yb_hbm, ybuf, 0, sem, tb)

    @pl.when(i + 1 < n)
    def _():
        _start_pair_gather(pos_ref, (i + 1) * tb, yb_hbm, ybuf, 1 - slot, sem, tb)

    _wait_pair_gather(yb_hbm, ybuf, slot, sem, tb)
    out = h_ref[...]
    for k in range(TOP_K):
        out = out + ybuf[slot, :, k * D:(k + 1) * D] * r_ref[:, k:k + 1]
    if final_norm:
        out = _rms(out, lnw_ref[...])
    o_ref[...] = out


def _combine(pos, yb, h, routed, ln_out_w, final_norm, tb=128):
    S, D = h.shape
    return pl.pallas_call(
        functools.partial(_combine_body, final_norm),
        grid_spec=pltpu.PrefetchScalarGridSpec(
            num_scalar_prefetch=1,
            grid=(S // tb,),
            in_specs=[pl.BlockSpec(memory_space=pl.ANY),
                      pl.BlockSpec((tb, D), lambda i, pos: (i, 0)),
                      pl.BlockSpec((tb, LANES), lambda i, pos: (i, 0)),
                      pl.BlockSpec((1, D), lambda i, pos: (0, 0))],
            out_specs=pl.BlockSpec((tb, D), lambda i, pos: (i, 0)),
            scratch_shapes=[pltpu.VMEM((2, tb, TOP_K * D), F32), pltpu.SemaphoreType.DMA((2,))]),
        out_shape=jax.ShapeDtypeStruct((S, D), F32),
        compiler_params=_params("arbitrary"),
        name="moe_combine",
    )(pos, yb, h, routed, ln_out_w.reshape(1, D))


def _moe_layer(h, ln_w, wg, bg, we, be, w_gu, w_down, layer, ln_out_w, final_norm):
    T, D = h.shape
    routed, counts = _router(h, ln_w, wg, bg, we, be)
    A = T * TOP_K
    n_blocks = -(-A // ROW_BLOCK) + N_EXPERTS
    n_rows = n_blocks * ROW_BLOCK
    counts = counts[0, N_GROUPS:N_GROUPS + N_EXPERTS].astype(jnp.int32)
    padded = (counts + ROW_BLOCK - 1) // ROW_BLOCK * ROW_BLOCK
    pad_end = jnp.cumsum(padded)
    pad_start = pad_end - padded
    expert_idx = routed[:, 2:2 + TOP_K].astype(jnp.int32)
    slot = routed[:, 4:4 + TOP_K].astype(jnp.int32)
    pos = (pad_start[expert_idx] + slot).reshape(A)
    block_start = jnp.arange(n_blocks, dtype=jnp.int32) * ROW_BLOCK
    block_e = jnp.minimum(jnp.searchsorted(pad_end, block_start, side='right'), N_EXPERTS - 1).astype(jnp.int32)
    n_used = (pad_end[-1] // ROW_BLOCK).astype(jnp.int32).reshape(1)
    xs = _dispatch(pos, pad_start + counts, pad_end, n_used, h, ln_w, n_rows)
    yb = _experts(n_used, block_e, xs, w_gu, w_down, layer)
    return _combine(pos, yb, h, routed, ln_out_w, final_norm)


def _rwkv_prep_body(h_ref, halo_ref, lnw_ref, mu_ref, xs_ref):
    i = pl.program_id(0)
    lnw = lnw_ref[...]
    hn = _rms(h_ref[...], lnw)
    ts = hn.shape[0]
    last = _rms(halo_ref[...], lnw)[SUBLANES - 1:SUBLANES, :] * (i > 0).astype(F32)
    row = lax.broadcasted_iota(jnp.int32, (ts, 1), 0)
    prev = jnp.where(row == 0, last, pltpu.roll(hn, 1, axis=0))
    xx = prev - hn
    for n in range(xs_ref.shape[0]):
        xs_ref[n] = (hn + xx * mu_ref[n:n + 1, :]).astype(xs_ref.dtype)


def _rwkv_prep(h, ln_w, mu, ts=256):
    S, D = h.shape
    n_mix = mu.shape[0]
    halo_blocks = ts // SUBLANES
    return pl.pallas_call(
        _rwkv_prep_body,
        grid=(S // ts,),
        in_specs=[pl.BlockSpec((ts, D), lambda i: (i, 0)),
                  pl.BlockSpec((SUBLANES, D), lambda i: (jnp.maximum(i * halo_blocks - 1, 0), 0)),
                  pl.BlockSpec((1, D), lambda i: (0, 0)),
                  pl.BlockSpec((n_mix, D), lambda i: (0, 0))],
        out_specs=pl.BlockSpec((n_mix, ts, D), lambda i: (0, i, 0)),
        out_shape=jax.ShapeDtypeStruct((n_mix, S, D), BF16),
        compiler_params=_params("arbitrary"),
        name="rwkv_prep",
    )(h, h, ln_w.reshape(1, D), mu)


def _sigmoid(x):
    return 1.0 / (1.0 + jnp.exp(-x))


def _rwkv_lora_body(xw_ref, xa_ref, xg_ref, w0_ref, w1_ref, w2_ref, a0_ref, a1_ref, a2_ref, g1_ref, g2_ref,
                    lw_ref, a_ref, g_ref):
    z = w0_ref[...] + _dot(jnp.tanh(_dot(xw_ref[...], w1_ref[...])).astype(BF16), w2_ref[...])
    u = -z
    softplus = jnp.maximum(u, 0.0) + jnp.log1p(jnp.exp(-jnp.abs(u)))
    lw_ref[...] = -jnp.exp(-softplus - 0.5)
    a_ref[...] = _sigmoid(a0_ref[...] + _dot(_dot(xa_ref[...], a1_ref[...]).astype(BF16), a2_ref[...]))
    g_ref[...] = _dot(_sigmoid(_dot(xg_ref[...], g1_ref[...])).astype(BF16), g2_ref[...])


def _rwkv_lora(xs, w0, w1, w2, a0, a1, a2, g1, g2, ts=256):
    _, S, D = xs.shape

    def pad_lora(wa, wb):
        r = wa.shape[1]
        rp = -(-r // LANES) * LANES
        return (jnp.pad(wa, ((0, 0), (0, rp - r))).astype(BF16), jnp.pad(wb, ((0, rp - r), (0, 0))).astype(BF16))

    w1p, w2p = pad_lora(w1, w2)
    a1p, a2p = pad_lora(a1, a2)
    g1p, g2p = pad_lora(g1, g2)
    full = lambda arr: pl.BlockSpec(arr.shape, lambda i: (0,) * arr.ndim)
    vec = pl.BlockSpec((1, D), lambda i: (0, 0))
    row = pl.BlockSpec((ts, D), lambda i: (i, 0))
    return pl.pallas_call(
        _rwkv_lora_body,
        grid=(S // ts,),
        in_specs=[pl.BlockSpec((None, ts, D), lambda i: (3, i, 0)),
                  pl.BlockSpec((None, ts, D), lambda i: (4, i, 0)),
                  pl.BlockSpec((None, ts, D), lambda i: (5, i, 0)),
                  vec, full(w1p), full(w2p), vec, full(a1p), full(a2p), full(g1p), full(g2p)],
        out_specs=[row, row, row],
        out_shape=[jax.ShapeDtypeStruct((S, D), F32)] * 3,
        compiler_params=_params("arbitrary"),
        name="rwkv_lora",
    )(xs, xs, xs, w0.reshape(1, D), w1p, w2p, a0.reshape(1, D), a1p, a2p, g1p, g2p)


def _scan_chunk(r, k, v, lw, a, g, kkw, kaw, rkw, lnw, lnb, state, c):
    m0, m1, seg_f, seg_b, tri, strict, incl, eye2 = c
    stack = lambda x: jnp.concatenate([x * m0, x * m1], axis=0)
    rows = lambda *xs: jnp.concatenate(xs, axis=0)
    cols = lambda *xs: jnp.concatenate(xs, axis=1)

    kk = k * kkw
    k2 = k * (1.0 + (a - 1.0) * kaw)
    sums = _segsum(rows(kk * kk, r * k2 * rkw), seg_b)
    cw = _cumsum_rows(tri, lw)
    yield
    kk = kk / jnp.maximum(jnp.sqrt(sums[:CHUNK]), 1e-12)
    bonus = sums[CHUNK:] * v
    bv = kk * a
    cl = cw[CHUNK - 1:CHUNK, :]
    at = -kk * jnp.exp(cw - lw)
    rt = r * jnp.exp(cw)
    dinv = jnp.exp(-cw)
    drem = jnp.exp(cl - cw)
    vs = stack(v)
    gram = _mm_p_nt(rows(at, rt), rows(stack(bv * dinv), stack(k2 * dinv)))
    bk_t = rows(bv * drem, k2 * drem).T
    decay_rows = jnp.broadcast_to(jnp.exp(cl), (PAIR, PAIR)).T
    yield
    a_ab = jnp.where(strict, gram[:CHUNK, :PAIR], 0.0)
    a_ak = jnp.where(strict, gram[:CHUNK, PAIR:], 0.0)
    a_rb = jnp.where(incl, gram[CHUNK:, :PAIR], 0.0)
    a_rk = jnp.where(incl, gram[CHUNK:, PAIR:], 0.0)
    inv = eye2 + a_ab
    pw = _mm_p(a_ab, stack(a_ab))
    xy = _mm_p(rows(cols(at, a_ak), cols(rt, a_rk)), rows(state, vs))
    yield
    span = 2
    while span < CHUNK // 2:
        both = _mm_p(rows(inv, pw), stack(pw))
        yield
        inv = inv + both[:CHUNK]
        pw = both[CHUNK:]
        span *= 2
    inv = inv + _mm_p(inv, stack(pw))
    yield
    u = _mm_p(inv, stack(xy[:CHUNK]))
    yield
    y = xy[CHUNK:] + _mm_p(a_rb, stack(u))
    new_state = state * decay_rows + seg_f * _mm_p(bk_t, rows(u, v))
    yield
    inv_n = 1.0 / RWKV_HEAD_DIM
    mean = _segsum(y, seg_b) * inv_n
    yield
    yc = y - mean
    var = _segsum(yc * yc, seg_b) * inv_n
    yield
    yn = yc * lax.rsqrt(var + GN_EPS) * lnw + lnb
    return ((yn + bonus) * g), new_state


def _run_interleaved(gens):
    results = [None] * len(gens)
    live = list(range(len(gens)))
    while live:
        for i in list(live):
            try:
                next(gens[i])
            except StopIteration as done:
                results[i] = done.value
                live.remove(i)
    return results


def _pack_lhs(a):
    hi, lo = _split(a)
    return jnp.concatenate([hi, lo], axis=1)


def _fold(x):
    n = x.shape[1] // 2
    return x[:, :n] + x[:, n:]


def _mm_p(a, b):
    hi, lo = _split(b)
    row = jnp.concatenate([hi, lo], axis=1)
    return _fold(_dot(_pack_lhs(a), jnp.concatenate([row, row], axis=0)))


def _mm_p_nt(a, b):
    hi, lo = _split(b)
    packed = jnp.concatenate([jnp.concatenate([hi, hi], axis=1), jnp.concatenate([lo, lo], axis=1)], axis=0)
    return _fold(_dot(_pack_lhs(a), packed, NT))


def _segsum(x, seg_b):
    hi = x.astype(BF16)
    mid = (x - hi.astype(F32)).astype(BF16)
    return _dot(jnp.concatenate([hi, mid], axis=1), jnp.concatenate([seg_b, seg_b], axis=0))


def _cumsum_rows(tri, x):
    hi = x.astype(BF16)
    r1 = x - hi.astype(F32)
    mid = r1.astype(BF16)
    lo = (r1 - mid.astype(F32)).astype(BF16)
    return _dot(jnp.concatenate([tri, tri, tri], axis=1), jnp.concatenate([hi, mid, lo], axis=0))


def _scan_consts():
    lane = lax.broadcasted_iota(jnp.int32, (1, PAIR), 1)
    m0 = (lane < RWKV_HEAD_DIM).astype(F32)
    m1 = 1.0 - m0
    ri = lax.broadcasted_iota(jnp.int32, (PAIR, PAIR), 0)
    ci = lax.broadcasted_iota(jnp.int32, (PAIR, PAIR), 1)
    head_r = (ri >= RWKV_HEAD_DIM).astype(F32)
    head_c = (ci >= RWKV_HEAD_DIM).astype(F32)
    seg_f = head_r * head_c + (1.0 - head_r) * (1.0 - head_c)
    tri = (lax.broadcasted_iota(jnp.int32, (CHUNK, CHUNK), 1)
           <= lax.broadcasted_iota(jnp.int32, (CHUNK, CHUNK), 0)).astype(F32).astype(BF16)
    t = lax.broadcasted_iota(jnp.int32, (CHUNK, PAIR), 0)
    s = lax.broadcasted_iota(jnp.int32, (CHUNK, PAIR), 1) & (RWKV_HEAD_DIM - 1)
    return (m0, m1, seg_f, seg_f.astype(BF16), tri, s < t, s <= t, (s == t).astype(F32))


def _scan_body(r_ref, k_ref, v_ref, lw_ref, a_ref, g_ref, kkw_ref, kaw_ref, rkw_ref, lnw_ref, lnb_ref,
               o_ref, state_ref):
    @pl.when(pl.program_id(1) == 0)
    def _():
        state_ref[...] = jnp.zeros_like(state_ref)

    n_pairs = o_ref.shape[1] // PAIR

    def chunk_step(j, carry):
        consts = _scan_consts()
        rows = pl.ds(pl.multiple_of(j * CHUNK, CHUNK), CHUNK)
        lanes = [slice(p * PAIR, (p + 1) * PAIR) for p in range(n_pairs)]
        args = [(r_ref[rows, ln], k_ref[rows, ln], v_ref[rows, ln], lw_ref[rows, ln], a_ref[rows, ln],
                 g_ref[rows, ln], kkw_ref[:, ln], kaw_ref[:, ln], rkw_ref[:, ln], lnw_ref[:, ln],
                 lnb_ref[:, ln], state_ref[p]) for p, ln in enumerate(lanes)]
        results = _run_interleaved([_scan_chunk(*a, consts) for a in args])
        for p, (ln, (out, new_state)) in enumerate(zip(lanes, results)):
            state_ref[p] = new_state
            o_ref[rows, ln] = out.astype(o_ref.dtype)
        return carry

    lax.fori_loop(0, o_ref.shape[0] // CHUNK, chunk_step, 0)


def _rwkv_scan(rkv, lw, a, g, k_k, k_a, r_k, ln_w, ln_b):
    _, S, D = rkv.shape
    tr, tl = min(SCAN_ROWS, S), SCAN_LANES
    blk = lambda n: pl.BlockSpec((None, tr, tl), lambda p, i: (n, i, p))
    row = pl.BlockSpec((tr, tl), lambda p, i: (i, p))
    vec = pl.BlockSpec((1, tl), lambda p, i: (0, p))
    as_row = lambda w: w.reshape(1, D)
    return pl.pallas_call(
        _scan_body,
        grid=(D // tl, S // tr),
        in_specs=[blk(0), blk(1), blk(2), row, row, row, vec, vec, vec, vec, vec],
        out_specs=row,
        out_shape=jax.ShapeDtypeStruct((S, D), BF16),
        scratch_shapes=[pltpu.VMEM((tl // PAIR, PAIR, PAIR), F32)],
        compiler_params=_params("arbitrary", "arbitrary"),
        name="rwkv_scan",
    )(rkv, rkv, rkv, lw, a, g, as_row(k_k), as_row(k_a), as_row(r_k), as_row(ln_w), as_row(ln_b))


def _rwkv_layer(h, ln_w, mu, w_rkv, w_o, j, w0, w1, w2, a0, a1, a2, g1, g2, k_k, k_a, r_k, ln_gn_w, ln_gn_b):
    xs = _rwkv_prep(h, ln_w, mu)
    rkv = _rkv_proj(xs, w_rkv, j)
    lw, a, g = _rwkv_lora(xs, w0, w1, w2, a0, a1, a2, g1, g2)
    z = _rwkv_scan(rkv, lw, a, g, k_k, k_a, r_k, ln_gn_w, ln_gn_b)
    return _out_proj(z, w_o, j, h)


def kernel(x, mem, ln_mix_w, ln_xattn_w, ln_mem_w, xattn_wq, xattn_wkv, xattn_wo, ln_ffn_w, router_group_w, router_group_b, router_expert_w, router_expert_b, moe_w_gate_up, moe_w_down, pool_w, pool_scale, rwkv_mu, rwkv_w_rkv, rwkv_w_o, rwkv_w0, rwkv_w1, rwkv_w2, rwkv_a0, rwkv_a1, rwkv_a2, rwkv_g1, rwkv_g2, rwkv_k_k, rwkv_k_a, rwkv_r_k, rwkv_ln_w, rwkv_ln_b, ln_out_w):
    B, S, D = x.shape
    depth = ln_mix_w.shape[0]
    outs = []
    for b in range(B):
        h = x[b]
        for i in range(depth):
            j = i // 2
            if i % 2 == 0:
                h = _pool_layer(h, ln_mix_w[i], pool_w[j], pool_scale[j])
            else:
                h = _rwkv_layer(h, ln_mix_w[i], rwkv_mu[j], rwkv_w_rkv, rwkv_w_o, j, rwkv_w0[j], rwkv_w1[j],
                                rwkv_w2[j], rwkv_a0[j], rwkv_a1[j], rwkv_a2[j], rwkv_g1[j], rwkv_g2[j],
                                rwkv_k_k[j], rwkv_k_a[j], rwkv_r_k[j].reshape(D), rwkv_ln_w[j], rwkv_ln_b[j])
            kv = _mem_kv(mem[b], ln_mem_w[i], xattn_wkv, i)
            h = _xattn_layer(h, ln_xattn_w[i], xattn_wq, kv, xattn_wo, i)
            h = _moe_layer(h, ln_ffn_w[i], router_group_w[i], router_group_b[i], router_expert_w[i],
                           router_expert_b[i], moe_w_gate_up, moe_w_down, i, ln_out_w,
                           final_norm=(i == depth - 1))
        outs.append(h)
    return outs[0][None] if B == 1 else jnp.stack(outs, axis=0)
```

```python
import functools

import jax
import jax.numpy as jnp
from jax import lax
from jax.experimental import pallas as pl
from jax.experimental.pallas import tpu as pltpu

F32 = jnp.float32
BF16 = jnp.bfloat16

D_MODEL = 2048
POOL_WINDOWS = (2, 4, 8, 16)
POOL_GROUP_DIM = D_MODEL // len(POOL_WINDOWS)
MAX_WINDOW = max(POOL_WINDOWS)
RWKV_HEAD_DIM = 64
GN_EPS = 64e-5
XATTN_HEADS = 4
XATTN_HEAD_DIM = D_MODEL // XATTN_HEADS
N_GROUPS = 8
EXPERTS_PER_GROUP = 8
N_EXPERTS = N_GROUPS * EXPERTS_PER_GROUP
TOP_K = 2
D_EXPERT = D_MODEL // 8
ROW_BLOCK = 128
RMS_EPS = 1e-6

LANES = 128
SUBLANES = 8
VMEM_LIMIT = 56 * 1024 * 1024

CHUNK = 64
GROUP_HEADS = 4
GROUP = GROUP_HEADS * RWKV_HEAD_DIM
ISSUE_UNROLL = 8
WEIGHT_STAGE_ROWS = 256
SCAN_ROWS = 256
SCAN_LANES = 2048

NN = (((1,), (0,)), ((), ()))
NT = (((1,), (1,)), ((), ()))


def _params(*sem):
    return pltpu.CompilerParams(dimension_semantics=sem, vmem_limit_bytes=VMEM_LIMIT)


def _rms(x, w):
    return x * lax.rsqrt(jnp.mean(x * x, axis=-1, keepdims=True) + RMS_EPS) * w


def _dot(a, b, dims=NN):
    return lax.dot_general(a, b, dims, preferred_element_type=F32)


def _split(x):
    hi = x.astype(BF16)
    return hi, (x - hi.astype(F32)).astype(BF16)


def _rkv_proj_body(x_ref, w_ref, o_ref, wb_ref):
    @pl.when(pl.program_id(2) == 0)
    def _():
        wb_ref[...] = w_ref[...].astype(BF16)

    o_ref[...] = _dot(x_ref[...], wb_ref[...])


def _rkv_proj(xs, w_rkv, j, tm=512, tn=1024):
    _, M, K = xs.shape
    B, N = w_rkv.shape[1], w_rkv.shape[3]
    return pl.pallas_call(
        _rkv_proj_body,
        grid=(B, N // tn, M // tm),
        in_specs=[pl.BlockSpec((None, tm, K), lambda b, n, i: (b, i, 0)),
                  pl.BlockSpec((None, None, K, tn), lambda b, n, i: (j, b, 0, n))],
        out_specs=pl.BlockSpec((None, tm, tn), lambda b, n, i: (b, i, n)),
        out_shape=jax.ShapeDtypeStruct((B, M, N), F32),
        scratch_shapes=[pltpu.VMEM((K, tn), BF16)],
        compiler_params=_params("arbitrary", "arbitrary", "arbitrary"),
        name="rwkv_rkv_proj",
    )(xs, w_rkv)


def _out_proj_body(x_ref, w_ref, r_ref, o_ref, wb_ref):
    @pl.when(pl.program_id(1) == 0)
    def _():
        wb_ref[...] = w_ref[...].astype(BF16)

    o_ref[...] = r_ref[...] + _dot(x_ref[...], wb_ref[...])


def _out_proj(x, w_o, j, res, tm=512, tn=1024):
    M, K = x.shape
    N = w_o.shape[2]
    return pl.pallas_call(
        _out_proj_body,
        grid=(N // tn, M // tm),
        in_specs=[pl.BlockSpec((tm, K), lambda n, i: (i, 0)),
                  pl.BlockSpec((None, K, tn), lambda n, i: (j, 0, n)),
                  pl.BlockSpec((tm, tn), lambda n, i: (i, n))],
        out_specs=pl.BlockSpec((tm, tn), lambda n, i: (i, n)),
        out_shape=jax.ShapeDtypeStruct((M, N), F32),
        scratch_shapes=[pltpu.VMEM((K, tn), BF16)],
        compiler_params=_params("arbitrary", "arbitrary"),
        name="rwkv_out_proj",
    )(x, w_o, res)


def _pool_body(h_ref, halo_ref, lnw_ref, pw_ref, ps_ref, o_ref):
    i = pl.program_id(0)
    ts = h_ref.shape[0]
    h = h_ref[...]
    lnw = lnw_ref[...]
    x = _rms(h, lnw)
    xh = _rms(halo_ref[...], lnw) * (i > 0).astype(F32)
    xe = jnp.concatenate([xh, x], axis=0)
    t = i * ts + lax.broadcasted_iota(jnp.int32, (ts, 1), 0)
    outs = []
    for g, win in enumerate(POOL_WINDOWS):
        sl = slice(g * POOL_GROUP_DIM, (g + 1) * POOL_GROUP_DIM)
        acc = xe[:, sl]
        span = 1
        while span < win:
            acc = acc + pltpu.roll(acc, span, axis=0)
            span *= 2
        cnt = jnp.minimum(t + 1, win).astype(F32)
        pooled = acc[MAX_WINDOW:, :] / cnt - x[:, sl]
        outs.append(_dot(pooled.astype(BF16), pw_ref[g]))
    y = jnp.concatenate(outs, axis=-1)
    o_ref[...] = h + y * ps_ref[...]


def _pool_layer(h, ln_w, pool_w, pool_scale, ts=512):
    S, D = h.shape
    G = pool_w.shape[0]
    halo_blocks = ts // MAX_WINDOW
    return pl.pallas_call(
        _pool_body,
        grid=(S // ts,),
        in_specs=[pl.BlockSpec((ts, D), lambda i: (i, 0)),
                  pl.BlockSpec((MAX_WINDOW, D), lambda i: (jnp.maximum(i * halo_blocks - 1, 0), 0)),
                  pl.BlockSpec((1, D), lambda i: (0, 0)),
                  pl.BlockSpec((G, POOL_GROUP_DIM, POOL_GROUP_DIM), lambda i: (0, 0, 0)),
                  pl.BlockSpec((1, D), lambda i: (0, 0))],
        out_specs=pl.BlockSpec((ts, D), lambda i: (i, 0)),
        out_shape=jax.ShapeDtypeStruct((S, D), F32),
        compiler_params=_params("arbitrary"),
        name="pool_layer",
    )(h, h, ln_w.reshape(1, D), pool_w.astype(BF16), pool_scale.reshape(1, D))


def _memkv_body(mem_ref, lnw_ref, w_ref, o_ref):
    mn = _rms(mem_ref[...], lnw_ref[...])
    o_ref[...] = _dot(mn.astype(BF16), w_ref[...].astype(BF16)).astype(o_ref.dtype)


def _mem_kv(mem, ln_w, wkv, layer, tn=1024):
    M, D = mem.shape
    N = wkv.shape[2]
    return pl.pallas_call(
        _memkv_body,
        grid=(N // tn,),
        in_specs=[pl.BlockSpec((M, D), lambda n: (0, 0)),
                  pl.BlockSpec((1, D), lambda n: (0, 0)),
                  pl.BlockSpec((None, D, tn), lambda n: (layer, 0, n))],
        out_specs=pl.BlockSpec((M, tn), lambda n: (0, n)),
        out_shape=jax.ShapeDtypeStruct((M, N), BF16),
        compiler_params=_params("arbitrary"),
        name="xattn_mem_kv",
    )(mem, ln_w.reshape(1, D), wkv)


def _load_weight_bf16(w_hbm, dst, stage, sem):
    rows = stage.shape[1]
    n_chunks = w_hbm.shape[0] // rows

    def chunk_copy(c):
        return pltpu.make_async_copy(w_hbm.at[pl.ds(c * rows, rows), :], stage.at[c % 2], sem.at[c % 2])

    chunk_copy(0).start()
    for c in range(n_chunks):
        if c + 1 < n_chunks:
            chunk_copy(c + 1).start()
        chunk_copy(c).wait()
        dst[pl.ds(c * rows, rows), :] = stage[c % 2].astype(BF16)


def _xattn_body(layer, h_ref, lnw_ref, k_ref, v_ref, wq_hbm, wo_hbm, o_ref, wq_ref, wo_ref, stage, sem):
    @pl.when(pl.program_id(0) == 0)
    def _():
        _load_weight_bf16(wq_hbm.at[layer], wq_ref, stage, sem)
        _load_weight_bf16(wo_hbm.at[layer], wo_ref, stage, sem)

    h = h_ref[...]
    hn = _rms(h, lnw_ref[...]).astype(BF16)
    q = _dot(hn, wq_ref[...])
    scale = XATTN_HEAD_DIM ** -0.5
    heads = []
    for hd in range(XATTN_HEADS):
        sl = slice(hd * XATTN_HEAD_DIM, (hd + 1) * XATTN_HEAD_DIM)
        s = _dot(q[:, sl].astype(BF16), k_ref[:, sl], NT) * scale
        s = s - jnp.max(s, axis=-1, keepdims=True)
        e = jnp.exp(s)
        p = e / jnp.sum(e, axis=-1, keepdims=True)
        heads.append(_dot(p.astype(BF16), v_ref[:, sl]))
    o = jnp.concatenate(heads, axis=-1).astype(BF16)
    o_ref[...] = h + _dot(o, wo_ref[...])


def _xattn_layer(h, ln_w, wq, kv, wo, layer, ts=256):
    S, D = h.shape
    M = kv.shape[0]
    return pl.pallas_call(
        functools.partial(_xattn_body, layer),
        grid=(S // ts,),
        in_specs=[pl.BlockSpec((ts, D), lambda i: (i, 0)),
                  pl.BlockSpec((1, D), lambda i: (0, 0)),
                  pl.BlockSpec((M, D), lambda i: (0, 0)),
                  pl.BlockSpec((M, D), lambda i: (0, 1)),
                  pl.BlockSpec(memory_space=pl.ANY),
                  pl.BlockSpec(memory_space=pl.ANY)],
        out_specs=pl.BlockSpec((ts, D), lambda i: (i, 0)),
        out_shape=jax.ShapeDtypeStruct((S, D), F32),
        scratch_shapes=[pltpu.VMEM((D, D), BF16), pltpu.VMEM((D, D), BF16),
                        pltpu.VMEM((2, WEIGHT_STAGE_ROWS, D), F32), pltpu.SemaphoreType.DMA((2,))],
        compiler_params=_params("arbitrary"),
        name="xattn_layer",
    )(h, ln_w.reshape(1, D), kv, kv, wq, wo)


def _router_body(h_ref, lnw_ref, whi_ref, wlo_ref, b_ref, r_ref, cnt_ref):
    i = pl.program_id(0)

    @pl.when(i == 0)
    def _():
        cnt_ref[...] = jnp.zeros_like(cnt_ref)

    hn = _rms(h_ref[...], lnw_ref[...])
    xh, xl = _split(hn)
    whi = whi_ref[...]
    lg = _dot(xh, whi) + (_dot(xh, wlo_ref[...]) + _dot(xl, whi)) + b_ref[...]
    ts = lg.shape[0]
    lane = lax.broadcasted_iota(jnp.int32, (ts, LANES), 1)
    lanef = lane.astype(F32)
    neg = jnp.float32(-jnp.inf)
    gl = jnp.where(lane < N_GROUPS, lg, neg)
    gmax = jnp.max(gl, axis=-1, keepdims=True)
    pg_top = 1.0 / jnp.sum(jnp.exp(gl - gmax), axis=-1, keepdims=True)
    g_idx = jnp.min(jnp.where(gl == gmax, lanef, float(LANES)), axis=-1, keepdims=True)
    lo_lane = g_idx * EXPERTS_PER_GROUP + N_GROUPS
    el = jnp.where(lanef >= lo_lane, jnp.where(lanef < lo_lane + EXPERTS_PER_GROUP, lg, neg), neg)
    m1 = jnp.max(el, axis=-1, keepdims=True)
    i1 = jnp.min(jnp.where(el == m1, lanef, float(LANES)), axis=-1, keepdims=True)
    el2 = jnp.where(lanef == i1, neg, el)
    m2 = jnp.max(el2, axis=-1, keepdims=True)
    i2 = jnp.min(jnp.where(el2 == m2, lanef, float(LANES)), axis=-1, keepdims=True)
    e2 = jnp.exp(m2 - m1)
    g1 = pg_top / (1.0 + e2)
    g2 = pg_top * e2 / (1.0 + e2)
    hit1 = (lanef == i1).astype(F32)
    hit2 = (lanef == i2).astype(F32)
    hits = hit1 + hit2
    earlier = (lax.broadcasted_iota(jnp.int32, (ts, ts), 1)
               < lax.broadcasted_iota(jnp.int32, (ts, ts), 0)).astype(F32).astype(BF16)
    before = _dot(earlier, hits.astype(BF16)) + cnt_ref[...]
    r1 = jnp.sum(before * hit1, axis=-1, keepdims=True)
    r2 = jnp.sum(before * hit2, axis=-1, keepdims=True)
    cnt_ref[...] += jnp.sum(hits, axis=0, keepdims=True)
    out = jnp.where(lane == 0, g1, 0.0)
    out = jnp.where(lane == 1, g2, out)
    out = jnp.where(lane == 2, i1 - N_GROUPS, out)
    out = jnp.where(lane == 3, i2 - N_GROUPS, out)
    out = jnp.where(lane == 4, r1, out)
    out = jnp.where(lane == 5, r2, out)
    r_ref[...] = out


def _router(h, ln_w, wg, bg, we, be, ts=512):
    S, D = h.shape
    unused = LANES - N_GROUPS - N_EXPERTS
    w = jnp.concatenate([wg, we, jnp.zeros((D, unused), F32)], axis=1)
    b = jnp.concatenate([bg, be, jnp.zeros((unused,), F32)]).reshape(1, LANES)
    whi, wlo = _split(w)
    return pl.pallas_call(
        _router_body,
        grid=(S // ts,),
        in_specs=[pl.BlockSpec((ts, D), lambda i: (i, 0)),
                  pl.BlockSpec((1, D), lambda i: (0, 0)),
                  pl.BlockSpec((D, LANES), lambda i: (0, 0)),
                  pl.BlockSpec((D, LANES), lambda i: (0, 0)),
                  pl.BlockSpec((1, LANES), lambda i: (0, 0))],
        out_specs=[pl.BlockSpec((ts, LANES), lambda i: (i, 0)),
                   pl.BlockSpec((1, LANES), lambda i: (0, 0))],
        out_shape=[jax.ShapeDtypeStruct((S, LANES), F32), jax.ShapeDtypeStruct((1, LANES), F32)],
        compiler_params=_params("arbitrary"),
        name="moe_router",
    )(h, ln_w.reshape(1, D), whi, wlo, b)


def _dispatch_body(pos_ref, fill_lo_ref, fill_hi_ref, nblk_ref, h_ref, lnw_ref, xs_hbm, xn, zeros, sem, zsem):
    i = pl.program_id(0)
    n = pl.num_programs(0)
    tb = h_ref.shape[0]
    slot = i % 2

    def row_copy(r, dst_row):
        return pltpu.make_async_copy(xn.at[slot, pl.ds(r, 1), :], xs_hbm.at[pl.ds(dst_row, 1), :], sem.at[slot])

    def wait_slot(s):
        for _ in range(TOP_K):
            pltpu.make_async_copy(xn.at[s], xs_hbm.at[pl.ds(0, tb), :], sem.at[s]).wait()

    @pl.when(i >= 2)
    def _():
        wait_slot(slot)

    xn[slot] = _rms(h_ref[...], lnw_ref[...])

    def issue(r, carry):
        base = (i * tb + r) * TOP_K
        for k in range(TOP_K):
            row_copy(r, pos_ref[base + k]).start(priority=k % 2)
        return carry
    lax.fori_loop(0, tb, issue, 0, unroll=ISSUE_UNROLL)

    @pl.when(i == n - 1)
    def _():
        zeros[...] = jnp.zeros_like(zeros)

        def zero_row(row):
            return pltpu.make_async_copy(zeros.at[pl.ds(0, 1), :], xs_hbm.at[pl.ds(row, 1), :], zsem)

        def zero_block(b):
            return pltpu.make_async_copy(zeros, xs_hbm.at[pl.ds(b * ROW_BLOCK, ROW_BLOCK), :], zsem)

        def per_expert(e, carry):
            lo, hi = fill_lo_ref[e], fill_hi_ref[e]
            lax.fori_loop(lo, hi, lambda row, c: (zero_row(row).start(), c)[1], 0)
            lax.fori_loop(lo, hi, lambda row, c: (zero_row(row).wait(), c)[1], 0)
            return carry
        lax.fori_loop(0, N_EXPERTS, per_expert, 0)
        n_blocks = xs_hbm.shape[0] // ROW_BLOCK
        lax.fori_loop(nblk_ref[0], n_blocks, lambda b, c: (zero_block(b).start(), c)[1], 0)
        lax.fori_loop(nblk_ref[0], n_blocks, lambda b, c: (zero_block(b).wait(), c)[1], 0)
        wait_slot(slot)

        @pl.when(n >= 2)
        def _():
            wait_slot(1 - slot)


def _dispatch(pos, fill_lo, fill_hi, n_used, h, ln_w, n_rows, tb=256):
    S, D = h.shape
    return pl.pallas_call(
        _dispatch_body,
        grid_spec=pltpu.PrefetchScalarGridSpec(
            num_scalar_prefetch=4,
            grid=(S // tb,),
            in_specs=[pl.BlockSpec((tb, D), lambda i, *_: (i, 0)),
                      pl.BlockSpec((1, D), lambda i, *_: (0, 0))],
            out_specs=pl.BlockSpec(memory_space=pl.ANY),
            scratch_shapes=[pltpu.VMEM((2, tb, D), F32), pltpu.VMEM((ROW_BLOCK, D), F32),
                            pltpu.SemaphoreType.DMA((2,)), pltpu.SemaphoreType.DMA(())]),
        out_shape=jax.ShapeDtypeStruct((n_rows, D), F32),
        compiler_params=_params("arbitrary"),
        name="moe_dispatch",
    )(pos, fill_lo, fill_hi, n_used, h, ln_w.reshape(1, D))


def _experts_body(nblk_ref, be_ref, x_ref, wgu_ref, wd_ref, o_ref, wgu_b, wd_b):
    i = pl.program_id(0)
    n_used = nblk_ref[0]
    fresh = jnp.logical_or(i == 0, be_ref[i] != be_ref[jnp.maximum(i - 1, 0)])

    @pl.when(jnp.logical_and(i < n_used, fresh))
    def _():
        wgu_b[...] = wgu_ref[...].astype(BF16)
        wd_b[...] = wd_ref[...].astype(BF16)

    @pl.when(i < n_used)
    def _():
        gu = _dot(x_ref[...].astype(BF16), wgu_b[...])
        gg = gu[:, :D_EXPERT]
        uu = gu[:, D_EXPERT:]
        act = (gg * (1.0 / (1.0 + jnp.exp(-gg))) * uu).astype(BF16)
        o_ref[...] = _dot(act, wd_b[...])

    @pl.when(i >= n_used)
    def _():
        o_ref[...] = jnp.zeros_like(o_ref)


def _experts(n_used, block_e, xs, w_gu, w_down, layer):
    n_rows, D = xs.shape
    n_blocks = n_rows // ROW_BLOCK
    F2 = w_gu.shape[3]
    return pl.pallas_call(
        _experts_body,
        grid_spec=pltpu.PrefetchScalarGridSpec(
            num_scalar_prefetch=2,
            grid=(n_blocks,),
            in_specs=[pl.BlockSpec((ROW_BLOCK, D), lambda i, nb, be: (jnp.minimum(i, nb[0] - 1), 0)),
                      pl.BlockSpec((None, None, D, F2), lambda i, nb, be: (layer, be[i], 0, 0)),
                      pl.BlockSpec((None, None, F2 // 2, D), lambda i, nb, be: (layer, be[i], 0, 0))],
            out_specs=pl.BlockSpec((ROW_BLOCK, D), lambda i, nb, be: (i, 0)),
            scratch_shapes=[pltpu.VMEM((D, F2), BF16), pltpu.VMEM((F2 // 2, D), BF16)]),
        out_shape=jax.ShapeDtypeStruct((n_rows, D), F32),
        compiler_params=_params("arbitrary"),
        name="moe_experts",
    )(n_used, block_e, xs, w_gu, w_down)


def _start_pair_gather(pos_ref, first_tok, yb_hbm, ybuf, slot, sem, tb):
    D = yb_hbm.shape[1]

    def issue(r, carry):
        base = (first_tok + r) * TOP_K
        for k in range(TOP_K):
            pltpu.make_async_copy(yb_hbm.at[pl.ds(pos_ref[base + k], 1), :],
                                  ybuf.at[slot, pl.ds(r, 1), pl.ds(k * D, D)], sem.at[slot]).start(priority=k % 2)
        return carry
    lax.fori_loop(0, tb, issue, 0, unroll=ISSUE_UNROLL)


def _wait_pair_gather(yb_hbm, ybuf, slot, sem, tb):
    D = yb_hbm.shape[1]
    for k in range(TOP_K):
        pltpu.make_async_copy(yb_hbm.at[pl.ds(0, tb), :], ybuf.at[slot, :, pl.ds(k * D, D)], sem.at[slot]).wait()


def _combine_body(final_norm, pos_ref, yb_hbm, h_ref, r_ref, lnw_ref, o_ref, ybuf, sem):
    i = pl.program_id(0)
    n = pl.num_programs(0)
    tb, D = h_ref.shape
    slot = i % 2

    @pl.when(i == 0)
    def _():
        _start_pair_gather(pos_ref, 0, yb_hbm, ybuf, 0, sem, tb)

    @pl.when(i + 1 < n)
    def _():
        _start_pair_gather(pos_ref, (i + 1) * tb, yb_hbm, ybuf, 1 - slot, sem, tb)

    _wait_pair_gather(yb_hbm, ybuf, slot, sem, tb)
    out = h_ref[...]
    for k in range(TOP_K):
        out = out + ybuf[slot, :, k * D:(k + 1) * D] * r_ref[:, k:k + 1]
    if final_norm:
        out = _rms(out, lnw_ref[...])
    o_ref[...] = out


def _combine(pos, yb, h, routed, ln_out_w, final_norm, tb=128):
    S, D = h.shape
    return pl.pallas_call(
        functools.partial(_combine_body, final_norm),
        grid_spec=pltpu.PrefetchScalarGridSpec(
            num_scalar_prefetch=1,
            grid=(S // tb,),
            in_specs=[pl.BlockSpec(memory_space=pl.ANY),
                      pl.BlockSpec((tb, D), lambda i, pos: (i, 0)),
                      pl.BlockSpec((tb, LANES), lambda i, pos: (i, 0)),
                      pl.BlockSpec((1, D), lambda i, pos: (0, 0))],
            out_specs=pl.BlockSpec((tb, D), lambda i, pos: (i, 0)),
            scratch_shapes=[pltpu.VMEM((2, tb, TOP_K * D), F32), pltpu.SemaphoreType.DMA((2,))]),
        out_shape=jax.ShapeDtypeStruct((S, D), F32),
        compiler_params=_params("arbitrary"),
        name="moe_combine",
    )(pos, yb, h, routed, ln_out_w.reshape(1, D))


def _moe_layer(h, ln_w, wg, bg, we, be, w_gu, w_down, layer, ln_out_w, final_norm):
    T, D = h.shape
    routed, counts = _router(h, ln_w, wg, bg, we, be)
    A = T * TOP_K
    n_blocks = -(-A // ROW_BLOCK) + N_EXPERTS
    n_rows = n_blocks * ROW_BLOCK
    counts = counts[0, N_GROUPS:N_GROUPS + N_EXPERTS].astype(jnp.int32)
    padded = (counts + ROW_BLOCK - 1) // ROW_BLOCK * ROW_BLOCK
    pad_end = jnp.cumsum(padded)
    pad_start = pad_end - padded
    expert_idx = routed[:, 2:2 + TOP_K].astype(jnp.int32)
    slot = routed[:, 4:4 + TOP_K].astype(jnp.int32)
    pos = (pad_start[expert_idx] + slot).reshape(A)
    block_start = jnp.arange(n_blocks, dtype=jnp.int32) * ROW_BLOCK
    block_e = jnp.minimum(jnp.searchsorted(pad_end, block_start, side='right'), N_EXPERTS - 1).astype(jnp.int32)
    n_used = (pad_end[-1] // ROW_BLOCK).astype(jnp.int32).reshape(1)
    xs = _dispatch(pos, pad_start + counts, pad_end, n_used, h, ln_w, n_rows)
    yb = _experts(n_used, block_e, xs, w_gu, w_down, layer)
    return _combine(pos, yb, h, routed, ln_out_w, final_norm)


def _rwkv_prep_body(h_ref, halo_ref, lnw_ref, mu_ref, xs_ref):
    i = pl.program_id(0)
    lnw = lnw_ref[...]
    hn = _rms(h_ref[...], lnw)
    ts = hn.shape[0]
    last = _rms(halo_ref[...], lnw)[SUBLANES - 1:SUBLANES, :] * (i > 0).astype(F32)
    row = lax.broadcasted_iota(jnp.int32, (ts, 1), 0)
    prev = jnp.where(row == 0, last, pltpu.roll(hn, 1, axis=0))
    xx = prev - hn
    for n in range(xs_ref.shape[0]):
        xs_ref[n] = (hn + xx * mu_ref[n:n + 1, :]).astype(xs_ref.dtype)


def _rwkv_prep(h, ln_w, mu, ts=256):
    S, D = h.shape
    n_mix = mu.shape[0]
    halo_blocks = ts // SUBLANES
    return pl.pallas_call(
        _rwkv_prep_body,
        grid=(S // ts,),
        in_specs=[pl.BlockSpec((ts, D), lambda i: (i, 0)),
                  pl.BlockSpec((SUBLANES, D), lambda i: (jnp.maximum(i * halo_blocks - 1, 0), 0)),
                  pl.BlockSpec((1, D), lambda i: (0, 0)),
                  pl.BlockSpec((n_mix, D), lambda i: (0, 0))],
        out_specs=pl.BlockSpec((n_mix, ts, D), lambda i: (0, i, 0)),
        out_shape=jax.ShapeDtypeStruct((n_mix, S, D), BF16),
        compiler_params=_params("arbitrary"),
        name="rwkv_prep",
    )(h, h, ln_w.reshape(1, D), mu)


def _sigmoid(x):
    return 1.0 / (1.0 + jnp.exp(-x))


def _rwkv_lora_body(xw_ref, xa_ref, xg_ref, w0_ref, w1_ref, w2_ref, a0_ref, a1_ref, a2_ref, g1_ref, g2_ref,
                    lw_ref, a_ref, g_ref):
    z = w0_ref[...] + _dot(jnp.tanh(_dot(xw_ref[...], w1_ref[...])).astype(BF16), w2_ref[...])
    u = -z
    softplus = jnp.maximum(u, 0.0) + jnp.log1p(jnp.exp(-jnp.abs(u)))
    lw_ref[...] = -jnp.exp(-softplus - 0.5)
    a_ref[...] = _sigmoid(a0_ref[...] + _dot(_dot(xa_ref[...], a1_ref[...]).astype(BF16), a2_ref[...]))
    g_ref[...] = _dot(_sigmoid(_dot(xg_ref[...], g1_ref[...])).astype(BF16), g2_ref[...])


def _rwkv_lora(xs, w0, w1, w2, a0, a1, a2, g1, g2, ts=256):
    _, S, D = xs.shape

    def pad_lora(wa, wb):
        r = wa.shape[1]
        rp = -(-r // LANES) * LANES
        return (jnp.pad(wa, ((0, 0), (0, rp - r))).astype(BF16), jnp.pad(wb, ((0, rp - r), (0, 0))).astype(BF16))

    w1p, w2p = pad_lora(w1, w2)
    a1p, a2p = pad_lora(a1, a2)
    g1p, g2p = pad_lora(g1, g2)
    full = lambda arr: pl.BlockSpec(arr.shape, lambda i: (0,) * arr.ndim)
    vec = pl.BlockSpec((1, D), lambda i: (0, 0))
    row = pl.BlockSpec((ts, D), lambda i: (i, 0))
    return pl.pallas_call(
        _rwkv_lora_body,
        grid=(S // ts,),
        in_specs=[pl.BlockSpec((None, ts, D), lambda i: (3, i, 0)),
                  pl.BlockSpec((None, ts, D), lambda i: (4, i, 0)),
                  pl.BlockSpec((None, ts, D), lambda i: (5, i, 0)),
                  vec, full(w1p), full(w2p), vec, full(a1p), full(a2p), full(g1p), full(g2p)],
        out_specs=[row, row, row],
        out_shape=[jax.ShapeDtypeStruct((S, D), F32)] * 3,
        compiler_params=_params("arbitrary"),
        name="rwkv_lora",
    )(xs, xs, xs, w0.reshape(1, D), w1p, w2p, a0.reshape(1, D), a1p, a2p, g1p, g2p)


def _scan_chunk(r, k, v, lw, a, g, kkw, kaw, rkw, lnw, lnb, state, c):
    masks, seg_f, seg_b, tri, strict, incl, eye = c
    bf = lambda x: x.astype(BF16)
    stack = lambda xb: jnp.concatenate([xb * m for m in masks], axis=0)
    rows = lambda *xs: jnp.concatenate(xs, axis=0)
    cols = lambda *xs: jnp.concatenate(xs, axis=1)

    kk = k * kkw
    k2 = k * (1.0 + (a - 1.0) * kaw)
    sums = _segsum(rows(kk * kk, r * k2 * rkw), seg_b)
    cw = _cumsum_rows(tri, lw)
    yield
    kk = kk / jnp.maximum(jnp.sqrt(sums[:CHUNK]), 1e-12)
    bonus = sums[CHUNK:] * v
    bv = kk * a
    cl = cw[CHUNK - 1:CHUNK, :]
    at = bf(-kk * jnp.exp(cw - lw))
    rt = bf(r * jnp.exp(cw))
    dinv = jnp.exp(-cw)
    drem = jnp.exp(cl - cw)
    vb = bf(v)
    vs = stack(vb)
    gram = _dot(rows(at, rt), rows(stack(bf(bv * dinv)), stack(bf(k2 * dinv))), NT)
    bk_t = bf(rows(bv * drem, k2 * drem).T)
    decay_rows = jnp.broadcast_to(jnp.exp(cl), (GROUP, GROUP)).T
    yield
    a_ab = jnp.where(strict, gram[:CHUNK, :GROUP], 0.0)
    a_ak = bf(jnp.where(strict, gram[:CHUNK, GROUP:], 0.0))
    a_rb = bf(jnp.where(incl, gram[CHUNK:, :GROUP], 0.0))
    a_rk = bf(jnp.where(incl, gram[CHUNK:, GROUP:], 0.0))
    inv = eye + a_ab
    pw = _dot(bf(a_ab), stack(bf(a_ab)))
    xy = _dot(rows(cols(at, a_ak), cols(rt, a_rk)), rows(bf(state), vs))
    yield
    span = 2
    while span < CHUNK // 2:
        pwb = bf(pw)
        both = _dot(rows(bf(inv), pwb), stack(pwb))
        yield
        inv = inv + both[:CHUNK]
        pw = both[CHUNK:]
        span *= 2
    inv = inv + _dot(bf(inv), stack(bf(pw)))
    yield
    ub = bf(_dot(bf(inv), stack(bf(xy[:CHUNK]))))
    yield
    y = xy[CHUNK:] + _dot(a_rb, stack(ub))
    new_state = state * decay_rows + seg_f * _dot(bk_t, rows(ub, vb))
    yield
    inv_n = 1.0 / RWKV_HEAD_DIM
    mean = _segsum(y, seg_b) * inv_n
    yield
    yc = y - mean
    var = _segsum(yc * yc, seg_b) * inv_n
    yield
    yn = yc * lax.rsqrt(var + GN_EPS) * lnw + lnb
    return ((yn + bonus) * g), new_state


def _run_interleaved(gens):
    results = [None] * len(gens)
    live = list(range(len(gens)))
    while live:
        for i in list(live):
            try:
                next(gens[i])
            except StopIteration as done:
                results[i] = done.value
                live.remove(i)
    return results


def _segsum(x, seg_b):
    hi = x.astype(BF16)
    mid = (x - hi.astype(F32)).astype(BF16)
    return _dot(jnp.concatenate([hi, mid], axis=1), jnp.concatenate([seg_b, seg_b], axis=0))


def _cumsum_rows(tri, x):
    hi = x.astype(BF16)
    r1 = x - hi.astype(F32)
    mid = r1.astype(BF16)
    lo = (r1 - mid.astype(F32)).astype(BF16)
    return _dot(jnp.concatenate([tri, tri, tri], axis=1), jnp.concatenate([hi, mid, lo], axis=0))


def _scan_consts():
    shift = RWKV_HEAD_DIM.bit_length() - 1
    lane_head = lax.broadcasted_iota(jnp.int32, (1, GROUP), 1) >> shift
    masks = [(lane_head == h).astype(F32).astype(BF16) for h in range(GROUP_HEADS)]
    ri = lax.broadcasted_iota(jnp.int32, (GROUP, GROUP), 0) >> shift
    ci = lax.broadcasted_iota(jnp.int32, (GROUP, GROUP), 1) >> shift
    seg_f = (ri == ci).astype(F32)
    tri = (lax.broadcasted_iota(jnp.int32, (CHUNK, CHUNK), 1)
           <= lax.broadcasted_iota(jnp.int32, (CHUNK, CHUNK), 0)).astype(F32).astype(BF16)
    t = lax.broadcasted_iota(jnp.int32, (CHUNK, GROUP), 0)
    s = lax.broadcasted_iota(jnp.int32, (CHUNK, GROUP), 1) & (RWKV_HEAD_DIM - 1)
    return (masks, seg_f, seg_f.astype(BF16), tri, s < t, s <= t, (s == t).astype(F32))


def _scan_body(r_ref, k_ref, v_ref, lw_ref, a_ref, g_ref, kkw_ref, kaw_ref, rkw_ref, lnw_ref, lnb_ref,
               o_ref, state_ref):
    @pl.when(pl.program_id(1) == 0)
    def _():
        state_ref[...] = jnp.zeros_like(state_ref)

    n_groups = o_ref.shape[1] // GROUP

    def chunk_step(j, carry):
        consts = _scan_consts()
        rows = pl.ds(pl.multiple_of(j * CHUNK, CHUNK), CHUNK)
        lanes = [slice(p * GROUP, (p + 1) * GROUP) for p in range(n_groups)]
        args = [(r_ref[rows, ln], k_ref[rows, ln], v_ref[rows, ln], lw_ref[rows, ln], a_ref[rows, ln],
                 g_ref[rows, ln], kkw_ref[:, ln], kaw_ref[:, ln], rkw_ref[:, ln], lnw_ref[:, ln],
                 lnb_ref[:, ln], state_ref[p]) for p, ln in enumerate(lanes)]
        results = _run_interleaved([_scan_chunk(*a, consts) for a in args])
        for p, (ln, (out, new_state)) in enumerate(zip(lanes, results)):
            state_ref[p] = new_state
            o_ref[rows, ln] = out.astype(o_ref.dtype)
        return carry

    lax.fori_loop(0, o_ref.shape[0] // CHUNK, chunk_step, 0)


def _rwkv_scan(rkv, lw, a, g, k_k, k_a, r_k, ln_w, ln_b):
    _, S, D = rkv.shape
    tr, tl = min(SCAN_ROWS, S), SCAN_LANES
    blk = lambda n: pl.BlockSpec((None, tr, tl), lambda p, i: (n, i, p))
    row = pl.BlockSpec((tr, tl), lambda p, i: (i, p))
    vec = pl.BlockSpec((1, tl), lambda p, i: (0, p))
    as_row = lambda w: w.reshape(1, D)
    return pl.pallas_call(
        _scan_body,
        grid=(D // tl, S // tr),
        in_specs=[blk(0), blk(1), blk(2), row, row, row, vec, vec, vec, vec, vec],
        out_specs=row,
        out_shape=jax.ShapeDtypeStruct((S, D), BF16),
        scratch_shapes=[pltpu.VMEM((tl // GROUP, GROUP, GROUP), F32)],
        compiler_params=_params("arbitrary", "arbitrary"),
        name="rwkv_scan",
    )(rkv, rkv, rkv, lw, a, g, as_row(k_k), as_row(k_a), as_row(r_k), as_row(ln_w), as_row(ln_b))


def _rwkv_layer(h, ln_w, mu, w_rkv, w_o, j, w0, w1, w2, a0, a1, a2, g1, g2, k_k, k_a, r_k, ln_gn_w, ln_gn_b):
    xs = _rwkv_prep(h, ln_w, mu)
    rkv = _rkv_proj(xs, w_rkv, j)
    lw, a, g = _rwkv_lora(xs, w0, w1, w2, a0, a1, a2, g1, g2)
    z = _rwkv_scan(rkv, lw, a, g, k_k, k_a, r_k, ln_gn_w, ln_gn_b)
    return _out_proj(z, w_o, j, h)


def kernel(x, mem, ln_mix_w, ln_xattn_w, ln_mem_w, xattn_wq, xattn_wkv, xattn_wo, ln_ffn_w, router_group_w, router_group_b, router_expert_w, router_expert_b, moe_w_gate_up, moe_w_down, pool_w, pool_scale, rwkv_mu, rwkv_w_rkv, rwkv_w_o, rwkv_w0, rwkv_w1, rwkv_w2, rwkv_a0, rwkv_a1, rwkv_a2, rwkv_g1, rwkv_g2, rwkv_k_k, rwkv_k_a, rwkv_r_k, rwkv_ln_w, rwkv_ln_b, ln_out_w):
    B, S, D = x.shape
    depth = ln_mix_w.shape[0]
    outs = []
    for b in range(B):
        h = x[b]
        for i in range(depth):
            j = i // 2
            if i % 2 == 0:
                h = _pool_layer(h, ln_mix_w[i], pool_w[j], pool_scale[j])
            else:
                h = _rwkv_layer(h, ln_mix_w[i], rwkv_mu[j], rwkv_w_rkv, rwkv_w_o, j, rwkv_w0[j], rwkv_w1[j],
                                rwkv_w2[j], rwkv_a0[j], rwkv_a1[j], rwkv_a2[j], rwkv_g1[j], rwkv_g2[j],
                                rwkv_k_k[j], rwkv_k_a[j], rwkv_r_k[j].reshape(D), rwkv_ln_w[j], rwkv_ln_b[j])
            kv = _mem_kv(mem[b], ln_mem_w[i], xattn_wkv, i)
            h = _xattn_layer(h, ln_xattn_w[i], xattn_wq, kv, xattn_wo, i)
            h = _moe_layer(h, ln_ffn_w[i], router_group_w[i], router_group_b[i], router_expert_w[i],
                           router_expert_b[i], moe_w_gate_up, moe_w_down, i, ln_out_w,
                           final_norm=(i == depth - 1))
        outs.append(h)
    return outs[0][None] if B == 1 else jnp.stack(outs, axis=0)
```

```python
import functools

import jax
import jax.numpy as jnp
from jax import lax
from jax.experimental import pallas as pl
from jax.experimental.pallas import tpu as pltpu

F32 = jnp.float32
BF16 = jnp.bfloat16

D_MODEL = 2048
POOL_WINDOWS = (2, 4, 8, 16)
POOL_GROUP_DIM = D_MODEL // len(POOL_WINDOWS)
MAX_WINDOW = max(POOL_WINDOWS)
RWKV_HEAD_DIM = 64
GN_EPS = 64e-5
XATTN_HEADS = 4
XATTN_HEAD_DIM = D_MODEL // XATTN_HEADS
N_GROUPS = 8
EXPERTS_PER_GROUP = 8
N_EXPERTS = N_GROUPS * EXPERTS_PER_GROUP
TOP_K = 2
D_EXPERT = D_MODEL // 8
ROW_BLOCK = 128
RMS_EPS = 1e-6

LANES = 128
SUBLANES = 8
VMEM_LIMIT = 56 * 1024 * 1024

CHUNK = 64
GROUP_HEADS = 4
GROUP = GROUP_HEADS * RWKV_HEAD_DIM
ISSUE_UNROLL = 8
WEIGHT_STAGE_ROWS = 256
SCAN_ROWS = 256
SCAN_LANES = 2048

NN = (((1,), (0,)), ((), ()))
NT = (((1,), (1,)), ((), ()))


def _params(*sem):
    return pltpu.CompilerParams(dimension_semantics=sem, vmem_limit_bytes=VMEM_LIMIT)


def _rms(x, w):
    return x * lax.rsqrt(jnp.mean(x * x, axis=-1, keepdims=True) + RMS_EPS) * w


def _dot(a, b, dims=NN):
    return lax.dot_general(a, b, dims, preferred_element_type=F32)


def _split(x):
    hi = x.astype(BF16)
    return hi, (x - hi.astype(F32)).astype(BF16)


def _rkv_proj_body(x_ref, w_ref, o_ref, wb_ref):
    @pl.when(pl.program_id(2) == 0)
    def _():
        wb_ref[...] = w_ref[...].astype(BF16)

    o_ref[...] = _dot(x_ref[...], wb_ref[...])


def _rkv_proj(xs, w_rkv, j, tm=512, tn=1024):
    _, M, K = xs.shape
    B, N = w_rkv.shape[1], w_rkv.shape[3]
    return pl.pallas_call(
        _rkv_proj_body,
        grid=(B, N // tn, M // tm),
        in_specs=[pl.BlockSpec((None, tm, K), lambda b, n, i: (b, i, 0)),
                  pl.BlockSpec((None, None, K, tn), lambda b, n, i: (j, b, 0, n))],
        out_specs=pl.BlockSpec((None, tm, tn), lambda b, n, i: (b, i, n)),
        out_shape=jax.ShapeDtypeStruct((B, M, N), F32),
        scratch_shapes=[pltpu.VMEM((K, tn), BF16)],
        compiler_params=_params("arbitrary", "arbitrary", "arbitrary"),
        name="rwkv_rkv_proj",
    )(xs, w_rkv)


def _out_proj_body(x_ref, w_ref, r_ref, o_ref, wb_ref):
    @pl.when(pl.program_id(1) == 0)
    def _():
        wb_ref[...] = w_ref[...].astype(BF16)

    o_ref[...] = r_ref[...] + _dot(x_ref[...], wb_ref[...])


def _out_proj(x, w_o, j, res, tm=512, tn=1024):
    M, K = x.shape
    N = w_o.shape[2]
    return pl.pallas_call(
        _out_proj_body,
        grid=(N // tn, M // tm),
        in_specs=[pl.BlockSpec((tm, K), lambda n, i: (i, 0)),
                  pl.BlockSpec((None, K, tn), lambda n, i: (j, 0, n)),
                  pl.BlockSpec((tm, tn), lambda n, i: (i, n))],
        out_specs=pl.BlockSpec((tm, tn), lambda n, i: (i, n)),
        out_shape=jax.ShapeDtypeStruct((M, N), F32),
        scratch_shapes=[pltpu.VMEM((K, tn), BF16)],
        compiler_params=_params("arbitrary", "arbitrary"),
        name="rwkv_out_proj",
    )(x, w_o, res)


def _pool_body(h_ref, halo_ref, lnw_ref, pw_ref, ps_ref, o_ref):
    i = pl.program_id(0)
    ts = h_ref.shape[0]
    h = h_ref[...]
    lnw = lnw_ref[...]
    x = _rms(h, lnw)
    xh = _rms(halo_ref[...], lnw) * (i > 0).astype(F32)
    xe = jnp.concatenate([xh, x], axis=0)
    t = i * ts + lax.broadcasted_iota(jnp.int32, (ts, 1), 0)
    outs = []
    for g, win in enumerate(POOL_WINDOWS):
        sl = slice(g * POOL_GROUP_DIM, (g + 1) * POOL_GROUP_DIM)
        acc = xe[:, sl]
        span = 1
        while span < win:
            acc = acc + pltpu.roll(acc, span, axis=0)
            span *= 2
        cnt = jnp.minimum(t + 1, win).astype(F32)
        pooled = acc[MAX_WINDOW:, :] / cnt - x[:, sl]
        outs.append(_dot(pooled.astype(BF16), pw_ref[g]))
    y = jnp.concatenate(outs, axis=-1)
    o_ref[...] = h + y * ps_ref[...]


def _pool_layer(h, ln_w, pool_w, pool_scale, ts=512):
    S, D = h.shape
    G = pool_w.shape[0]
    halo_blocks = ts // MAX_WINDOW
    return pl.pallas_call(
        _pool_body,
        grid=(S // ts,),
        in_specs=[pl.BlockSpec((ts, D), lambda i: (i, 0)),
                  pl.BlockSpec((MAX_WINDOW, D), lambda i: (jnp.maximum(i * halo_blocks - 1, 0), 0)),
                  pl.BlockSpec((1, D), lambda i: (0, 0)),
                  pl.BlockSpec((G, POOL_GROUP_DIM, POOL_GROUP_DIM), lambda i: (0, 0, 0)),
                  pl.BlockSpec((1, D), lambda i: (0, 0))],
        out_specs=pl.BlockSpec((ts, D), lambda i: (i, 0)),
        out_shape=jax.ShapeDtypeStruct((S, D), F32),
        compiler_params=_params("arbitrary"),
        name="pool_layer",
    )(h, h, ln_w.reshape(1, D), pool_w.astype(BF16), pool_scale.reshape(1, D))


def _memkv_body(mem_ref, lnw_ref, w_ref, o_ref):
    mn = _rms(mem_ref[...], lnw_ref[...])
    o_ref[...] = _dot(mn.astype(BF16), w_ref[...].astype(BF16)).astype(o_ref.dtype)


def _mem_kv(mem, ln_w, wkv, layer, tn=1024):
    M, D = mem.shape
    N = wkv.shape[2]
    return pl.pallas_call(
        _memkv_body,
        grid=(N // tn,),
        in_specs=[pl.BlockSpec((M, D), lambda n: (0, 0)),
                  pl.BlockSpec((1, D), lambda n: (0, 0)),
                  pl.BlockSpec((None, D, tn), lambda n: (layer, 0, n))],
        out_specs=pl.BlockSpec((M, tn), lambda n: (0, n)),
        out_shape=jax.ShapeDtypeStruct((M, N), BF16),
        compiler_params=_params("arbitrary"),
        name="xattn_mem_kv",
    )(mem, ln_w.reshape(1, D), wkv)


def _load_weight_bf16(w_hbm, dst, stage, sem):
    rows = stage.shape[1]
    n_chunks = w_hbm.shape[0] // rows

    def chunk_copy(c):
        return pltpu.make_async_copy(w_hbm.at[pl.ds(c * rows, rows), :], stage.at[c % 2], sem.at[c % 2])

    chunk_copy(0).start()
    for c in range(n_chunks):
        if c + 1 < n_chunks:
            chunk_copy(c + 1).start()
        chunk_copy(c).wait()
        dst[pl.ds(c * rows, rows), :] = stage[c % 2].astype(BF16)


def _xattn_body(layer, h_ref, lnw_ref, k_ref, v_ref, wq_hbm, wo_hbm, o_ref, wq_ref, wo_ref, stage, sem):
    @pl.when(pl.program_id(0) == 0)
    def _():
        _load_weight_bf16(wq_hbm.at[layer], wq_ref, stage, sem)
        _load_weight_bf16(wo_hbm.at[layer], wo_ref, stage, sem)

    h = h_ref[...]
    hn = _rms(h, lnw_ref[...]).astype(BF16)
    q = _dot(hn, wq_ref[...])
    scale = XATTN_HEAD_DIM ** -0.5
    heads = []
    for hd in range(XATTN_HEADS):
        sl = slice(hd * XATTN_HEAD_DIM, (hd + 1) * XATTN_HEAD_DIM)
        s = _dot(q[:, sl].astype(BF16), k_ref[:, sl], NT) * scale
        s = s - jnp.max(s, axis=-1, keepdims=True)
        e = jnp.exp(s)
        p = e / jnp.sum(e, axis=-1, keepdims=True)
        heads.append(_dot(p.astype(BF16), v_ref[:, sl]))
    o = jnp.concatenate(heads, axis=-1).astype(BF16)
    o_ref[...] = h + _dot(o, wo_ref[...])


def _xattn_layer(h, ln_w, wq, kv, wo, layer, ts=256):
    S, D = h.shape
    M = kv.shape[0]
    return pl.pallas_call(
        functools.partial(_xattn_body, layer),
        grid=(S // ts,),
        in_specs=[pl.BlockSpec((ts, D), lambda i: (i, 0)),
                  pl.BlockSpec((1, D), lambda i: (0, 0)),
                  pl.BlockSpec((M, D), lambda i: (0, 0)),
                  pl.BlockSpec((M, D), lambda i: (0, 1)),
                  pl.BlockSpec(memory_space=pl.ANY),
                  pl.BlockSpec(memory_space=pl.ANY)],
        out_specs=pl.BlockSpec((ts, D), lambda i: (i, 0)),
        out_shape=jax.ShapeDtypeStruct((S, D), F32),
        scratch_shapes=[pltpu.VMEM((D, D), BF16), pltpu.VMEM((D, D), BF16),
                        pltpu.VMEM((2, WEIGHT_STAGE_ROWS, D), F32), pltpu.SemaphoreType.DMA((2,))],
        compiler_params=_params("arbitrary"),
        name="xattn_layer",
    )(h, ln_w.reshape(1, D), kv, kv, wq, wo)


def _router_body(h_ref, lnw_ref, whi_ref, wlo_ref, b_ref, r_ref, rt_ref, cnt_ref):
    i = pl.program_id(0)

    @pl.when(i == 0)
    def _():
        cnt_ref[...] = jnp.zeros_like(cnt_ref)

    hn = _rms(h_ref[...], lnw_ref[...])
    xh, xl = _split(hn)
    whi = whi_ref[...]
    lg = _dot(xh, whi) + (_dot(xh, wlo_ref[...]) + _dot(xl, whi)) + b_ref[...]
    ts = lg.shape[0]
    lane = lax.broadcasted_iota(jnp.int32, (ts, LANES), 1)
    lanef = lane.astype(F32)
    neg = jnp.float32(-jnp.inf)
    gl = jnp.where(lane < N_GROUPS, lg, neg)
    gmax = jnp.max(gl, axis=-1, keepdims=True)
    pg_top = 1.0 / jnp.sum(jnp.exp(gl - gmax), axis=-1, keepdims=True)
    g_idx = jnp.min(jnp.where(gl == gmax, lanef, float(LANES)), axis=-1, keepdims=True)
    lo_lane = g_idx * EXPERTS_PER_GROUP + N_GROUPS
    el = jnp.where(lanef >= lo_lane, jnp.where(lanef < lo_lane + EXPERTS_PER_GROUP, lg, neg), neg)
    m1 = jnp.max(el, axis=-1, keepdims=True)
    i1 = jnp.min(jnp.where(el == m1, lanef, float(LANES)), axis=-1, keepdims=True)
    el2 = jnp.where(lanef == i1, neg, el)
    m2 = jnp.max(el2, axis=-1, keepdims=True)
    i2 = jnp.min(jnp.where(el2 == m2, lanef, float(LANES)), axis=-1, keepdims=True)
    e2 = jnp.exp(m2 - m1)
    g1 = pg_top / (1.0 + e2)
    g2 = pg_top * e2 / (1.0 + e2)
    hit1 = (lanef == i1).astype(F32)
    hit2 = (lanef == i2).astype(F32)
    hits = hit1 + hit2
    earlier = (lax.broadcasted_iota(jnp.int32, (ts, ts), 1)
               < lax.broadcasted_iota(jnp.int32, (ts, ts), 0)).astype(F32).astype(BF16)
    before = _dot(earlier, hits.astype(BF16)) + cnt_ref[...]
    r1 = jnp.sum(before * hit1, axis=-1, keepdims=True)
    r2 = jnp.sum(before * hit2, axis=-1, keepdims=True)
    cnt_ref[...] += jnp.sum(hits, axis=0, keepdims=True)
    out = jnp.where(lane == 0, g1, 0.0)
    out = jnp.where(lane == 1, g2, out)
    out = jnp.where(lane == 2, i1 - N_GROUPS, out)
    out = jnp.where(lane == 3, i2 - N_GROUPS, out)
    out = jnp.where(lane == 4, r1, out)
    out = jnp.where(lane == 5, r2, out)
    r_ref[...] = out
    rt_ref[...] = out.T[:SUBLANES, :]


def _router(h, ln_w, wg, bg, we, be, ts=512):
    S, D = h.shape
    unused = LANES - N_GROUPS - N_EXPERTS
    w = jnp.concatenate([wg, we, jnp.zeros((D, unused), F32)], axis=1)
    b = jnp.concatenate([bg, be, jnp.zeros((unused,), F32)]).reshape(1, LANES)
    whi, wlo = _split(w)
    return pl.pallas_call(
        _router_body,
        grid=(S // ts,),
        in_specs=[pl.BlockSpec((ts, D), lambda i: (i, 0)),
                  pl.BlockSpec((1, D), lambda i: (0, 0)),
                  pl.BlockSpec((D, LANES), lambda i: (0, 0)),
                  pl.BlockSpec((D, LANES), lambda i: (0, 0)),
                  pl.BlockSpec((1, LANES), lambda i: (0, 0))],
        out_specs=[pl.BlockSpec((ts, LANES), lambda i: (i, 0)),
                   pl.BlockSpec((SUBLANES, ts), lambda i: (0, i)),
                   pl.BlockSpec((1, LANES), lambda i: (0, 0))],
        out_shape=[jax.ShapeDtypeStruct((S, LANES), F32), jax.ShapeDtypeStruct((SUBLANES, S), F32),
                   jax.ShapeDtypeStruct((1, LANES), F32)],
        compiler_params=_params("arbitrary"),
        name="moe_router",
    )(h, ln_w.reshape(1, D), whi, wlo, b)


def _dispatch_body(pos_ref, fill_lo_ref, fill_hi_ref, nblk_ref, h_ref, lnw_ref, xs_hbm, xn, zeros, sem, zsem):
    i = pl.program_id(0)
    n = pl.num_programs(0)
    tb = h_ref.shape[0]
    slot = i % 2

    def row_copy(r, dst_row):
        return pltpu.make_async_copy(xn.at[slot, pl.ds(r, 1), :], xs_hbm.at[pl.ds(dst_row, 1), :], sem.at[slot])

    def wait_slot(s):
        for _ in range(TOP_K):
            pltpu.make_async_copy(xn.at[s], xs_hbm.at[pl.ds(0, tb), :], sem.at[s]).wait()

    @pl.when(i >= 2)
    def _():
        wait_slot(slot)

    xn[slot] = _rms(h_ref[...], lnw_ref[...])

    n_tok = n * tb

    def issue(r, carry):
        for k in range(TOP_K):
            row_copy(r, pos_ref[k * n_tok + i * tb + r]).start(priority=k % 2)
        return carry
    lax.fori_loop(0, tb, issue, 0, unroll=ISSUE_UNROLL)

    @pl.when(i == n - 1)
    def _():
        zeros[...] = jnp.zeros_like(zeros)

        def zero_row(row):
            return pltpu.make_async_copy(zeros.at[pl.ds(0, 1), :], xs_hbm.at[pl.ds(row, 1), :], zsem)

        def zero_block(b):
            return pltpu.make_async_copy(zeros, xs_hbm.at[pl.ds(b * ROW_BLOCK, ROW_BLOCK), :], zsem)

        def per_expert(e, carry):
            lo, hi = fill_lo_ref[e], fill_hi_ref[e]
            lax.fori_loop(lo, hi, lambda row, c: (zero_row(row).start(), c)[1], 0)
            lax.fori_loop(lo, hi, lambda row, c: (zero_row(row).wait(), c)[1], 0)
            return carry
        lax.fori_loop(0, N_EXPERTS, per_expert, 0)
        n_blocks = xs_hbm.shape[0] // ROW_BLOCK
        lax.fori_loop(nblk_ref[0], n_blocks, lambda b, c: (zero_block(b).start(), c)[1], 0)
        lax.fori_loop(nblk_ref[0], n_blocks, lambda b, c: (zero_block(b).wait(), c)[1], 0)
        wait_slot(slot)

        @pl.when(n >= 2)
        def _():
            wait_slot(1 - slot)


def _dispatch(pos, fill_lo, fill_hi, n_used, h, ln_w, n_rows, tb=256):
    S, D = h.shape
    return pl.pallas_call(
        _dispatch_body,
        grid_spec=pltpu.PrefetchScalarGridSpec(
            num_scalar_prefetch=4,
            grid=(S // tb,),
            in_specs=[pl.BlockSpec((tb, D), lambda i, *_: (i, 0)),
                      pl.BlockSpec((1, D), lambda i, *_: (0, 0))],
            out_specs=pl.BlockSpec(memory_space=pl.ANY),
            scratch_shapes=[pltpu.VMEM((2, tb, D), F32), pltpu.VMEM((ROW_BLOCK, D), F32),
                            pltpu.SemaphoreType.DMA((2,)), pltpu.SemaphoreType.DMA(())]),
        out_shape=jax.ShapeDtypeStruct((n_rows, D), F32),
        compiler_params=_params("arbitrary"),
        name="moe_dispatch",
    )(pos, fill_lo, fill_hi, n_used, h, ln_w.reshape(1, D))


def _experts_body(layer, nblk_ref, be_ref, fresh_ref, wslot_ref, next_e_ref, x_ref, wgu_hbm, wd_hbm, o_ref,
                  wgu_f, wd_f, wgu_b, wd_b, sem):
    i = pl.program_id(0)
    n_used = nblk_ref[0]

    def weight_copies(e, slot):
        return (pltpu.make_async_copy(wgu_hbm.at[layer, e], wgu_f.at[slot], sem.at[0, slot]),
                pltpu.make_async_copy(wd_hbm.at[layer, e], wd_f.at[slot], sem.at[1, slot]))

    @pl.when(i == 0)
    def _():
        for cp in weight_copies(be_ref[0], 0):
            cp.start()

    @pl.when(jnp.logical_and(i < n_used, fresh_ref[i] == 1))
    def _():
        slot = wslot_ref[i]
        for cp in weight_copies(be_ref[i], slot):
            cp.wait()
        wgu_b[...] = wgu_f[slot].astype(BF16)
        wd_b[...] = wd_f[slot].astype(BF16)

        @pl.when(next_e_ref[i] >= 0)
        def _():
            for cp in weight_copies(next_e_ref[i], 1 - slot):
                cp.start()

    @pl.when(i < n_used)
    def _():
        gu = _dot(x_ref[...].astype(BF16), wgu_b[...])
        gg = gu[:, :D_EXPERT]
        uu = gu[:, D_EXPERT:]
        act = (gg * (1.0 / (1.0 + jnp.exp(-gg))) * uu).astype(BF16)
        o_ref[...] = _dot(act, wd_b[...])

    @pl.when(i >= n_used)
    def _():
        o_ref[...] = jnp.zeros_like(o_ref)


def _experts(n_used, block_e, xs, w_gu, w_down, layer):
    n_rows, D = xs.shape
    n_blocks = n_rows // ROW_BLOCK
    F2 = w_gu.shape[3]
    idx = jnp.arange(n_blocks, dtype=jnp.int32)
    used = idx < n_used[0]
    fresh = jnp.logical_and(used, jnp.concatenate([jnp.ones((1,), bool), block_e[1:] != block_e[:-1]]))
    run = jnp.cumsum(fresh.astype(jnp.int32)) - 1
    run_start = jnp.where(fresh, idx, n_blocks)
    next_start = lax.cummin(jnp.concatenate([run_start[1:], jnp.full((1,), n_blocks, jnp.int32)]), reverse=True)
    next_e = jnp.where(next_start < n_blocks, block_e[jnp.minimum(next_start, n_blocks - 1)], -1)
    return pl.pallas_call(
        functools.partial(_experts_body, layer),
        grid_spec=pltpu.PrefetchScalarGridSpec(
            num_scalar_prefetch=5,
            grid=(n_blocks,),
            in_specs=[pl.BlockSpec((ROW_BLOCK, D), lambda i, nb, *_: (jnp.minimum(i, nb[0] - 1), 0)),
                      pl.BlockSpec(memory_space=pl.ANY),
                      pl.BlockSpec(memory_space=pl.ANY)],
            out_specs=pl.BlockSpec((ROW_BLOCK, D), lambda i, *_: (i, 0)),
            scratch_shapes=[pltpu.VMEM((2, D, F2), F32), pltpu.VMEM((2, F2 // 2, D), F32),
                            pltpu.VMEM((D, F2), BF16), pltpu.VMEM((F2 // 2, D), BF16),
                            pltpu.SemaphoreType.DMA((2, 2))]),
        out_shape=jax.ShapeDtypeStruct((n_rows, D), F32),
        compiler_params=_params("arbitrary"),
        name="moe_experts",
    )(n_used, block_e, fresh.astype(jnp.int32), (run % 2).astype(jnp.int32), next_e.astype(jnp.int32),
      xs, w_gu, w_down)


def _start_pair_gather(pos_ref, first_tok, n_tok, yb_hbm, ybuf, slot, sem, tb):
    D = yb_hbm.shape[1]

    def issue(r, carry):
        for k in range(TOP_K):
            pltpu.make_async_copy(yb_hbm.at[pl.ds(pos_ref[k * n_tok + first_tok + r], 1), :],
                                  ybuf.at[slot, pl.ds(r, 1), pl.ds(k * D, D)], sem.at[slot]).start(priority=k % 2)
        return carry
    lax.fori_loop(0, tb, issue, 0, unroll=ISSUE_UNROLL)


def _wait_pair_gather(yb_hbm, ybuf, slot, sem, tb):
    D = yb_hbm.shape[1]
    for k in range(TOP_K):
        pltpu.make_async_copy(yb_hbm.at[pl.ds(0, tb), :], ybuf.at[slot, :, pl.ds(k * D, D)], sem.at[slot]).wait()


def _combine_body(final_norm, pos_ref, yb_hbm, h_ref, r_ref, lnw_ref, o_ref, ybuf, sem):
    i = pl.program_id(0)
    n = pl.num_programs(0)
    tb, D = h_ref.shape
    slot = i % 2

    @pl.when(i == 0)
    def _():
        _start_pair_gather(pos_ref, 0, n * tb, yb_hbm, ybuf, 0, sem, tb)

    @pl.when(i + 1 < n)
    def _():
        _start_pair_gather(pos_ref, (i + 1) * tb, n * tb, yb_hbm, ybuf, 1 - slot, sem, tb)

    _wait_pair_gather(yb_hbm, ybuf, slot, sem, tb)
    out = h_ref[...]
    for k in range(TOP_K):
        out = out + ybuf[slot, :, k * D:(k + 1) * D] * r_ref[:, k:k + 1]
    if final_norm:
        out = _rms(out, lnw_ref[...])
    o_ref[...] = out


def _combine(pos, yb, h, routed, ln_out_w, final_norm, tb=128):
    S, D = h.shape
    return pl.pallas_call(
        functools.partial(_combine_body, final_norm),
        grid_spec=pltpu.PrefetchScalarGridSpec(
            num_scalar_prefetch=1,
            grid=(S // tb,),
            in_specs=[pl.BlockSpec(memory_space=pl.ANY),
                      pl.BlockSpec((tb, D), lambda i, pos: (i, 0)),
                      pl.BlockSpec((tb, LANES), lambda i, pos: (i, 0)),
                      pl.BlockSpec((1, D), lambda i, pos: (0, 0))],
            out_specs=pl.BlockSpec((tb, D), lambda i, pos: (i, 0)),
            scratch_shapes=[pltpu.VMEM((2, tb, TOP_K * D), F32), pltpu.SemaphoreType.DMA((2,))]),
        out_shape=jax.ShapeDtypeStruct((S, D), F32),
        compiler_params=_params("arbitrary"),
        name="moe_combine",
    )(pos, yb, h, routed, ln_out_w.reshape(1, D))


def _moe_layer(h, ln_w, wg, bg, we, be, w_gu, w_down, layer, ln_out_w, final_norm):
    T, D = h.shape
    routed, routed_t, counts = _router(h, ln_w, wg, bg, we, be)
    A = T * TOP_K
    n_blocks = -(-A // ROW_BLOCK) + N_EXPERTS
    n_rows = n_blocks * ROW_BLOCK
    counts = counts[0, N_GROUPS:N_GROUPS + N_EXPERTS].astype(jnp.int32)
    padded = (counts + ROW_BLOCK - 1) // ROW_BLOCK * ROW_BLOCK
    pad_end = jnp.cumsum(padded)
    pad_start = pad_end - padded
    expert_idx = routed_t[2:2 + TOP_K].astype(jnp.int32)
    slot = routed_t[4:4 + TOP_K].astype(jnp.int32)
    pos = (pad_start[expert_idx] + slot).reshape(A)
    block_start = jnp.arange(n_blocks, dtype=jnp.int32) * ROW_BLOCK
    block_e = jnp.minimum(jnp.searchsorted(pad_end, block_start, side='right'), N_EXPERTS - 1).astype(jnp.int32)
    n_used = (pad_end[-1] // ROW_BLOCK).astype(jnp.int32).reshape(1)
    xs = _dispatch(pos, pad_start + counts, pad_end, n_used, h, ln_w, n_rows)
    yb = _experts(n_used, block_e, xs, w_gu, w_down, layer)
    return _combine(pos, yb, h, routed, ln_out_w, final_norm)


def _rwkv_prep_body(h_ref, halo_ref, lnw_ref, mu_ref, xs_ref):
    i = pl.program_id(0)
    lnw = lnw_ref[...]
    hn = _rms(h_ref[...], lnw)
    ts = hn.shape[0]
    last = _rms(halo_ref[...], lnw)[SUBLANES - 1:SUBLANES, :] * (i > 0).astype(F32)
    row = lax.broadcasted_iota(jnp.int32, (ts, 1), 0)
    prev = jnp.where(row == 0, last, pltpu.roll(hn, 1, axis=0))
    xx = prev - hn
    for n in range(xs_ref.shape[0]):
        xs_ref[n] = (hn + xx * mu_ref[n:n + 1, :]).astype(xs_ref.dtype)


def _rwkv_prep(h, ln_w, mu, ts=256):
    S, D = h.shape
    n_mix = mu.shape[0]
    halo_blocks = ts // SUBLANES
    return pl.pallas_call(
        _rwkv_prep_body,
        grid=(S // ts,),
        in_specs=[pl.BlockSpec((ts, D), lambda i: (i, 0)),
                  pl.BlockSpec((SUBLANES, D), lambda i: (jnp.maximum(i * halo_blocks - 1, 0), 0)),
                  pl.BlockSpec((1, D), lambda i: (0, 0)),
                  pl.BlockSpec((n_mix, D), lambda i: (0, 0))],
        out_specs=pl.BlockSpec((n_mix, ts, D), lambda i: (0, i, 0)),
        out_shape=jax.ShapeDtypeStruct((n_mix, S, D), BF16),
        compiler_params=_params("arbitrary"),
        name="rwkv_prep",
    )(h, h, ln_w.reshape(1, D), mu)


def _sigmoid(x):
    return 1.0 / (1.0 + jnp.exp(-x))


def _rwkv_lora_body(xw_ref, xa_ref, xg_ref, w0_ref, w1_ref, w2_ref, a0_ref, a1_ref, a2_ref, g1_ref, g2_ref,
                    lw_ref, a_ref, g_ref):
    z = w0_ref[...] + _dot(jnp.tanh(_dot(xw_ref[...], w1_ref[...])).astype(BF16), w2_ref[...])
    u = -z
    softplus = jnp.maximum(u, 0.0) + jnp.log1p(jnp.exp(-jnp.abs(u)))
    lw_ref[...] = -jnp.exp(-softplus - 0.5)
    a_ref[...] = _sigmoid(a0_ref[...] + _dot(_dot(xa_ref[...], a1_ref[...]).astype(BF16), a2_ref[...]))
    g_ref[...] = _dot(_sigmoid(_dot(xg_ref[...], g1_ref[...])).astype(BF16), g2_ref[...])


def _rwkv_lora(xs, w0, w1, w2, a0, a1, a2, g1, g2, ts=256):
    _, S, D = xs.shape

    def pad_lora(wa, wb):
        r = wa.shape[1]
        rp = -(-r // LANES) * LANES
        return (jnp.pad(wa, ((0, 0), (0, rp - r))).astype(BF16), jnp.pad(wb, ((0, rp - r), (0, 0))).astype(BF16))

    w1p, w2p = pad_lora(w1, w2)
    a1p, a2p = pad_lora(a1, a2)
    g1p, g2p = pad_lora(g1, g2)
    full = lambda arr: pl.BlockSpec(arr.shape, lambda i: (0,) * arr.ndim)
    vec = pl.BlockSpec((1, D), lambda i: (0, 0))
    row = pl.BlockSpec((ts, D), lambda i: (i, 0))
    return pl.pallas_call(
        _rwkv_lora_body,
        grid=(S // ts,),
        in_specs=[pl.BlockSpec((None, ts, D), lambda i: (3, i, 0)),
                  pl.BlockSpec((None, ts, D), lambda i: (4, i, 0)),
                  pl.BlockSpec((None, ts, D), lambda i: (5, i, 0)),
                  vec, full(w1p), full(w2p), vec, full(a1p), full(a2p), full(g1p), full(g2p)],
        out_specs=[row, row, row],
        out_shape=[jax.ShapeDtypeStruct((S, D), F32)] * 3,
        compiler_params=_params("arbitrary"),
        name="rwkv_lora",
    )(xs, xs, xs, w0.reshape(1, D), w1p, w2p, a0.reshape(1, D), a1p, a2p, g1p, g2p)


def _scan_chunk(r, k, v, lw, a, g, kkw, kaw, rkw, lnw, lnb, state, c):
    masks, seg_f, seg_b, tri, strict, incl, eye = c
    bf = lambda x: x.astype(BF16)
    stack = lambda xb: jnp.concatenate([xb * m for m in masks], axis=0)
    rows = lambda *xs: jnp.concatenate(xs, axis=0)
    cols = lambda *xs: jnp.concatenate(xs, axis=1)

    kk = k * kkw
    k2 = k * (1.0 + (a - 1.0) * kaw)
    sums = _segsum(rows(kk * kk, r * k2 * rkw), seg_b)
    cw = _cumsum_rows(tri, lw)
    yield
    kk = kk / jnp.maximum(jnp.sqrt(sums[:CHUNK]), 1e-12)
    bonus = sums[CHUNK:] * v
    bv = kk * a
    cl = cw[CHUNK - 1:CHUNK, :]
    at = bf(-kk * jnp.exp(cw - lw))
    rt = bf(r * jnp.exp(cw))
    dinv = jnp.exp(-cw)
    drem = jnp.exp(cl - cw)
    vb = bf(v)
    vs = stack(vb)
    gram = _dot(rows(at, rt), rows(stack(bf(bv * dinv)), stack(bf(k2 * dinv))), NT)
    bk_t = bf(rows(bv * drem, k2 * drem).T)
    decay_rows = jnp.broadcast_to(jnp.exp(cl), (GROUP, GROUP)).T
    yield
    a_ab = jnp.where(strict, gram[:CHUNK, :GROUP], 0.0)
    a_ak = bf(jnp.where(strict, gram[:CHUNK, GROUP:], 0.0))
    a_rb = bf(jnp.where(incl, gram[CHUNK:, :GROUP], 0.0))
    a_rk = bf(jnp.where(incl, gram[CHUNK:, GROUP:], 0.0))
    inv = eye + a_ab
    pw = _dot(bf(a_ab), stack(bf(a_ab)))
    xy = _dot(rows(cols(at, a_ak), cols(rt, a_rk)), rows(bf(state), vs))
    yield
    span = 2
    while span < CHUNK // 2:
        pwb = bf(pw)
        both = _dot(rows(bf(inv), pwb), stack(pwb))
        yield
        inv = inv + both[:CHUNK]
        pw = both[CHUNK:]
        span *= 2
    inv = inv + _dot(bf(inv), stack(bf(pw)))
    yield
    ub = bf(_dot(bf(inv), stack(bf(xy[:CHUNK]))))
    yield
    y = xy[CHUNK:] + _dot(a_rb, stack(ub))
    new_state = state * decay_rows + seg_f * _dot(bk_t, rows(ub, vb))
    yield
    inv_n = 1.0 / RWKV_HEAD_DIM
    mean = _segsum(y, seg_b) * inv_n
    yield
    yc = y - mean
    var = _segsum(yc * yc, seg_b) * inv_n
    yield
    yn = yc * lax.rsqrt(var + GN_EPS) * lnw + lnb
    return ((yn + bonus) * g), new_state


def _run_interleaved(gens):
    results = [None] * len(gens)
    live = list(range(len(gens)))
    while live:
        for i in list(live):
            try:
                next(gens[i])
            except StopIteration as done:
                results[i] = done.value
                live.remove(i)
    return results


def _segsum(x, seg_b):
    hi = x.astype(BF16)
    mid = (x - hi.astype(F32)).astype(BF16)
    return _dot(jnp.concatenate([hi, mid], axis=1), jnp.concatenate([seg_b, seg_b], axis=0))


def _cumsum_rows(tri, x):
    hi = x.astype(BF16)
    r1 = x - hi.astype(F32)
    mid = r1.astype(BF16)
    lo = (r1 - mid.astype(F32)).astype(BF16)
    return _dot(jnp.concatenate([tri, tri, tri], axis=1), jnp.concatenate([hi, mid, lo], axis=0))


def _scan_consts():
    shift = RWKV_HEAD_DIM.bit_length() - 1
    lane_head = lax.broadcasted_iota(jnp.int32, (1, GROUP), 1) >> shift
    masks = [(lane_head == h).astype(F32).astype(BF16) for h in range(GROUP_HEADS)]
    ri = lax.broadcasted_iota(jnp.int32, (GROUP, GROUP), 0) >> shift
    ci = lax.broadcasted_iota(jnp.int32, (GROUP, GROUP), 1) >> shift
    seg_f = (ri == ci).astype(F32)
    tri = (lax.broadcasted_iota(jnp.int32, (CHUNK, CHUNK), 1)
           <= lax.broadcasted_iota(jnp.int32, (CHUNK, CHUNK), 0)).astype(F32).astype(BF16)
    t = lax.broadcasted_iota(jnp.int32, (CHUNK, GROUP), 0)
    s = lax.broadcasted_iota(jnp.int32, (CHUNK, GROUP), 1) & (RWKV_HEAD_DIM - 1)
    return (masks, seg_f, seg_f.astype(BF16), tri, s < t, s <= t, (s == t).astype(F32))


def _scan_body(r_ref, k_ref, v_ref, lw_ref, a_ref, g_ref, kkw_ref, kaw_ref, rkw_ref, lnw_ref, lnb_ref,
               o_ref, state_ref):
    @pl.when(pl.program_id(1) == 0)
    def _():
        state_ref[...] = jnp.zeros_like(state_ref)

    n_groups = o_ref.shape[1] // GROUP

    def chunk_step(j, carry):
        consts = _scan_consts()
        rows = pl.ds(pl.multiple_of(j * CHUNK, CHUNK), CHUNK)
        lanes = [slice(p * GROUP, (p + 1) * GROUP) for p in range(n_groups)]
        args = [(r_ref[rows, ln], k_ref[rows, ln], v_ref[rows, ln], lw_ref[rows, ln], a_ref[rows, ln],
                 g_ref[rows, ln], kkw_ref[:, ln], kaw_ref[:, ln], rkw_ref[:, ln], lnw_ref[:, ln],
                 lnb_ref[:, ln], state_ref[p]) for p, ln in enumerate(lanes)]
        results = _run_interleaved([_scan_chunk(*a, consts) for a in args])
        for p, (ln, (out, new_state)) in enumerate(zip(lanes, results)):
            state_ref[p] = new_state
            o_ref[rows, ln] = out.astype(o_ref.dtype)
        return carry

    lax.fori_loop(0, o_ref.shape[0] // CHUNK, chunk_step, 0)


def _rwkv_scan(rkv, lw, a, g, k_k, k_a, r_k, ln_w, ln_b):
    _, S, D = rkv.shape
    tr, tl = min(SCAN_ROWS, S), SCAN_LANES
    blk = lambda n: pl.BlockSpec((None, tr, tl), lambda p, i: (n, i, p))
    row = pl.BlockSpec((tr, tl), lambda p, i: (i, p))
    vec = pl.BlockSpec((1, tl), lambda p, i: (0, p))
    as_row = lambda w: w.reshape(1, D)
    return pl.pallas_call(
        _scan_body,
        grid=(D // tl, S // tr),
        in_specs=[blk(0), blk(1), blk(2), row, row, row, vec, vec, vec, vec, vec],
        out_specs=row,
        out_shape=jax.ShapeDtypeStruct((S, D), BF16),
        scratch_shapes=[pltpu.VMEM((tl // GROUP, GROUP, GROUP), F32)],
        compiler_params=_params("arbitrary", "arbitrary"),
        name="rwkv_scan",
    )(rkv, rkv, rkv, lw, a, g, as_row(k_k), as_row(k_a), as_row(r_k), as_row(ln_w), as_row(ln_b))


def _rwkv_layer(h, ln_w, mu, w_rkv, w_o, j, w0, w1, w2, a0, a1, a2, g1, g2, k_k, k_a, r_k, ln_gn_w, ln_gn_b):
    xs = _rwkv_prep(h, ln_w, mu)
    rkv = _rkv_proj(xs, w_rkv, j)
    lw, a, g = _rwkv_lora(xs, w0, w1, w2, a0, a1, a2, g1, g2)
    z = _rwkv_scan(rkv, lw, a, g, k_k, k_a, r_k, ln_gn_w, ln_gn_b)
    return _out_proj(z, w_o, j, h)


def kernel(x, mem, ln_mix_w, ln_xattn_w, ln_mem_w, xattn_wq, xattn_wkv, xattn_wo, ln_ffn_w, router_group_w, router_group_b, router_expert_w, router_expert_b, moe_w_gate_up, moe_w_down, pool_w, pool_scale, rwkv_mu, rwkv_w_rkv, rwkv_w_o, rwkv_w0, rwkv_w1, rwkv_w2, rwkv_a0, rwkv_a1, rwkv_a2, rwkv_g1, rwkv_g2, rwkv_k_k, rwkv_k_a, rwkv_r_k, rwkv_ln_w, rwkv_ln_b, ln_out_w):
    B, S, D = x.shape
    depth = ln_mix_w.shape[0]
    outs = []
    for b in range(B):
        h = x[b]
        for i in range(depth):
            j = i // 2
            if i % 2 == 0:
                h = _pool_layer(h, ln_mix_w[i], pool_w[j], pool_scale[j])
            else:
                h = _rwkv_layer(h, ln_mix_w[i], rwkv_mu[j], rwkv_w_rkv, rwkv_w_o, j, rwkv_w0[j], rwkv_w1[j],
                                rwkv_w2[j], rwkv_a0[j], rwkv_a1[j], rwkv_a2[j], rwkv_g1[j], rwkv_g2[j],
                                rwkv_k_k[j], rwkv_k_a[j], rwkv_r_k[j].reshape(D), rwkv_ln_w[j], rwkv_ln_b[j])
            kv = _mem_kv(mem[b], ln_mem_w[i], xattn_wkv, i)
            h = _xattn_layer(h, ln_xattn_w[i], xattn_wq, kv, xattn_wo, i)
            h = _moe_layer(h, ln_ffn_w[i], router_group_w[i], router_group_b[i], router_expert_w[i],
                           router_expert_b[i], moe_w_gate_up, moe_w_down, i, ln_out_w,
                           final_norm=(i == depth - 1))
        outs.append(h)
    return outs[0][None] if B == 1 else jnp.stack(outs, axis=0)
```

```python
import functools

import jax
import jax.numpy as jnp
from jax import lax
from jax.experimental import pallas as pl
from jax.experimental.pallas import tpu as pltpu

F32 = jnp.float32
BF16 = jnp.bfloat16

D_MODEL = 2048
POOL_WINDOWS = (2, 4, 8, 16)
POOL_GROUP_DIM = D_MODEL // len(POOL_WINDOWS)
MAX_WINDOW = max(POOL_WINDOWS)
RWKV_HEAD_DIM = 64
GN_EPS = 64e-5
XATTN_HEADS = 4
XATTN_HEAD_DIM = D_MODEL // XATTN_HEADS
N_GROUPS = 8
EXPERTS_PER_GROUP = 8
N_EXPERTS = N_GROUPS * EXPERTS_PER_GROUP
TOP_K = 2
D_EXPERT = D_MODEL // 8
ROW_BLOCK = 128
RMS_EPS = 1e-6

LANES = 128
SUBLANES = 8
VMEM_LIMIT = 56 * 1024 * 1024

CHUNK = 64
GROUP_HEADS = 4
GROUP = GROUP_HEADS * RWKV_HEAD_DIM
ISSUE_UNROLL = 8
WEIGHT_STAGE_ROWS = 256
SCAN_ROWS = 256
SCAN_LANES = 2048

NN = (((1,), (0,)), ((), ()))
NT = (((1,), (1,)), ((), ()))


def _params(*sem):
    return pltpu.CompilerParams(dimension_semantics=sem, vmem_limit_bytes=VMEM_LIMIT)


def _rms(x, w):
    return x * lax.rsqrt(jnp.mean(x * x, axis=-1, keepdims=True) + RMS_EPS) * w


def _dot(a, b, dims=NN):
    return lax.dot_general(a, b, dims, preferred_element_type=F32)


def _split(x):
    hi = x.astype(BF16)
    return hi, (x - hi.astype(F32)).astype(BF16)


def _rkv_proj_body(x_ref, w_ref, o_ref, wb_ref):
    @pl.when(pl.program_id(2) == 0)
    def _():
        wb_ref[...] = w_ref[...].astype(BF16)

    o_ref[...] = _dot(x_ref[...], wb_ref[...])


def _rkv_proj(xs, w_rkv, j, tm=512, tn=1024):
    _, M, K = xs.shape
    B, N = w_rkv.shape[1], w_rkv.shape[3]
    return pl.pallas_call(
        _rkv_proj_body,
        grid=(B, N // tn, M // tm),
        in_specs=[pl.BlockSpec((None, tm, K), lambda b, n, i: (b, i, 0)),
                  pl.BlockSpec((None, None, K, tn), lambda b, n, i: (j, b, 0, n))],
        out_specs=pl.BlockSpec((None, tm, tn), lambda b, n, i: (b, i, n)),
        out_shape=jax.ShapeDtypeStruct((B, M, N), F32),
        scratch_shapes=[pltpu.VMEM((K, tn), BF16)],
        compiler_params=_params("arbitrary", "arbitrary", "arbitrary"),
        name="rwkv_rkv_proj",
    )(xs, w_rkv)


def _out_proj_body(x_ref, w_ref, r_ref, o_ref, wb_ref):
    @pl.when(pl.program_id(1) == 0)
    def _():
        wb_ref[...] = w_ref[...].astype(BF16)

    o_ref[...] = r_ref[...] + _dot(x_ref[...], wb_ref[...])


def _out_proj(x, w_o, j, res, tm=512, tn=1024):
    M, K = x.shape
    N = w_o.shape[2]
    return pl.pallas_call(
        _out_proj_body,
        grid=(N // tn, M // tm),
        in_specs=[pl.BlockSpec((tm, K), lambda n, i: (i, 0)),
                  pl.BlockSpec((None, K, tn), lambda n, i: (j, 0, n)),
                  pl.BlockSpec((tm, tn), lambda n, i: (i, n))],
        out_specs=pl.BlockSpec((tm, tn), lambda n, i: (i, n)),
        out_shape=jax.ShapeDtypeStruct((M, N), F32),
        scratch_shapes=[pltpu.VMEM((K, tn), BF16)],
        compiler_params=_params("arbitrary", "arbitrary"),
        name="rwkv_out_proj",
    )(x, w_o, res)


def _pool_body(h_ref, halo_ref, lnw_ref, pw_ref, ps_ref, o_ref):
    i = pl.program_id(0)
    ts = h_ref.shape[0]
    h = h_ref[...]
    lnw = lnw_ref[...]
    x = _rms(h, lnw)
    xh = _rms(halo_ref[...], lnw) * (i > 0).astype(F32)
    xe = jnp.concatenate([xh, x], axis=0)
    t = i * ts + lax.broadcasted_iota(jnp.int32, (ts, 1), 0)
    outs = []
    for g, win in enumerate(POOL_WINDOWS):
        sl = slice(g * POOL_GROUP_DIM, (g + 1) * POOL_GROUP_DIM)
        acc = xe[:, sl]
        span = 1
        while span < win:
            acc = acc + pltpu.roll(acc, span, axis=0)
            span *= 2
        cnt = jnp.minimum(t + 1, win).astype(F32)
        pooled = acc[MAX_WINDOW:, :] / cnt - x[:, sl]
        outs.append(_dot(pooled.astype(BF16), pw_ref[g]))
    y = jnp.concatenate(outs, axis=-1)
    o_ref[...] = h + y * ps_ref[...]


def _pool_layer(h, ln_w, pool_w, pool_scale, ts=512):
    S, D = h.shape
    G = pool_w.shape[0]
    halo_blocks = ts // MAX_WINDOW
    return pl.pallas_call(
        _pool_body,
        grid=(S // ts,),
        in_specs=[pl.BlockSpec((ts, D), lambda i: (i, 0)),
                  pl.BlockSpec((MAX_WINDOW, D), lambda i: (jnp.maximum(i * halo_blocks - 1, 0), 0)),
                  pl.BlockSpec((1, D), lambda i: (0, 0)),
                  pl.BlockSpec((G, POOL_GROUP_DIM, POOL_GROUP_DIM), lambda i: (0, 0, 0)),
                  pl.BlockSpec((1, D), lambda i: (0, 0))],
        out_specs=pl.BlockSpec((ts, D), lambda i: (i, 0)),
        out_shape=jax.ShapeDtypeStruct((S, D), F32),
        compiler_params=_params("arbitrary"),
        name="pool_layer",
    )(h, h, ln_w.reshape(1, D), pool_w.astype(BF16), pool_scale.reshape(1, D))


def _memkv_body(mem_ref, lnw_ref, w_ref, o_ref):
    mn = _rms(mem_ref[...], lnw_ref[...])
    o_ref[...] = _dot(mn.astype(BF16), w_ref[...].astype(BF16)).astype(o_ref.dtype)


def _mem_kv(mem, ln_w, wkv, layer, tn=1024):
    M, D = mem.shape
    N = wkv.shape[2]
    return pl.pallas_call(
        _memkv_body,
        grid=(N // tn,),
        in_specs=[pl.BlockSpec((M, D), lambda n: (0, 0)),
                  pl.BlockSpec((1, D), lambda n: (0, 0)),
                  pl.BlockSpec((None, D, tn), lambda n: (layer, 0, n))],
        out_specs=pl.BlockSpec((M, tn), lambda n: (0, n)),
        out_shape=jax.ShapeDtypeStruct((M, N), BF16),
        compiler_params=_params("arbitrary"),
        name="xattn_mem_kv",
    )(mem, ln_w.reshape(1, D), wkv)


def _load_weight_bf16(w_hbm, dst, stage, sem):
    rows = stage.shape[1]
    n_chunks = w_hbm.shape[0] // rows

    def chunk_copy(c):
        return pltpu.make_async_copy(w_hbm.at[pl.ds(c * rows, rows), :], stage.at[c % 2], sem.at[c % 2])

    chunk_copy(0).start()
    for c in range(n_chunks):
        if c + 1 < n_chunks:
            chunk_copy(c + 1).start()
        chunk_copy(c).wait()
        dst[pl.ds(c * rows, rows), :] = stage[c % 2].astype(BF16)


def _xattn_body(layer, h_ref, lnw_ref, k_ref, v_ref, wq_hbm, wo_hbm, o_ref, wq_ref, wo_ref, stage, sem):
    @pl.when(pl.program_id(0) == 0)
    def _():
        _load_weight_bf16(wq_hbm.at[layer], wq_ref, stage, sem)
        _load_weight_bf16(wo_hbm.at[layer], wo_ref, stage, sem)

    h = h_ref[...]
    hn = _rms(h, lnw_ref[...]).astype(BF16)
    q = _dot(hn, wq_ref[...])
    scale = XATTN_HEAD_DIM ** -0.5
    heads = []
    for hd in range(XATTN_HEADS):
        sl = slice(hd * XATTN_HEAD_DIM, (hd + 1) * XATTN_HEAD_DIM)
        s = _dot(q[:, sl].astype(BF16), k_ref[:, sl], NT) * scale
        s = s - jnp.max(s, axis=-1, keepdims=True)
        e = jnp.exp(s)
        p = e / jnp.sum(e, axis=-1, keepdims=True)
        heads.append(_dot(p.astype(BF16), v_ref[:, sl]))
    o = jnp.concatenate(heads, axis=-1).astype(BF16)
    o_ref[...] = h + _dot(o, wo_ref[...])


def _xattn_layer(h, ln_w, wq, kv, wo, layer, ts=256):
    S, D = h.shape
    M = kv.shape[0]
    return pl.pallas_call(
        functools.partial(_xattn_body, layer),
        grid=(S // ts,),
        in_specs=[pl.BlockSpec((ts, D), lambda i: (i, 0)),
                  pl.BlockSpec((1, D), lambda i: (0, 0)),
                  pl.BlockSpec((M, D), lambda i: (0, 0)),
                  pl.BlockSpec((M, D), lambda i: (0, 1)),
                  pl.BlockSpec(memory_space=pl.ANY),
                  pl.BlockSpec(memory_space=pl.ANY)],
        out_specs=pl.BlockSpec((ts, D), lambda i: (i, 0)),
        out_shape=jax.ShapeDtypeStruct((S, D), F32),
        scratch_shapes=[pltpu.VMEM((D, D), BF16), pltpu.VMEM((D, D), BF16),
                        pltpu.VMEM((2, WEIGHT_STAGE_ROWS, D), F32), pltpu.SemaphoreType.DMA((2,))],
        compiler_params=_params("arbitrary"),
        name="xattn_layer",
    )(h, ln_w.reshape(1, D), kv, kv, wq, wo)


def _router_body(h_ref, lnw_ref, whi_ref, wlo_ref, b_ref, r_ref, rt_ref, cnt_ref):
    i = pl.program_id(0)

    @pl.when(i == 0)
    def _():
        cnt_ref[...] = jnp.zeros_like(cnt_ref)

    hn = _rms(h_ref[...], lnw_ref[...])
    xh, xl = _split(hn)
    whi = whi_ref[...]
    lg = _dot(xh, whi) + (_dot(xh, wlo_ref[...]) + _dot(xl, whi)) + b_ref[...]
    ts = lg.shape[0]
    lane = lax.broadcasted_iota(jnp.int32, (ts, LANES), 1)
    lanef = lane.astype(F32)
    neg = jnp.float32(-jnp.inf)
    gl = jnp.where(lane < N_GROUPS, lg, neg)
    gmax = jnp.max(gl, axis=-1, keepdims=True)
    pg_top = 1.0 / jnp.sum(jnp.exp(gl - gmax), axis=-1, keepdims=True)
    g_idx = jnp.min(jnp.where(gl == gmax, lanef, float(LANES)), axis=-1, keepdims=True)
    lo_lane = g_idx * EXPERTS_PER_GROUP + N_GROUPS
    el = jnp.where(lanef >= lo_lane, jnp.where(lanef < lo_lane + EXPERTS_PER_GROUP, lg, neg), neg)
    m1 = jnp.max(el, axis=-1, keepdims=True)
    i1 = jnp.min(jnp.where(el == m1, lanef, float(LANES)), axis=-1, keepdims=True)
    el2 = jnp.where(lanef == i1, neg, el)
    m2 = jnp.max(el2, axis=-1, keepdims=True)
    i2 = jnp.min(jnp.where(el2 == m2, lanef, float(LANES)), axis=-1, keepdims=True)
    e2 = jnp.exp(m2 - m1)
    g1 = pg_top / (1.0 + e2)
    g2 = pg_top * e2 / (1.0 + e2)
    hit1 = (lanef == i1).astype(F32)
    hit2 = (lanef == i2).astype(F32)
    hits = hit1 + hit2
    earlier = (lax.broadcasted_iota(jnp.int32, (ts, ts), 1)
               < lax.broadcasted_iota(jnp.int32, (ts, ts), 0)).astype(F32).astype(BF16)
    before = _dot(earlier, hits.astype(BF16)) + cnt_ref[...]
    r1 = jnp.sum(before * hit1, axis=-1, keepdims=True)
    r2 = jnp.sum(before * hit2, axis=-1, keepdims=True)
    cnt_ref[...] += jnp.sum(hits, axis=0, keepdims=True)
    out = jnp.where(lane == 0, g1, 0.0)
    out = jnp.where(lane == 1, g2, out)
    out = jnp.where(lane == 2, i1 - N_GROUPS, out)
    out = jnp.where(lane == 3, i2 - N_GROUPS, out)
    out = jnp.where(lane == 4, r1, out)
    out = jnp.where(lane == 5, r2, out)
    r_ref[...] = out
    rt_ref[...] = out.T[:SUBLANES, :]


def _router(h, ln_w, wg, bg, we, be, ts=512):
    S, D = h.shape
    unused = LANES - N_GROUPS - N_EXPERTS
    w = jnp.concatenate([wg, we, jnp.zeros((D, unused), F32)], axis=1)
    b = jnp.concatenate([bg, be, jnp.zeros((unused,), F32)]).reshape(1, LANES)
    whi, wlo = _split(w)
    return pl.pallas_call(
        _router_body,
        grid=(S // ts,),
        in_specs=[pl.BlockSpec((ts, D), lambda i: (i, 0)),
                  pl.BlockSpec((1, D), lambda i: (0, 0)),
                  pl.BlockSpec((D, LANES), lambda i: (0, 0)),
                  pl.BlockSpec((D, LANES), lambda i: (0, 0)),
                  pl.BlockSpec((1, LANES), lambda i: (0, 0))],
        out_specs=[pl.BlockSpec((ts, LANES), lambda i: (i, 0)),
                   pl.BlockSpec((SUBLANES, ts), lambda i: (0, i)),
                   pl.BlockSpec((1, LANES), lambda i: (0, 0))],
        out_shape=[jax.ShapeDtypeStruct((S, LANES), F32), jax.ShapeDtypeStruct((SUBLANES, S), F32),
                   jax.ShapeDtypeStruct((1, LANES), F32)],
        compiler_params=_params("arbitrary"),
        name="moe_router",
    )(h, ln_w.reshape(1, D), whi, wlo, b)


def _dispatch_body(pos_ref, fill_lo_ref, fill_hi_ref, nblk_ref, h_ref, lnw_ref, xs_hbm, xn, zeros, sem, zsem):
    i = pl.program_id(0)
    n = pl.num_programs(0)
    tb = h_ref.shape[0]
    slot = i % 2

    def row_copy(r, dst_row):
        return pltpu.make_async_copy(xn.at[slot, pl.ds(r, 1), :], xs_hbm.at[pl.ds(dst_row, 1), :], sem.at[slot])

    def wait_slot(s):
        for _ in range(TOP_K):
            pltpu.make_async_copy(xn.at[s], xs_hbm.at[pl.ds(0, tb), :], sem.at[s]).wait()

    @pl.when(i >= 2)
    def _():
        wait_slot(slot)

    xn[slot] = _rms(h_ref[...], lnw_ref[...])

    n_tok = n * tb

    def issue(r, carry):
        for k in range(TOP_K):
            row_copy(r, pos_ref[k * n_tok + i * tb + r]).start(priority=k % 2)
        return carry
    lax.fori_loop(0, tb, issue, 0, unroll=ISSUE_UNROLL)

    @pl.when(i == n - 1)
    def _():
        zeros[...] = jnp.zeros_like(zeros)

        def zero_row(row):
            return pltpu.make_async_copy(zeros.at[pl.ds(0, 1), :], xs_hbm.at[pl.ds(row, 1), :], zsem)

        def zero_block(b):
            return pltpu.make_async_copy(zeros, xs_hbm.at[pl.ds(b * ROW_BLOCK, ROW_BLOCK), :], zsem)

        def per_expert(e, carry):
            lo, hi = fill_lo_ref[e], fill_hi_ref[e]
            lax.fori_loop(lo, hi, lambda row, c: (zero_row(row).start(), c)[1], 0)
            lax.fori_loop(lo, hi, lambda row, c: (zero_row(row).wait(), c)[1], 0)
            return carry
        lax.fori_loop(0, N_EXPERTS, per_expert, 0)
        n_blocks = xs_hbm.shape[0] // ROW_BLOCK
        lax.fori_loop(nblk_ref[0], n_blocks, lambda b, c: (zero_block(b).start(), c)[1], 0)
        lax.fori_loop(nblk_ref[0], n_blocks, lambda b, c: (zero_block(b).wait(), c)[1], 0)
        wait_slot(slot)

        @pl.when(n >= 2)
        def _():
            wait_slot(1 - slot)


def _dispatch(pos, fill_lo, fill_hi, n_used, h, ln_w, n_rows, tb=256):
    S, D = h.shape
    return pl.pallas_call(
        _dispatch_body,
        grid_spec=pltpu.PrefetchScalarGridSpec(
            num_scalar_prefetch=4,
            grid=(S // tb,),
            in_specs=[pl.BlockSpec((tb, D), lambda i, *_: (i, 0)),
                      pl.BlockSpec((1, D), lambda i, *_: (0, 0))],
            out_specs=pl.BlockSpec(memory_space=pl.ANY),
            scratch_shapes=[pltpu.VMEM((2, tb, D), F32), pltpu.VMEM((ROW_BLOCK, D), F32),
                            pltpu.SemaphoreType.DMA((2,)), pltpu.SemaphoreType.DMA(())]),
        out_shape=jax.ShapeDtypeStruct((n_rows, D), F32),
        compiler_params=_params("arbitrary"),
        name="moe_dispatch",
    )(pos, fill_lo, fill_hi, n_used, h, ln_w.reshape(1, D))


def _experts_body(layer, nblk_ref, be_ref, fresh_ref, wslot_ref, next_e_ref, x_ref, wgu_hbm, wd_hbm, o_ref,
                  wgu_f, wd_f, wgu_b, wd_b, sem):
    i = pl.program_id(0)
    n_used = nblk_ref[0]

    def weight_copies(e, slot):
        return (pltpu.make_async_copy(wgu_hbm.at[layer, e], wgu_f.at[slot], sem.at[0, slot]),
                pltpu.make_async_copy(wd_hbm.at[layer, e], wd_f.at[slot], sem.at[1, slot]))

    @pl.when(i == 0)
    def _():
        for cp in weight_copies(be_ref[0], 0):
            cp.start()

    @pl.when(jnp.logical_and(i < n_used, fresh_ref[i] == 1))
    def _():
        slot = wslot_ref[i]
        for cp in weight_copies(be_ref[i], slot):
            cp.wait()
        wgu_b[...] = wgu_f[slot].astype(BF16)
        wd_b[...] = wd_f[slot].astype(BF16)

        @pl.when(next_e_ref[i] >= 0)
        def _():
            for cp in weight_copies(next_e_ref[i], 1 - slot):
                cp.start()

    @pl.when(i < n_used)
    def _():
        gu = _dot(x_ref[...].astype(BF16), wgu_b[...])
        gg = gu[:, :D_EXPERT]
        uu = gu[:, D_EXPERT:]
        act = (gg * (1.0 / (1.0 + jnp.exp(-gg))) * uu).astype(BF16)
        o_ref[...] = _dot(act, wd_b[...])

    @pl.when(i >= n_used)
    def _():
        o_ref[...] = jnp.zeros_like(o_ref)


def _experts(n_used, block_e, xs, w_gu, w_down, layer):
    n_rows, D = xs.shape
    n_blocks = n_rows // ROW_BLOCK
    F2 = w_gu.shape[3]
    idx = jnp.arange(n_blocks, dtype=jnp.int32)
    used = idx < n_used[0]
    fresh = jnp.logical_and(used, jnp.concatenate([jnp.ones((1,), bool), block_e[1:] != block_e[:-1]]))
    run = jnp.cumsum(fresh.astype(jnp.int32)) - 1
    run_start = jnp.where(fresh, idx, n_blocks)
    next_start = lax.cummin(jnp.concatenate([run_start[1:], jnp.full((1,), n_blocks, jnp.int32)]), reverse=True)
    next_e = jnp.where(next_start < n_blocks, block_e[jnp.minimum(next_start, n_blocks - 1)], -1)
    return pl.pallas_call(
        functools.partial(_experts_body, layer),
        grid_spec=pltpu.PrefetchScalarGridSpec(
            num_scalar_prefetch=5,
            grid=(n_blocks,),
            in_specs=[pl.BlockSpec((ROW_BLOCK, D), lambda i, nb, *_: (jnp.minimum(i, nb[0] - 1), 0)),
                      pl.BlockSpec(memory_space=pl.ANY),
                      pl.BlockSpec(memory_space=pl.ANY)],
            out_specs=pl.BlockSpec((ROW_BLOCK, D), lambda i, *_: (i, 0)),
            scratch_shapes=[pltpu.VMEM((2, D, F2), F32), pltpu.VMEM((2, F2 // 2, D), F32),
                            pltpu.VMEM((D, F2), BF16), pltpu.VMEM((F2 // 2, D), BF16),
                            pltpu.SemaphoreType.DMA((2, 2))]),
        out_shape=jax.ShapeDtypeStruct((n_rows, D), F32),
        compiler_params=_params("arbitrary"),
        name="moe_experts",
    )(n_used, block_e, fresh.astype(jnp.int32), (run % 2).astype(jnp.int32), next_e.astype(jnp.int32),
      xs, w_gu, w_down)


def _start_pair_gather(pos_ref, first_tok, n_tok, yb_hbm, ybuf, slot, sem, tb):
    D = yb_hbm.shape[1]

    def issue(r, carry):
        for k in range(TOP_K):
            pltpu.make_async_copy(yb_hbm.at[pl.ds(pos_ref[k * n_tok + first_tok + r], 1), :],
                                  ybuf.at[slot, pl.ds(r, 1), pl.ds(k * D, D)], sem.at[slot]).start(priority=k % 2)
        return carry
    lax.fori_loop(0, tb, issue, 0, unroll=ISSUE_UNROLL)


def _wait_pair_gather(yb_hbm, ybuf, slot, sem, tb):
    D = yb_hbm.shape[1]
    for k in range(TOP_K):
        pltpu.make_async_copy(yb_hbm.at[pl.ds(0, tb), :], ybuf.at[slot, :, pl.ds(k * D, D)], sem.at[slot]).wait()


def _combine_body(final_norm, pos_ref, yb_hbm, h_ref, r_ref, lnw_ref, o_ref, ybuf, sem):
    i = pl.program_id(0)
    n = pl.num_programs(0)
    tb, D = h_ref.shape
    slot = i % 2

    @pl.when(i == 0)
    def _():
        _start_pair_gather(pos_ref, 0, n * tb, yb_hbm, ybuf, 0, sem, tb)

    @pl.when(i + 1 < n)
    def _():
        _start_pair_gather(pos_ref, (i + 1) * tb, n * tb, yb_hbm, ybuf, 1 - slot, sem, tb)

    _wait_pair_gather(yb_hbm, ybuf, slot, sem, tb)
    out = h_ref[...]
    for k in range(TOP_K):
        out = out + ybuf[slot, :, k * D:(k + 1) * D] * r_ref[:, k:k + 1]
    if final_norm:
        out = _rms(out, lnw_ref[...])
    o_ref[...] = out


def _combine(pos, yb, h, routed, ln_out_w, final_norm, tb=128):
    S, D = h.shape
    return pl.pallas_call(
        functools.partial(_combine_body, final_norm),
        grid_spec=pltpu.PrefetchScalarGridSpec(
            num_scalar_prefetch=1,
            grid=(S // tb,),
            in_specs=[pl.BlockSpec(memory_space=pl.ANY),
                      pl.BlockSpec((tb, D), lambda i, pos: (i, 0)),
                      pl.BlockSpec((tb, LANES), lambda i, pos: (i, 0)),
                      pl.BlockSpec((1, D), lambda i, pos: (0, 0))],
            out_specs=pl.BlockSpec((tb, D), lambda i, pos: (i, 0)),
            scratch_shapes=[pltpu.VMEM((2, tb, TOP_K * D), F32), pltpu.SemaphoreType.DMA((2,))]),
        out_shape=jax.ShapeDtypeStruct((S, D), F32),
        compiler_params=_params("arbitrary"),
        name="moe_combine",
    )(pos, yb, h, routed, ln_out_w.reshape(1, D))


def _moe_layer(h, ln_w, wg, bg, we, be, w_gu, w_down, layer, ln_out_w, final_norm):
    T, D = h.shape
    routed, routed_t, counts = _router(h, ln_w, wg, bg, we, be)
    A = T * TOP_K
    n_blocks = -(-A // ROW_BLOCK) + N_EXPERTS
    n_rows = n_blocks * ROW_BLOCK
    counts = counts[0, N_GROUPS:N_GROUPS + N_EXPERTS].astype(jnp.int32)
    padded = (counts + ROW_BLOCK - 1) // ROW_BLOCK * ROW_BLOCK
    pad_end = jnp.cumsum(padded)
    pad_start = pad_end - padded
    expert_idx = routed_t[2:2 + TOP_K].astype(jnp.int32)
    slot = routed_t[4:4 + TOP_K].astype(jnp.int32)
    is_e = expert_idx[..., None] == jnp.arange(N_EXPERTS, dtype=jnp.int32)
    pos = (jnp.sum(jnp.where(is_e, pad_start, 0), axis=-1) + slot).reshape(A)
    block_start = jnp.arange(n_blocks, dtype=jnp.int32) * ROW_BLOCK
    block_e = jnp.minimum(jnp.sum(pad_end[None, :] <= block_start[:, None], axis=1), N_EXPERTS - 1).astype(jnp.int32)
    n_used = (pad_end[-1] // ROW_BLOCK).astype(jnp.int32).reshape(1)
    xs = _dispatch(pos, pad_start + counts, pad_end, n_used, h, ln_w, n_rows)
    yb = _experts(n_used, block_e, xs, w_gu, w_down, layer)
    return _combine(pos, yb, h, routed, ln_out_w, final_norm)


def _sigmoid(x):
    return 1.0 / (1.0 + jnp.exp(-x))


def _rwkv_prep_body(h_ref, halo_ref, lnw_ref, mu_ref, w0_ref, w1_ref, w2_ref, a0_ref, a1_ref, a2_ref, g1_ref,
                    g2_ref, xs_ref, lw_ref, a_ref, g_ref):
    i = pl.program_id(0)
    lnw = lnw_ref[...]
    hn = _rms(h_ref[...], lnw)
    ts = hn.shape[0]
    last = _rms(halo_ref[...], lnw)[SUBLANES - 1:SUBLANES, :] * (i > 0).astype(F32)
    row = lax.broadcasted_iota(jnp.int32, (ts, 1), 0)
    prev = jnp.where(row == 0, last, pltpu.roll(hn, 1, axis=0))
    xx = prev - hn
    mix = lambda n: (hn + xx * mu_ref[n:n + 1, :]).astype(BF16)
    n_out = xs_ref.shape[0]
    for n in range(n_out):
        xs_ref[n] = mix(n)
    z = w0_ref[...] + _dot(jnp.tanh(_dot(mix(n_out), w1_ref[...])).astype(BF16), w2_ref[...])
    u = -z
    softplus = jnp.maximum(u, 0.0) + jnp.log1p(jnp.exp(-jnp.abs(u)))
    lw_ref[...] = -jnp.exp(-softplus - 0.5)
    a_ref[...] = _sigmoid(a0_ref[...] + _dot(_dot(mix(n_out + 1), a1_ref[...]).astype(BF16), a2_ref[...]))
    g_ref[...] = _dot(_sigmoid(_dot(mix(n_out + 2), g1_ref[...])).astype(BF16), g2_ref[...])


def _rwkv_prep(h, ln_w, mu, w0, w1, w2, a0, a1, a2, g1, g2, ts=256):
    S, D = h.shape
    n_mix = mu.shape[0]
    n_out = n_mix - 3
    halo_blocks = ts // SUBLANES

    def pad_lora(wa, wb):
        r = wa.shape[1]
        rp = -(-r // LANES) * LANES
        return (jnp.pad(wa, ((0, 0), (0, rp - r))).astype(BF16), jnp.pad(wb, ((0, rp - r), (0, 0))).astype(BF16))

    w1p, w2p = pad_lora(w1, w2)
    a1p, a2p = pad_lora(a1, a2)
    g1p, g2p = pad_lora(g1, g2)
    full = lambda arr: pl.BlockSpec(arr.shape, lambda i: (0,) * arr.ndim)
    vec = pl.BlockSpec((1, D), lambda i: (0, 0))
    row = pl.BlockSpec((ts, D), lambda i: (i, 0))
    return pl.pallas_call(
        _rwkv_prep_body,
        grid=(S // ts,),
        in_specs=[row,
                  pl.BlockSpec((SUBLANES, D), lambda i: (jnp.maximum(i * halo_blocks - 1, 0), 0)),
                  vec, full(mu), vec, full(w1p), full(w2p), vec, full(a1p), full(a2p), full(g1p), full(g2p)],
        out_specs=[pl.BlockSpec((n_out, ts, D), lambda i: (0, i, 0)), row, row, row],
        out_shape=[jax.ShapeDtypeStruct((n_out, S, D), BF16)] + [jax.ShapeDtypeStruct((S, D), F32)] * 3,
        compiler_params=_params("arbitrary"),
        name="rwkv_prep",
    )(h, h, ln_w.reshape(1, D), mu, w0.reshape(1, D), w1p, w2p, a0.reshape(1, D), a1p, a2p, g1p, g2p)


def _scan_chunk(r, k, v, lw, a, g, kkw, kaw, rkw, lnw, lnb, state, c):
    masks, seg_f, seg_b, tri, strict, incl, eye = c
    bf = lambda x: x.astype(BF16)
    stack = lambda xb: jnp.concatenate([xb * m for m in masks], axis=0)
    rows = lambda *xs: jnp.concatenate(xs, axis=0)
    cols = lambda *xs: jnp.concatenate(xs, axis=1)

    kk = k * kkw
    k2 = k * (1.0 + (a - 1.0) * kaw)
    sums = _segsum(rows(kk * kk, r * k2 * rkw), seg_b)
    cw = _cumsum_rows(tri, lw)
    yield
    kk = kk / jnp.maximum(jnp.sqrt(sums[:CHUNK]), 1e-12)
    bonus = sums[CHUNK:] * v
    bv = kk * a
    cl = cw[CHUNK - 1:CHUNK, :]
    at = bf(-kk * jnp.exp(cw - lw))
    rt = bf(r * jnp.exp(cw))
    dinv = jnp.exp(-cw)
    drem = jnp.exp(cl - cw)
    vb = bf(v)
    vs = stack(vb)
    gram = _dot(rows(at, rt), rows(stack(bf(bv * dinv)), stack(bf(k2 * dinv))), NT)
    bk_t = bf(rows(bv * drem, k2 * drem).T)
    decay_rows = jnp.broadcast_to(jnp.exp(cl), (GROUP, GROUP)).T
    yield
    a_ab = jnp.where(strict, gram[:CHUNK, :GROUP], 0.0)
    a_ak = bf(jnp.where(strict, gram[:CHUNK, GROUP:], 0.0))
    a_rb = bf(jnp.where(incl, gram[CHUNK:, :GROUP], 0.0))
    a_rk = bf(jnp.where(incl, gram[CHUNK:, GROUP:], 0.0))
    inv = eye + a_ab
    pw = _dot(bf(a_ab), stack(bf(a_ab)))
    xy = _dot(rows(cols(at, a_ak), cols(rt, a_rk)), rows(bf(state), vs))
    yield
    span = 2
    while span < CHUNK // 2:
        pwb = bf(pw)
        both = _dot(rows(bf(inv), pwb), stack(pwb))
        yield
        inv = inv + both[:CHUNK]
        pw = both[CHUNK:]
        span *= 2
    inv = inv + _dot(bf(inv), stack(bf(pw)))
    yield
    ub = bf(_dot(bf(inv), stack(bf(xy[:CHUNK]))))
    yield
    y = xy[CHUNK:] + _dot(a_rb, stack(ub))
    new_state = state * decay_rows + seg_f * _dot(bk_t, rows(ub, vb))
    yield
    inv_n = 1.0 / RWKV_HEAD_DIM
    mean = _segsum(y, seg_b) * inv_n
    yield
    yc = y - mean
    var = _segsum(yc * yc, seg_b) * inv_n
    yield
    yn = yc * lax.rsqrt(var + GN_EPS) * lnw + lnb
    return ((yn + bonus) * g), new_state


def _run_interleaved(gens):
    results = [None] * len(gens)
    live = list(range(len(gens)))
    while live:
        for i in list(live):
            try:
                next(gens[i])
            except StopIteration as done:
                results[i] = done.value
                live.remove(i)
    return results


def _segsum(x, seg_b):
    hi = x.astype(BF16)
    mid = (x - hi.astype(F32)).astype(BF16)
    return _dot(jnp.concatenate([hi, mid], axis=1), jnp.concatenate([seg_b, seg_b], axis=0))


def _cumsum_rows(tri, x):
    hi = x.astype(BF16)
    r1 = x - hi.astype(F32)
    mid = r1.astype(BF16)
    lo = (r1 - mid.astype(F32)).astype(BF16)
    return _dot(jnp.concatenate([tri, tri, tri], axis=1), jnp.concatenate([hi, mid, lo], axis=0))


def _scan_consts():
    shift = RWKV_HEAD_DIM.bit_length() - 1
    lane_head = lax.broadcasted_iota(jnp.int32, (1, GROUP), 1) >> shift
    masks = [(lane_head == h).astype(F32).astype(BF16) for h in range(GROUP_HEADS)]
    ri = lax.broadcasted_iota(jnp.int32, (GROUP, GROUP), 0) >> shift
    ci = lax.broadcasted_iota(jnp.int32, (GROUP, GROUP), 1) >> shift
    seg_f = (ri == ci).astype(F32)
    tri = (lax.broadcasted_iota(jnp.int32, (CHUNK, CHUNK), 1)
           <= lax.broadcasted_iota(jnp.int32, (CHUNK, CHUNK), 0)).astype(F32).astype(BF16)
    t = lax.broadcasted_iota(jnp.int32, (CHUNK, GROUP), 0)
    s = lax.broadcasted_iota(jnp.int32, (CHUNK, GROUP), 1) & (RWKV_HEAD_DIM - 1)
    return (masks, seg_f, seg_f.astype(BF16), tri, s < t, s <= t, (s == t).astype(F32))


def _scan_body(r_ref, k_ref, v_ref, lw_ref, a_ref, g_ref, kkw_ref, kaw_ref, rkw_ref, lnw_ref, lnb_ref,
               o_ref, state_ref):
    @pl.when(pl.program_id(1) == 0)
    def _():
        state_ref[...] = jnp.zeros_like(state_ref)

    n_groups = o_ref.shape[1] // GROUP

    def chunk_step(j, carry):
        consts = _scan_consts()
        rows = pl.ds(pl.multiple_of(j * CHUNK, CHUNK), CHUNK)
        lanes = [slice(p * GROUP, (p + 1) * GROUP) for p in range(n_groups)]
        args = [(r_ref[rows, ln], k_ref[rows, ln], v_ref[rows, ln], lw_ref[rows, ln], a_ref[rows, ln],
                 g_ref[rows, ln], kkw_ref[:, ln], kaw_ref[:, ln], rkw_ref[:, ln], lnw_ref[:, ln],
                 lnb_ref[:, ln], state_ref[p]) for p, ln in enumerate(lanes)]
        results = _run_interleaved([_scan_chunk(*a, consts) for a in args])
        for p, (ln, (out, new_state)) in enumerate(zip(lanes, results)):
            state_ref[p] = new_state
            o_ref[rows, ln] = out.astype(o_ref.dtype)
        return carry

    lax.fori_loop(0, o_ref.shape[0] // CHUNK, chunk_step, 0)


def _rwkv_scan(rkv, lw, a, g, k_k, k_a, r_k, ln_w, ln_b):
    _, S, D = rkv.shape
    tr, tl = min(SCAN_ROWS, S), SCAN_LANES
    blk = lambda n: pl.BlockSpec((None, tr, tl), lambda p, i: (n, i, p))
    row = pl.BlockSpec((tr, tl), lambda p, i: (i, p))
    vec = pl.BlockSpec((1, tl), lambda p, i: (0, p))
    as_row = lambda w: w.reshape(1, D)
    return pl.pallas_call(
        _scan_body,
        grid=(D // tl, S // tr),
        in_specs=[blk(0), blk(1), blk(2), row, row, row, vec, vec, vec, vec, vec],
        out_specs=row,
        out_shape=jax.ShapeDtypeStruct((S, D), BF16),
        scratch_shapes=[pltpu.VMEM((tl // GROUP, GROUP, GROUP), F32)],
        compiler_params=_params("arbitrary", "arbitrary"),
        name="rwkv_scan",
    )(rkv, rkv, rkv, lw, a, g, as_row(k_k), as_row(k_a), as_row(r_k), as_row(ln_w), as_row(ln_b))


def _rwkv_layer(h, ln_w, mu, w_rkv, w_o, j, w0, w1, w2, a0, a1, a2, g1, g2, k_k, k_a, r_k, ln_gn_w, ln_gn_b):
    xs, lw, a, g = _rwkv_prep(h, ln_w, mu, w0, w1, w2, a0, a1, a2, g1, g2)
    rkv = _rkv_proj(xs, w_rkv, j)
    z = _rwkv_scan(rkv, lw, a, g, k_k, k_a, r_k, ln_gn_w, ln_gn_b)
    return _out_proj(z, w_o, j, h)


def kernel(x, mem, ln_mix_w, ln_xattn_w, ln_mem_w, xattn_wq, xattn_wkv, xattn_wo, ln_ffn_w, router_group_w, router_group_b, router_expert_w, router_expert_b, moe_w_gate_up, moe_w_down, pool_w, pool_scale, rwkv_mu, rwkv_w_rkv, rwkv_w_o, rwkv_w0, rwkv_w1, rwkv_w2, rwkv_a0, rwkv_a1, rwkv_a2, rwkv_g1, rwkv_g2, rwkv_k_k, rwkv_k_a, rwkv_r_k, rwkv_ln_w, rwkv_ln_b, ln_out_w):
    B, S, D = x.shape
    depth = ln_mix_w.shape[0]
    outs = []
    for b in range(B):
        h = x[b]
        for i in range(depth):
            j = i // 2
            if i % 2 == 0:
                h = _pool_layer(h, ln_mix_w[i], pool_w[j], pool_scale[j])
            else:
                h = _rwkv_layer(h, ln_mix_w[i], rwkv_mu[j], rwkv_w_rkv, rwkv_w_o, j, rwkv_w0[j], rwkv_w1[j],
                                rwkv_w2[j], rwkv_a0[j], rwkv_a1[j], rwkv_a2[j], rwkv_g1[j], rwkv_g2[j],
                                rwkv_k_k[j], rwkv_k_a[j], rwkv_r_k[j].reshape(D), rwkv_ln_w[j], rwkv_ln_b[j])
            kv = _mem_kv(mem[b], ln_mem_w[i], xattn_wkv, i)
            h = _xattn_layer(h, ln_xattn_w[i], xattn_wq, kv, xattn_wo, i)
            h = _moe_layer(h, ln_ffn_w[i], router_group_w[i], router_group_b[i], router_expert_w[i],
                           router_expert_b[i], moe_w_gate_up, moe_w_down, i, ln_out_w,
                           final_norm=(i == depth - 1))
        outs.append(h)
    return outs[0][None] if B == 1 else jnp.stack(outs, axis=0)
```

```python
import functools

import jax
import jax.numpy as jnp
from jax import lax
from jax.experimental import pallas as pl
from jax.experimental.pallas import tpu as pltpu

F32 = jnp.float32
BF16 = jnp.bfloat16

D_MODEL = 2048
POOL_WINDOWS = (2, 4, 8, 16)
POOL_GROUP_DIM = D_MODEL // len(POOL_WINDOWS)
MAX_WINDOW = max(POOL_WINDOWS)
RWKV_HEAD_DIM = 64
GN_EPS = 64e-5
XATTN_HEADS = 4
XATTN_HEAD_DIM = D_MODEL // XATTN_HEADS
N_GROUPS = 8
EXPERTS_PER_GROUP = 8
N_EXPERTS = N_GROUPS * EXPERTS_PER_GROUP
TOP_K = 2
D_EXPERT = D_MODEL // 8
ROW_BLOCK = 128
RMS_EPS = 1e-6

LANES = 128
SUBLANES = 8
VMEM_LIMIT = 56 * 1024 * 1024

CHUNK = 64
GROUP_HEADS = 4
GROUP = GROUP_HEADS * RWKV_HEAD_DIM
ISSUE_UNROLL = 8
WEIGHT_STAGE_ROWS = 256
SCAN_ROWS = 256
SCAN_LANES = 2048

NN = (((1,), (0,)), ((), ()))
NT = (((1,), (1,)), ((), ()))


def _params(*sem):
    return pltpu.CompilerParams(dimension_semantics=sem, vmem_limit_bytes=VMEM_LIMIT)


def _rms(x, w):
    return x * lax.rsqrt(jnp.mean(x * x, axis=-1, keepdims=True) + RMS_EPS) * w


def _dot(a, b, dims=NN):
    return lax.dot_general(a, b, dims, preferred_element_type=F32)


def _split(x):
    hi = x.astype(BF16)
    return hi, (x - hi.astype(F32)).astype(BF16)


def _rkv_proj_body(x_ref, w_ref, o_ref, wb_ref):
    @pl.when(pl.program_id(2) == 0)
    def _():
        wb_ref[...] = w_ref[...].astype(BF16)

    o_ref[...] = _dot(x_ref[...], wb_ref[...])


def _rkv_proj(xs, w_rkv, j, tm=512, tn=1024):
    _, M, K = xs.shape
    B, N = w_rkv.shape[1], w_rkv.shape[3]
    return pl.pallas_call(
        _rkv_proj_body,
        grid=(B, N // tn, M // tm),
        in_specs=[pl.BlockSpec((None, tm, K), lambda b, n, i: (b, i, 0)),
                  pl.BlockSpec((None, None, K, tn), lambda b, n, i: (j, b, 0, n))],
        out_specs=pl.BlockSpec((None, tm, tn), lambda b, n, i: (b, i, n)),
        out_shape=jax.ShapeDtypeStruct((B, M, N), F32),
        scratch_shapes=[pltpu.VMEM((K, tn), BF16)],
        compiler_params=_params("arbitrary", "arbitrary", "arbitrary"),
        name="rwkv_rkv_proj",
    )(xs, w_rkv)


def _out_proj_body(x_ref, w_ref, r_ref, o_ref, wb_ref):
    @pl.when(pl.program_id(1) == 0)
    def _():
        wb_ref[...] = w_ref[...].astype(BF16)

    o_ref[...] = r_ref[...] + _dot(x_ref[...], wb_ref[...])


def _out_proj(x, w_o, j, res, tm=512, tn=1024):
    M, K = x.shape
    N = w_o.shape[2]
    return pl.pallas_call(
        _out_proj_body,
        grid=(N // tn, M // tm),
        in_specs=[pl.BlockSpec((tm, K), lambda n, i: (i, 0)),
                  pl.BlockSpec((None, K, tn), lambda n, i: (j, 0, n)),
                  pl.BlockSpec((tm, tn), lambda n, i: (i, n))],
        out_specs=pl.BlockSpec((tm, tn), lambda n, i: (i, n)),
        out_shape=jax.ShapeDtypeStruct((M, N), F32),
        scratch_shapes=[pltpu.VMEM((K, tn), BF16)],
        compiler_params=_params("arbitrary", "arbitrary"),
        name="rwkv_out_proj",
    )(x, w_o, res)


def _pool_body(h_ref, halo_ref, lnw_ref, pw_ref, ps_ref, o_ref):
    i = pl.program_id(0)
    ts = h_ref.shape[0]
    h = h_ref[...]
    lnw = lnw_ref[...]
    x = _rms(h, lnw)
    xh = _rms(halo_ref[...], lnw) * (i > 0).astype(F32)
    xe = jnp.concatenate([xh, x], axis=0)
    t = i * ts + lax.broadcasted_iota(jnp.int32, (ts, 1), 0)
    outs = []
    for g, win in enumerate(POOL_WINDOWS):
        sl = slice(g * POOL_GROUP_DIM, (g + 1) * POOL_GROUP_DIM)
        acc = xe[:, sl]
        span = 1
        while span < win:
            acc = acc + pltpu.roll(acc, span, axis=0)
            span *= 2
        cnt = jnp.minimum(t + 1, win).astype(F32)
        pooled = acc[MAX_WINDOW:, :] / cnt - x[:, sl]
        outs.append(_dot(pooled.astype(BF16), pw_ref[g]))
    y = jnp.concatenate(outs, axis=-1)
    o_ref[...] = h + y * ps_ref[...]


def _pool_layer(h, ln_w, pool_w, pool_scale, ts=512):
    S, D = h.shape
    G = pool_w.shape[0]
    halo_blocks = ts // MAX_WINDOW
    return pl.pallas_call(
        _pool_body,
        grid=(S // ts,),
        in_specs=[pl.BlockSpec((ts, D), lambda i: (i, 0)),
                  pl.BlockSpec((MAX_WINDOW, D), lambda i: (jnp.maximum(i * halo_blocks - 1, 0), 0)),
                  pl.BlockSpec((1, D), lambda i: (0, 0)),
                  pl.BlockSpec((G, POOL_GROUP_DIM, POOL_GROUP_DIM), lambda i: (0, 0, 0)),
                  pl.BlockSpec((1, D), lambda i: (0, 0))],
        out_specs=pl.BlockSpec((ts, D), lambda i: (i, 0)),
        out_shape=jax.ShapeDtypeStruct((S, D), F32),
        compiler_params=_params("arbitrary"),
        name="pool_layer",
    )(h, h, ln_w.reshape(1, D), pool_w.astype(BF16), pool_scale.reshape(1, D))


def _memkv_body(mem_ref, lnw_ref, w_ref, o_ref):
    mn = _rms(mem_ref[...], lnw_ref[...])
    o_ref[...] = _dot(mn.astype(BF16), w_ref[...].astype(BF16)).astype(o_ref.dtype)


def _mem_kv(mem, ln_w, wkv, layer, tn=1024):
    M, D = mem.shape
    N = wkv.shape[2]
    return pl.pallas_call(
        _memkv_body,
        grid=(N // tn,),
        in_specs=[pl.BlockSpec((M, D), lambda n: (0, 0)),
                  pl.BlockSpec((1, D), lambda n: (0, 0)),
                  pl.BlockSpec((None, D, tn), lambda n: (layer, 0, n))],
        out_specs=pl.BlockSpec((M, tn), lambda n: (0, n)),
        out_shape=jax.ShapeDtypeStruct((M, N), BF16),
        compiler_params=_params("arbitrary"),
        name="xattn_mem_kv",
    )(mem, ln_w.reshape(1, D), wkv)


def _load_weight_bf16(w_hbm, dst, stage, sem):
    rows = stage.shape[1]
    n_chunks = w_hbm.shape[0] // rows

    def chunk_copy(c):
        return pltpu.make_async_copy(w_hbm.at[pl.ds(c * rows, rows), :], stage.at[c % 2], sem.at[c % 2])

    chunk_copy(0).start()
    for c in range(n_chunks):
        if c + 1 < n_chunks:
            chunk_copy(c + 1).start()
        chunk_copy(c).wait()
        dst[pl.ds(c * rows, rows), :] = stage[c % 2].astype(BF16)


def _xattn_body(layer, h_ref, lnw_ref, k_ref, v_ref, wq_hbm, wo_hbm, o_ref, wq_ref, wo_ref, stage, sem):
    @pl.when(pl.program_id(0) == 0)
    def _():
        _load_weight_bf16(wq_hbm.at[layer], wq_ref, stage, sem)
        _load_weight_bf16(wo_hbm.at[layer], wo_ref, stage, sem)

    h = h_ref[...]
    hn = _rms(h, lnw_ref[...]).astype(BF16)
    q = _dot(hn, wq_ref[...])
    scale = XATTN_HEAD_DIM ** -0.5
    heads = []
    for hd in range(XATTN_HEADS):
        sl = slice(hd * XATTN_HEAD_DIM, (hd + 1) * XATTN_HEAD_DIM)
        s = _dot(q[:, sl].astype(BF16), k_ref[:, sl], NT) * scale
        s = s - jnp.max(s, axis=-1, keepdims=True)
        e = jnp.exp(s)
        p = e / jnp.sum(e, axis=-1, keepdims=True)
        heads.append(_dot(p.astype(BF16), v_ref[:, sl]))
    o = jnp.concatenate(heads, axis=-1).astype(BF16)
    o_ref[...] = h + _dot(o, wo_ref[...])


def _xattn_layer(h, ln_w, wq, kv, wo, layer, ts=256):
    S, D = h.shape
    M = kv.shape[0]
    return pl.pallas_call(
        functools.partial(_xattn_body, layer),
        grid=(S // ts,),
        in_specs=[pl.BlockSpec((ts, D), lambda i: (i, 0)),
                  pl.BlockSpec((1, D), lambda i: (0, 0)),
                  pl.BlockSpec((M, D), lambda i: (0, 0)),
                  pl.BlockSpec((M, D), lambda i: (0, 1)),
                  pl.BlockSpec(memory_space=pl.ANY),
                  pl.BlockSpec(memory_space=pl.ANY)],
        out_specs=pl.BlockSpec((ts, D), lambda i: (i, 0)),
        out_shape=jax.ShapeDtypeStruct((S, D), F32),
        scratch_shapes=[pltpu.VMEM((D, D), BF16), pltpu.VMEM((D, D), BF16),
                        pltpu.VMEM((2, WEIGHT_STAGE_ROWS, D), F32), pltpu.SemaphoreType.DMA((2,))],
        compiler_params=_params("arbitrary"),
        name="xattn_layer",
    )(h, ln_w.reshape(1, D), kv, kv, wq, wo)


def _router_body(h_ref, lnw_ref, whi_ref, wlo_ref, b_ref, r_ref, rt_ref, cnt_ref):
    i = pl.program_id(0)

    @pl.when(i == 0)
    def _():
        cnt_ref[...] = jnp.zeros_like(cnt_ref)

    hn = _rms(h_ref[...], lnw_ref[...])
    xh, xl = _split(hn)
    whi = whi_ref[...]
    lg = _dot(xh, whi) + (_dot(xh, wlo_ref[...]) + _dot(xl, whi)) + b_ref[...]
    ts = lg.shape[0]
    lane = lax.broadcasted_iota(jnp.int32, (ts, LANES), 1)
    lanef = lane.astype(F32)
    neg = jnp.float32(-jnp.inf)
    gl = jnp.where(lane < N_GROUPS, lg, neg)
    gmax = jnp.max(gl, axis=-1, keepdims=True)
    pg_top = 1.0 / jnp.sum(jnp.exp(gl - gmax), axis=-1, keepdims=True)
    g_idx = jnp.min(jnp.where(gl == gmax, lanef, float(LANES)), axis=-1, keepdims=True)
    lo_lane = g_idx * EXPERTS_PER_GROUP + N_GROUPS
    el = jnp.where(lanef >= lo_lane, jnp.where(lanef < lo_lane + EXPERTS_PER_GROUP, lg, neg), neg)
    m1 = jnp.max(el, axis=-1, keepdims=True)
    i1 = jnp.min(jnp.where(el == m1, lanef, float(LANES)), axis=-1, keepdims=True)
    el2 = jnp.where(lanef == i1, neg, el)
    m2 = jnp.max(el2, axis=-1, keepdims=True)
    i2 = jnp.min(jnp.where(el2 == m2, lanef, float(LANES)), axis=-1, keepdims=True)
    e2 = jnp.exp(m2 - m1)
    g1 = pg_top / (1.0 + e2)
    g2 = pg_top * e2 / (1.0 + e2)
    hit1 = (lanef == i1).astype(F32)
    hit2 = (lanef == i2).astype(F32)
    hits = hit1 + hit2
    earlier = (lax.broadcasted_iota(jnp.int32, (ts, ts), 1)
               < lax.broadcasted_iota(jnp.int32, (ts, ts), 0)).astype(F32).astype(BF16)
    before = _dot(earlier, hits.astype(BF16)) + cnt_ref[...]
    r1 = jnp.sum(before * hit1, axis=-1, keepdims=True)
    r2 = jnp.sum(before * hit2, axis=-1, keepdims=True)
    cnt_ref[...] += jnp.sum(hits, axis=0, keepdims=True)
    out = jnp.where(lane == 0, g1, 0.0)
    out = jnp.where(lane == 1, g2, out)
    out = jnp.where(lane == 2, i1 - N_GROUPS, out)
    out = jnp.where(lane == 3, i2 - N_GROUPS, out)
    out = jnp.where(lane == 4, r1, out)
    out = jnp.where(lane == 5, r2, out)
    r_ref[...] = out
    rt_ref[...] = out.T[:SUBLANES, :]


def _router(h, ln_w, wg, bg, we, be, ts=512):
    S, D = h.shape
    unused = LANES - N_GROUPS - N_EXPERTS
    w = jnp.concatenate([wg, we, jnp.zeros((D, unused), F32)], axis=1)
    b = jnp.concatenate([bg, be, jnp.zeros((unused,), F32)]).reshape(1, LANES)
    whi, wlo = _split(w)
    return pl.pallas_call(
        _router_body,
        grid=(S // ts,),
        in_specs=[pl.BlockSpec((ts, D), lambda i: (i, 0)),
                  pl.BlockSpec((1, D), lambda i: (0, 0)),
                  pl.BlockSpec((D, LANES), lambda i: (0, 0)),
                  pl.BlockSpec((D, LANES), lambda i: (0, 0)),
                  pl.BlockSpec((1, LANES), lambda i: (0, 0))],
        out_specs=[pl.BlockSpec((ts, LANES), lambda i: (i, 0)),
                   pl.BlockSpec((SUBLANES, ts), lambda i: (0, i)),
                   pl.BlockSpec((1, LANES), lambda i: (0, 0))],
        out_shape=[jax.ShapeDtypeStruct((S, LANES), F32), jax.ShapeDtypeStruct((SUBLANES, S), F32),
                   jax.ShapeDtypeStruct((1, LANES), F32)],
        compiler_params=_params("arbitrary"),
        name="moe_router",
    )(h, ln_w.reshape(1, D), whi, wlo, b)


def _row_tokens_body(pos_ref, tok_ref):
    n_tok = pos_ref.shape[0] // TOP_K

    def clear(j, carry):
        tok_ref[j] = 0
        return carry
    lax.fori_loop(0, tok_ref.shape[0], clear, 0, unroll=ISSUE_UNROLL)

    def put(t, carry):
        for k in range(TOP_K):
            tok_ref[pos_ref[k * n_tok + t]] = t
        return carry
    lax.fori_loop(0, n_tok, put, 0, unroll=ISSUE_UNROLL)


def _row_tokens(pos, n_rows):
    return pl.pallas_call(
        _row_tokens_body,
        grid_spec=pltpu.PrefetchScalarGridSpec(
            num_scalar_prefetch=1,
            grid=(1,),
            in_specs=[],
            out_specs=pl.BlockSpec(memory_space=pltpu.SMEM)),
        out_shape=jax.ShapeDtypeStruct((n_rows,), jnp.int32),
        compiler_params=_params("arbitrary"),
        name="moe_row_tokens",
    )(pos)


def _experts_body(layer, nblk_ref, be_ref, fresh_ref, wslot_ref, next_e_ref, tok_ref, h_hbm, lnw_ref, wgu_hbm,
                  wd_hbm, o_ref, xbuf, wgu_f, wd_f, wgu_b, wd_b, xsem, wsem):
    i = pl.program_id(0)
    n_used = nblk_ref[0]
    xslot = i % 2

    def start_rows(block, slot):
        def issue(r, carry):
            pltpu.make_async_copy(h_hbm.at[pl.ds(tok_ref[block * ROW_BLOCK + r], 1), :],
                                  xbuf.at[slot, pl.ds(r, 1), :], xsem.at[slot]).start(priority=1)
            return carry
        lax.fori_loop(0, ROW_BLOCK, issue, 0, unroll=ISSUE_UNROLL)

    def weight_copies(e, slot):
        return (pltpu.make_async_copy(wgu_hbm.at[layer, e], wgu_f.at[slot], wsem.at[0, slot]),
                pltpu.make_async_copy(wd_hbm.at[layer, e], wd_f.at[slot], wsem.at[1, slot]))

    @pl.when(i == 0)
    def _():
        for cp in weight_copies(be_ref[0], 0):
            cp.start()
        start_rows(0, 0)

    @pl.when(i + 1 < n_used)
    def _():
        start_rows(i + 1, 1 - xslot)

    @pl.when(jnp.logical_and(i < n_used, fresh_ref[i] == 1))
    def _():
        slot = wslot_ref[i]
        for cp in weight_copies(be_ref[i], slot):
            cp.wait()
        wgu_b[...] = wgu_f[slot].astype(BF16)
        wd_b[...] = wd_f[slot].astype(BF16)

        @pl.when(next_e_ref[i] >= 0)
        def _():
            for cp in weight_copies(next_e_ref[i], 1 - slot):
                cp.start()

    @pl.when(i < n_used)
    def _():
        pltpu.make_async_copy(h_hbm.at[pl.ds(0, ROW_BLOCK), :], xbuf.at[xslot], xsem.at[xslot]).wait()
        x = _rms(xbuf[xslot], lnw_ref[...]).astype(BF16)
        gu = _dot(x, wgu_b[...])
        gg = gu[:, :D_EXPERT]
        uu = gu[:, D_EXPERT:]
        act = (gg * (1.0 / (1.0 + jnp.exp(-gg))) * uu).astype(BF16)
        o_ref[...] = _dot(act, wd_b[...])

    @pl.when(i >= n_used)
    def _():
        o_ref[...] = jnp.zeros_like(o_ref)


def _experts(n_used, block_e, row_tok, h, ln_w, w_gu, w_down, layer):
    n_rows = row_tok.shape[0]
    n_blocks = n_rows // ROW_BLOCK
    D = h.shape[1]
    F2 = w_gu.shape[3]
    idx = jnp.arange(n_blocks, dtype=jnp.int32)
    used = idx < n_used[0]
    fresh = jnp.logical_and(used, jnp.concatenate([jnp.ones((1,), bool), block_e[1:] != block_e[:-1]]))
    run = jnp.cumsum(fresh.astype(jnp.int32)) - 1
    run_start = jnp.where(fresh, idx, n_blocks)
    next_start = lax.cummin(jnp.concatenate([run_start[1:], jnp.full((1,), n_blocks, jnp.int32)]), reverse=True)
    next_e = jnp.where(next_start < n_blocks, block_e[jnp.minimum(next_start, n_blocks - 1)], -1)
    return pl.pallas_call(
        functools.partial(_experts_body, layer),
        grid_spec=pltpu.PrefetchScalarGridSpec(
            num_scalar_prefetch=6,
            grid=(n_blocks,),
            in_specs=[pl.BlockSpec(memory_space=pl.ANY),
                      pl.BlockSpec((1, D), lambda i, *_: (0, 0)),
                      pl.BlockSpec(memory_space=pl.ANY),
                      pl.BlockSpec(memory_space=pl.ANY)],
            out_specs=pl.BlockSpec((ROW_BLOCK, D), lambda i, *_: (i, 0)),
            scratch_shapes=[pltpu.VMEM((2, ROW_BLOCK, D), F32),
                            pltpu.VMEM((2, D, F2), F32), pltpu.VMEM((2, F2 // 2, D), F32),
                            pltpu.VMEM((D, F2), BF16), pltpu.VMEM((F2 // 2, D), BF16),
                            pltpu.SemaphoreType.DMA((2,)), pltpu.SemaphoreType.DMA((2, 2))]),
        out_shape=jax.ShapeDtypeStruct((n_rows, D), F32),
        compiler_params=_params("arbitrary"),
        name="moe_experts",
    )(n_used, block_e, fresh.astype(jnp.int32), (run % 2).astype(jnp.int32), next_e.astype(jnp.int32), row_tok,
      h, ln_w.reshape(1, D), w_gu, w_down)


def _start_pair_gather(pos_ref, first_tok, n_tok, yb_hbm, ybuf, slot, sem, tb):
    D = yb_hbm.shape[1]

    def issue(r, carry):
        for k in range(TOP_K):
            pltpu.make_async_copy(yb_hbm.at[pl.ds(pos_ref[k * n_tok + first_tok + r], 1), :],
                                  ybuf.at[slot, pl.ds(r, 1), pl.ds(k * D, D)], sem.at[slot]).start(priority=k % 2)
        return carry
    lax.fori_loop(0, tb, issue, 0, unroll=ISSUE_UNROLL)


def _wait_pair_gather(yb_hbm, ybuf, slot, sem, tb):
    D = yb_hbm.shape[1]
    for k in range(TOP_K):
        pltpu.make_async_copy(yb_hbm.at[pl.ds(0, tb), :], ybuf.at[slot, :, pl.ds(k * D, D)], sem.at[slot]).wait()


def _combine_body(final_norm, pos_ref, yb_hbm, h_ref, r_ref, lnw_ref, o_ref, ybuf, sem):
    i = pl.program_id(0)
    n = pl.num_programs(0)
    tb, D = h_ref.shape
    slot = i % 2

    @pl.when(i == 0)
    def _():
        _start_pair_gather(pos_ref, 0, n * tb, yb_hbm, ybuf, 0, sem, tb)

    @pl.when(i + 1 < n)
    def _():
        _start_pair_gather(pos_ref, (i + 1) * tb, n * tb, yb_hbm, ybuf, 1 - slot, sem, tb)

    _wait_pair_gather(yb_hbm, ybuf, slot, sem, tb)
    out = h_ref[...]
    for k in range(TOP_K):
        out = out + ybuf[slot, :, k * D:(k + 1) * D] * r_ref[:, k:k + 1]
    if final_norm:
        out = _rms(out, lnw_ref[...])
    o_ref[...] = out


def _combine(pos, yb, h, routed, ln_out_w, final_norm, tb=128):
    S, D = h.shape
    return pl.pallas_call(
        functools.partial(_combine_body, final_norm),
        grid_spec=pltpu.PrefetchScalarGridSpec(
            num_scalar_prefetch=1,
            grid=(S // tb,),
            in_specs=[pl.BlockSpec(memory_space=pl.ANY),
                      pl.BlockSpec((tb, D), lambda i, pos: (i, 0)),
                      pl.BlockSpec((tb, LANES), lambda i, pos: (i, 0)),
                      pl.BlockSpec((1, D), lambda i, pos: (0, 0))],
            out_specs=pl.BlockSpec((tb, D), lambda i, pos: (i, 0)),
            scratch_shapes=[pltpu.VMEM((2, tb, TOP_K * D), F32), pltpu.SemaphoreType.DMA((2,))]),
        out_shape=jax.ShapeDtypeStruct((S, D), F32),
        compiler_params=_params("arbitrary"),
        name="moe_combine",
    )(pos, yb, h, routed, ln_out_w.reshape(1, D))


def _moe_layer(h, ln_w, wg, bg, we, be, w_gu, w_down, layer, ln_out_w, final_norm):
    T, D = h.shape
    routed, routed_t, counts = _router(h, ln_w, wg, bg, we, be)
    A = T * TOP_K
    n_blocks = -(-A // ROW_BLOCK) + N_EXPERTS
    n_rows = n_blocks * ROW_BLOCK
    counts = counts[0, N_GROUPS:N_GROUPS + N_EXPERTS].astype(jnp.int32)
    padded = (counts + ROW_BLOCK - 1) // ROW_BLOCK * ROW_BLOCK
    pad_end = jnp.cumsum(padded)
    pad_start = pad_end - padded
    expert_idx = routed_t[2:2 + TOP_K].astype(jnp.int32)
    slot = routed_t[4:4 + TOP_K].astype(jnp.int32)
    is_e = expert_idx[..., None] == jnp.arange(N_EXPERTS, dtype=jnp.int32)
    pos = (jnp.sum(jnp.where(is_e, pad_start, 0), axis=-1) + slot).reshape(A)
    block_start = jnp.arange(n_blocks, dtype=jnp.int32) * ROW_BLOCK
    block_e = jnp.minimum(jnp.sum(pad_end[None, :] <= block_start[:, None], axis=1), N_EXPERTS - 1).astype(jnp.int32)
    n_used = (pad_end[-1] // ROW_BLOCK).astype(jnp.int32).reshape(1)
    yb = _experts(n_used, block_e, _row_tokens(pos, n_rows), h, ln_w, w_gu, w_down, layer)
    return _combine(pos, yb, h, routed, ln_out_w, final_norm)


def _sigmoid(x):
    return 1.0 / (1.0 + jnp.exp(-x))


def _rwkv_prep_body(h_ref, halo_ref, lnw_ref, mu_ref, w0_ref, w1_ref, w2_ref, a0_ref, a1_ref, a2_ref, g1_ref,
                    g2_ref, xs_ref, lw_ref, a_ref, g_ref):
    i = pl.program_id(0)
    lnw = lnw_ref[...]
    hn = _rms(h_ref[...], lnw)
    ts = hn.shape[0]
    last = _rms(halo_ref[...], lnw)[SUBLANES - 1:SUBLANES, :] * (i > 0).astype(F32)
    row = lax.broadcasted_iota(jnp.int32, (ts, 1), 0)
    prev = jnp.where(row == 0, last, pltpu.roll(hn, 1, axis=0))
    xx = prev - hn
    mix = lambda n: (hn + xx * mu_ref[n:n + 1, :]).astype(BF16)
    n_out = xs_ref.shape[0]
    for n in range(n_out):
        xs_ref[n] = mix(n)
    z = w0_ref[...] + _dot(jnp.tanh(_dot(mix(n_out), w1_ref[...])).astype(BF16), w2_ref[...])
    u = -z
    softplus = jnp.maximum(u, 0.0) + jnp.log1p(jnp.exp(-jnp.abs(u)))
    lw_ref[...] = -jnp.exp(-softplus - 0.5)
    a_ref[...] = _sigmoid(a0_ref[...] + _dot(_dot(mix(n_out + 1), a1_ref[...]).astype(BF16), a2_ref[...]))
    g_ref[...] = _dot(_sigmoid(_dot(mix(n_out + 2), g1_ref[...])).astype(BF16), g2_ref[...])


def _rwkv_prep(h, ln_w, mu, w0, w1, w2, a0, a1, a2, g1, g2, ts=256):
    S, D = h.shape
    n_mix = mu.shape[0]
    n_out = n_mix - 3
    halo_blocks = ts // SUBLANES

    def pad_lora(wa, wb):
        r = wa.shape[1]
        rp = -(-r // LANES) * LANES
        return (jnp.pad(wa, ((0, 0), (0, rp - r))).astype(BF16), jnp.pad(wb, ((0, rp - r), (0, 0))).astype(BF16))

    w1p, w2p = pad_lora(w1, w2)
    a1p, a2p = pad_lora(a1, a2)
    g1p, g2p = pad_lora(g1, g2)
    full = lambda arr: pl.BlockSpec(arr.shape, lambda i: (0,) * arr.ndim)
    vec = pl.BlockSpec((1, D), lambda i: (0, 0))
    row = pl.BlockSpec((ts, D), lambda i: (i, 0))
    return pl.pallas_call(
        _rwkv_prep_body,
        grid=(S // ts,),
        in_specs=[row,
                  pl.BlockSpec((SUBLANES, D), lambda i: (jnp.maximum(i * halo_blocks - 1, 0), 0)),
                  vec, full(mu), vec, full(w1p), full(w2p), vec, full(a1p), full(a2p), full(g1p), full(g2p)],
        out_specs=[pl.BlockSpec((n_out, ts, D), lambda i: (0, i, 0)), row, row, row],
        out_shape=[jax.ShapeDtypeStruct((n_out, S, D), BF16)] + [jax.ShapeDtypeStruct((S, D), F32)] * 3,
        compiler_params=_params("arbitrary"),
        name="rwkv_prep",
    )(h, h, ln_w.reshape(1, D), mu, w0.reshape(1, D), w1p, w2p, a0.reshape(1, D), a1p, a2p, g1p, g2p)


def _scan_chunk(r, k, v, lw, a, g, kkw, kaw, rkw, lnw, lnb, state, c):
    masks, seg_f, seg_b, tri, strict, incl, eye = c
    bf = lambda x: x.astype(BF16)
    stack = lambda xb: jnp.concatenate([xb * m for m in masks], axis=0)
    rows = lambda *xs: jnp.concatenate(xs, axis=0)
    cols = lambda *xs: jnp.concatenate(xs, axis=1)

    kk = k * kkw
    k2 = k * (1.0 + (a - 1.0) * kaw)
    sums = _segsum(rows(kk * kk, r * k2 * rkw), seg_b)
    cw = _cumsum_rows(tri, lw)
    yield
    kk = kk / jnp.maximum(jnp.sqrt(sums[:CHUNK]), 1e-12)
    bonus = sums[CHUNK:] * v
    bv = kk * a
    cl = cw[CHUNK - 1:CHUNK, :]
    at = bf(-kk * jnp.exp(cw - lw))
    rt = bf(r * jnp.exp(cw))
    dinv = jnp.exp(-cw)
    drem = jnp.exp(cl - cw)
    vb = bf(v)
    vs = stack(vb)
    gram = _dot(rows(at, rt), rows(stack(bf(bv * dinv)), stack(bf(k2 * dinv))), NT)
    bk_t = bf(rows(bv * drem, k2 * drem).T)
    decay_rows = jnp.broadcast_to(jnp.exp(cl), (GROUP, GROUP)).T
    yield
    a_ab = jnp.where(strict, gram[:CHUNK, :GROUP], 0.0)
    a_ak = bf(jnp.where(strict, gram[:CHUNK, GROUP:], 0.0))
    a_rb = bf(jnp.where(incl, gram[CHUNK:, :GROUP], 0.0))
    a_rk = bf(jnp.where(incl, gram[CHUNK:, GROUP:], 0.0))
    inv = eye + a_ab
    pw = _dot(bf(a_ab), stack(bf(a_ab)))
    xy = _dot(rows(cols(at, a_ak), cols(rt, a_rk)), rows(bf(state), vs))
    yield
    span = 2
    while span < CHUNK // 2:
        pwb = bf(pw)
        both = _dot(rows(bf(inv), pwb), stack(pwb))
        yield
        inv = inv + both[:CHUNK]
        pw = both[CHUNK:]
        span *= 2
    inv = inv + _dot(bf(inv), stack(bf(pw)))
    yield
    ub = bf(_dot(bf(inv), stack(bf(xy[:CHUNK]))))
    yield
    y = xy[CHUNK:] + _dot(a_rb, stack(ub))
    new_state = state * decay_rows + seg_f * _dot(bk_t, rows(ub, vb))
    yield
    inv_n = 1.0 / RWKV_HEAD_DIM
    mean = _segsum(y, seg_b) * inv_n
    yield
    yc = y - mean
    var = _segsum(yc * yc, seg_b) * inv_n
    yield
    yn = yc * lax.rsqrt(var + GN_EPS) * lnw + lnb
    return ((yn + bonus) * g), new_state


def _run_interleaved(gens):
    results = [None] * len(gens)
    live = list(range(len(gens)))
    while live:
        for i in list(live):
            try:
                next(gens[i])
            except StopIteration as done:
                results[i] = done.value
                live.remove(i)
    return results


def _segsum(x, seg_b):
    return _dot(x.astype(BF16), seg_b)


def _cumsum_rows(tri, x):
    hi = x.astype(BF16)
    r1 = x - hi.astype(F32)
    mid = r1.astype(BF16)
    lo = (r1 - mid.astype(F32)).astype(BF16)
    return _dot(jnp.concatenate([tri, tri, tri], axis=1), jnp.concatenate([hi, mid, lo], axis=0))


def _scan_consts():
    shift = RWKV_HEAD_DIM.bit_length() - 1
    lane_head = lax.broadcasted_iota(jnp.int32, (1, GROUP), 1) >> shift
    masks = [(lane_head == h).astype(F32).astype(BF16) for h in range(GROUP_HEADS)]
    ri = lax.broadcasted_iota(jnp.int32, (GROUP, GROUP), 0) >> shift
    ci = lax.broadcasted_iota(jnp.int32, (GROUP, GROUP), 1) >> shift
    seg_f = (ri == ci).astype(F32)
    tri = (lax.broadcasted_iota(jnp.int32, (CHUNK, CHUNK), 1)
           <= lax.broadcasted_iota(jnp.int32, (CHUNK, CHUNK), 0)).astype(F32).astype(BF16)
    t = lax.broadcasted_iota(jnp.int32, (CHUNK, GROUP), 0)
    s = lax.broadcasted_iota(jnp.int32, (CHUNK, GROUP), 1) & (RWKV_HEAD_DIM - 1)
    return (masks, seg_f, seg_f.astype(BF16), tri, s < t, s <= t, (s == t).astype(F32))


def _scan_body(r_ref, k_ref, v_ref, lw_ref, a_ref, g_ref, kkw_ref, kaw_ref, rkw_ref, lnw_ref, lnb_ref,
               o_ref, state_ref):
    @pl.when(pl.program_id(1) == 0)
    def _():
        state_ref[...] = jnp.zeros_like(state_ref)

    n_groups = o_ref.shape[1] // GROUP

    def chunk_step(j, carry):
        consts = _scan_consts()
        rows = pl.ds(pl.multiple_of(j * CHUNK, CHUNK), CHUNK)
        lanes = [slice(p * GROUP, (p + 1) * GROUP) for p in range(n_groups)]
        args = [(r_ref[rows, ln], k_ref[rows, ln], v_ref[rows, ln], lw_ref[rows, ln], a_ref[rows, ln],
                 g_ref[rows, ln], kkw_ref[:, ln], kaw_ref[:, ln], rkw_ref[:, ln], lnw_ref[:, ln],
                 lnb_ref[:, ln], state_ref[p]) for p, ln in enumerate(lanes)]
        results = _run_interleaved([_scan_chunk(*a, consts) for a in args])
        for p, (ln, (out, new_state)) in enumerate(zip(lanes, results)):
            state_ref[p] = new_state
            o_ref[rows, ln] = out.astype(o_ref.dtype)
        return carry

    lax.fori_loop(0, o_ref.shape[0] // CHUNK, chunk_step, 0)


def _rwkv_scan(rkv, lw, a, g, k_k, k_a, r_k, ln_w, ln_b):
    _, S, D = rkv.shape
    tr, tl = min(SCAN_ROWS, S), SCAN_LANES
    blk = lambda n: pl.BlockSpec((None, tr, tl), lambda p, i: (n, i, p))
    row = pl.BlockSpec((tr, tl), lambda p, i: (i, p))
    vec = pl.BlockSpec((1, tl), lambda p, i: (0, p))
    as_row = lambda w: w.reshape(1, D)
    return pl.pallas_call(
        _scan_body,
        grid=(D // tl, S // tr),
        in_specs=[blk(0), blk(1), blk(2), row, row, row, vec, vec, vec, vec, vec],
        out_specs=row,
        out_shape=jax.ShapeDtypeStruct((S, D), BF16),
        scratch_shapes=[pltpu.VMEM((tl // GROUP, GROUP, GROUP), F32)],
        compiler_params=_params("arbitrary", "arbitrary"),
        name="rwkv_scan",
    )(rkv, rkv, rkv, lw, a, g, as_row(k_k), as_row(k_a), as_row(r_k), as_row(ln_w), as_row(ln_b))


def _rwkv_layer(h, ln_w, mu, w_rkv, w_o, j, w0, w1, w2, a0, a1, a2, g1, g2, k_k, k_a, r_k, ln_gn_w, ln_gn_b):
    xs, lw, a, g = _rwkv_prep(h, ln_w, mu, w0, w1, w2, a0, a1, a2, g1, g2)
    rkv = _rkv_proj(xs, w_rkv, j)
    z = _rwkv_scan(rkv, lw, a, g, k_k, k_a, r_k, ln_gn_w, ln_gn_b)
    return _out_proj(z, w_o, j, h)


def kernel(x, mem, ln_mix_w, ln_xattn_w, ln_mem_w, xattn_wq, xattn_wkv, xattn_wo, ln_ffn_w, router_group_w, router_group_b, router_expert_w, router_expert_b, moe_w_gate_up, moe_w_down, pool_w, pool_scale, rwkv_mu, rwkv_w_rkv, rwkv_w_o, rwkv_w0, rwkv_w1, rwkv_w2, rwkv_a0, rwkv_a1, rwkv_a2, rwkv_g1, rwkv_g2, rwkv_k_k, rwkv_k_a, rwkv_r_k, rwkv_ln_w, rwkv_ln_b, ln_out_w):
    B, S, D = x.shape
    depth = ln_mix_w.shape[0]
    outs = []
    for b in range(B):
        h = x[b]
        for i in range(depth):
            j = i // 2
            if i % 2 == 0:
                h = _pool_layer(h, ln_mix_w[i], pool_w[j], pool_scale[j])
            else:
                h = _rwkv_layer(h, ln_mix_w[i], rwkv_mu[j], rwkv_w_rkv, rwkv_w_o, j, rwkv_w0[j], rwkv_w1[j],
                                rwkv_w2[j], rwkv_a0[j], rwkv_a1[j], rwkv_a2[j], rwkv_g1[j], rwkv_g2[j],
                                rwkv_k_k[j], rwkv_k_a[j], rwkv_r_k[j].reshape(D), rwkv_ln_w[j], rwkv_ln_b[j])
            kv = _mem_kv(mem[b], ln_mem_w[i], xattn_wkv, i)
            h = _xattn_layer(h, ln_xattn_w[i], xattn_wq, kv, xattn_wo, i)
            h = _moe_layer(h, ln_ffn_w[i], router_group_w[i], router_group_b[i], router_expert_w[i],
                           router_expert_b[i], moe_w_gate_up, moe_w_down, i, ln_out_w,
                           final_norm=(i == depth - 1))
        outs.append(h)
    return outs[0][None] if B == 1 else jnp.stack(outs, axis=0)
```

```python
import functools

import jax
import jax.numpy as jnp
from jax import lax
from jax.experimental import pallas as pl
from jax.experimental.pallas import tpu as pltpu

F32 = jnp.float32
BF16 = jnp.bfloat16

D_MODEL = 2048
POOL_WINDOWS = (2, 4, 8, 16)
POOL_GROUP_DIM = D_MODEL // len(POOL_WINDOWS)
MAX_WINDOW = max(POOL_WINDOWS)
RWKV_HEAD_DIM = 64
GN_EPS = 64e-5
XATTN_HEADS = 4
XATTN_HEAD_DIM = D_MODEL // XATTN_HEADS
N_GROUPS = 8
EXPERTS_PER_GROUP = 8
N_EXPERTS = N_GROUPS * EXPERTS_PER_GROUP
TOP_K = 2
D_EXPERT = D_MODEL // 8
ROW_BLOCK = 128
RMS_EPS = 1e-6

LANES = 128
SUBLANES = 8
VMEM_LIMIT = 56 * 1024 * 1024

CHUNK = 64
GROUP_HEADS = 4
GROUP = GROUP_HEADS * RWKV_HEAD_DIM
SLAB = D_MODEL // LANES
SLAB_PITCH = SLAB + SUBLANES
ISSUE_UNROLL = 8
WEIGHT_STAGE_ROWS = 256
SCAN_ROWS = 256
SCAN_LANES = 2048

NN = (((1,), (0,)), ((), ()))
NT = (((1,), (1,)), ((), ()))


def _params(*sem):
    return pltpu.CompilerParams(dimension_semantics=sem, vmem_limit_bytes=VMEM_LIMIT)


def _rms(x, w):
    return x * lax.rsqrt(jnp.mean(x * x, axis=-1, keepdims=True) + RMS_EPS) * w


def _dot(a, b, dims=NN):
    return lax.dot_general(a, b, dims, preferred_element_type=F32)


def _split(x):
    hi = x.astype(BF16)
    return hi, (x - hi.astype(F32)).astype(BF16)


def _store_slabs(ref, x):
    for j in range(SLAB):
        ref[pl.ds(j, x.shape[0], stride=SLAB), :] = x[:, j * LANES:(j + 1) * LANES]


def _load_slabs(ref):
    n = ref.shape[0] // SLAB_PITCH
    return jnp.concatenate([ref[pl.ds(j, n, stride=SLAB_PITCH), :] for j in range(SLAB)], axis=-1)


def _slab(ref, row, pitch=SLAB):
    return ref.at[pl.ds(pl.multiple_of(row * pitch, SUBLANES), SLAB), :]


def _rkv_proj_body(x_ref, w_ref, o_ref, wb_ref):
    @pl.when(pl.program_id(2) == 0)
    def _():
        wb_ref[...] = w_ref[...].astype(BF16)

    o_ref[...] = _dot(x_ref[...], wb_ref[...])


def _rkv_proj(xs, w_rkv, j, tm=512, tn=1024):
    _, M, K = xs.shape
    B, N = w_rkv.shape[1], w_rkv.shape[3]
    return pl.pallas_call(
        _rkv_proj_body,
        grid=(B, N // tn, M // tm),
        in_specs=[pl.BlockSpec((None, tm, K), lambda b, n, i: (b, i, 0)),
                  pl.BlockSpec((None, None, K, tn), lambda b, n, i: (j, b, 0, n))],
        out_specs=pl.BlockSpec((None, tm, tn), lambda b, n, i: (b, i, n)),
        out_shape=jax.ShapeDtypeStruct((B, M, N), F32),
        scratch_shapes=[pltpu.VMEM((K, tn), BF16)],
        compiler_params=_params("arbitrary", "arbitrary", "arbitrary"),
        name="rwkv_rkv_proj",
    )(xs, w_rkv)


def _out_proj_body(x_ref, w_ref, r_ref, o_ref, wb_ref):
    @pl.when(pl.program_id(1) == 0)
    def _():
        wb_ref[...] = w_ref[...].astype(BF16)

    o_ref[...] = r_ref[...] + _dot(x_ref[...], wb_ref[...])


def _out_proj(x, w_o, j, res, tm=512, tn=1024):
    M, K = x.shape
    N = w_o.shape[2]
    return pl.pallas_call(
        _out_proj_body,
        grid=(N // tn, M // tm),
        in_specs=[pl.BlockSpec((tm, K), lambda n, i: (i, 0)),
                  pl.BlockSpec((None, K, tn), lambda n, i: (j, 0, n)),
                  pl.BlockSpec((tm, tn), lambda n, i: (i, n))],
        out_specs=pl.BlockSpec((tm, tn), lambda n, i: (i, n)),
        out_shape=jax.ShapeDtypeStruct((M, N), F32),
        scratch_shapes=[pltpu.VMEM((K, tn), BF16)],
        compiler_params=_params("arbitrary", "arbitrary"),
        name="rwkv_out_proj",
    )(x, w_o, res)


def _pool_body(h_ref, halo_ref, lnw_ref, pw_ref, ps_ref, o_ref):
    i = pl.program_id(0)
    ts = h_ref.shape[0]
    h = h_ref[...]
    lnw = lnw_ref[...]
    x = _rms(h, lnw)
    xh = _rms(halo_ref[...], lnw) * (i > 0).astype(F32)
    xe = jnp.concatenate([xh, x], axis=0)
    t = i * ts + lax.broadcasted_iota(jnp.int32, (ts, 1), 0)
    outs = []
    for g, win in enumerate(POOL_WINDOWS):
        sl = slice(g * POOL_GROUP_DIM, (g + 1) * POOL_GROUP_DIM)
        acc = xe[:, sl]
        span = 1
        while span < win:
            acc = acc + pltpu.roll(acc, span, axis=0)
            span *= 2
        cnt = jnp.minimum(t + 1, win).astype(F32)
        pooled = acc[MAX_WINDOW:, :] / cnt - x[:, sl]
        outs.append(_dot(pooled.astype(BF16), pw_ref[g]))
    y = jnp.concatenate(outs, axis=-1)
    o_ref[...] = h + y * ps_ref[...]


def _pool_layer(h, ln_w, pool_w, pool_scale, ts=512):
    S, D = h.shape
    G = pool_w.shape[0]
    halo_blocks = ts // MAX_WINDOW
    return pl.pallas_call(
        _pool_body,
        grid=(S // ts,),
        in_specs=[pl.BlockSpec((ts, D), lambda i: (i, 0)),
                  pl.BlockSpec((MAX_WINDOW, D), lambda i: (jnp.maximum(i * halo_blocks - 1, 0), 0)),
                  pl.BlockSpec((1, D), lambda i: (0, 0)),
                  pl.BlockSpec((G, POOL_GROUP_DIM, POOL_GROUP_DIM), lambda i: (0, 0, 0)),
                  pl.BlockSpec((1, D), lambda i: (0, 0))],
        out_specs=pl.BlockSpec((ts, D), lambda i: (i, 0)),
        out_shape=jax.ShapeDtypeStruct((S, D), F32),
        compiler_params=_params("arbitrary"),
        name="pool_layer",
    )(h, h, ln_w.reshape(1, D), pool_w.astype(BF16), pool_scale.reshape(1, D))


def _memkv_body(mem_ref, lnw_ref, w_ref, o_ref):
    mn = _rms(mem_ref[...], lnw_ref[...])
    o_ref[...] = _dot(mn.astype(BF16), w_ref[...].astype(BF16)).astype(o_ref.dtype)


def _mem_kv(mem, ln_w, wkv, layer, tn=1024):
    M, D = mem.shape
    N = wkv.shape[2]
    return pl.pallas_call(
        _memkv_body,
        grid=(N // tn,),
        in_specs=[pl.BlockSpec((M, D), lambda n: (0, 0)),
                  pl.BlockSpec((1, D), lambda n: (0, 0)),
                  pl.BlockSpec((None, D, tn), lambda n: (layer, 0, n))],
        out_specs=pl.BlockSpec((M, tn), lambda n: (0, n)),
        out_shape=jax.ShapeDtypeStruct((M, N), BF16),
        compiler_params=_params("arbitrary"),
        name="xattn_mem_kv",
    )(mem, ln_w.reshape(1, D), wkv)


def _load_weight_bf16(w_hbm, dst, stage, sem):
    rows = stage.shape[1]
    n_chunks = w_hbm.shape[0] // rows

    def chunk_copy(c):
        return pltpu.make_async_copy(w_hbm.at[pl.ds(c * rows, rows), :], stage.at[c % 2], sem.at[c % 2])

    chunk_copy(0).start()
    for c in range(n_chunks):
        if c + 1 < n_chunks:
            chunk_copy(c + 1).start()
        chunk_copy(c).wait()
        dst[pl.ds(c * rows, rows), :] = stage[c % 2].astype(BF16)


def _xattn_body(layer, h_ref, lnw_ref, k_ref, v_ref, wq_hbm, wo_hbm, o_ref, wq_ref, wo_ref, stage, sem):
    @pl.when(pl.program_id(0) == 0)
    def _():
        _load_weight_bf16(wq_hbm.at[layer], wq_ref, stage, sem)
        _load_weight_bf16(wo_hbm.at[layer], wo_ref, stage, sem)

    h = h_ref[...]
    hn = _rms(h, lnw_ref[...]).astype(BF16)
    q = _dot(hn, wq_ref[...])
    scale = XATTN_HEAD_DIM ** -0.5
    heads = []
    for hd in range(XATTN_HEADS):
        sl = slice(hd * XATTN_HEAD_DIM, (hd + 1) * XATTN_HEAD_DIM)
        s = _dot(q[:, sl].astype(BF16), k_ref[:, sl], NT) * scale
        s = s - jnp.max(s, axis=-1, keepdims=True)
        e = jnp.exp(s)
        p = e / jnp.sum(e, axis=-1, keepdims=True)
        heads.append(_dot(p.astype(BF16), v_ref[:, sl]))
    o = jnp.concatenate(heads, axis=-1).astype(BF16)
    o_ref[...] = h + _dot(o, wo_ref[...])


def _xattn_layer(h, ln_w, wq, kv, wo, layer, ts=256):
    S, D = h.shape
    M = kv.shape[0]
    return pl.pallas_call(
        functools.partial(_xattn_body, layer),
        grid=(S // ts,),
        in_specs=[pl.BlockSpec((ts, D), lambda i: (i, 0)),
                  pl.BlockSpec((1, D), lambda i: (0, 0)),
                  pl.BlockSpec((M, D), lambda i: (0, 0)),
                  pl.BlockSpec((M, D), lambda i: (0, 1)),
                  pl.BlockSpec(memory_space=pl.ANY),
                  pl.BlockSpec(memory_space=pl.ANY)],
        out_specs=pl.BlockSpec((ts, D), lambda i: (i, 0)),
        out_shape=jax.ShapeDtypeStruct((S, D), F32),
        scratch_shapes=[pltpu.VMEM((D, D), BF16), pltpu.VMEM((D, D), BF16),
                        pltpu.VMEM((2, WEIGHT_STAGE_ROWS, D), F32), pltpu.SemaphoreType.DMA((2,))],
        compiler_params=_params("arbitrary"),
        name="xattn_layer",
    )(h, ln_w.reshape(1, D), kv, kv, wq, wo)


def _router_body(h_ref, lnw_ref, whi_ref, wlo_ref, b_ref, hn_ref, r_ref, rt_ref, cnt_ref):
    i = pl.program_id(0)

    @pl.when(i == 0)
    def _():
        cnt_ref[...] = jnp.zeros_like(cnt_ref)

    hn = _rms(h_ref[...], lnw_ref[...])
    _store_slabs(hn_ref, hn)
    xh, xl = _split(hn)
    whi = whi_ref[...]
    lg = _dot(xh, whi) + (_dot(xh, wlo_ref[...]) + _dot(xl, whi)) + b_ref[...]
    ts = lg.shape[0]
    lane = lax.broadcasted_iota(jnp.int32, (ts, LANES), 1)
    lanef = lane.astype(F32)
    neg = jnp.float32(-jnp.inf)
    gl = jnp.where(lane < N_GROUPS, lg, neg)
    gmax = jnp.max(gl, axis=-1, keepdims=True)
    pg_top = 1.0 / jnp.sum(jnp.exp(gl - gmax), axis=-1, keepdims=True)
    g_idx = jnp.min(jnp.where(gl == gmax, lanef, float(LANES)), axis=-1, keepdims=True)
    lo_lane = g_idx * EXPERTS_PER_GROUP + N_GROUPS
    el = jnp.where(lanef >= lo_lane, jnp.where(lanef < lo_lane + EXPERTS_PER_GROUP, lg, neg), neg)
    m1 = jnp.max(el, axis=-1, keepdims=True)
    i1 = jnp.min(jnp.where(el == m1, lanef, float(LANES)), axis=-1, keepdims=True)
    el2 = jnp.where(lanef == i1, neg, el)
    m2 = jnp.max(el2, axis=-1, keepdims=True)
    i2 = jnp.min(jnp.where(el2 == m2, lanef, float(LANES)), axis=-1, keepdims=True)
    e2 = jnp.exp(m2 - m1)
    g1 = pg_top / (1.0 + e2)
    g2 = pg_top * e2 / (1.0 + e2)
    hit1 = (lanef == i1).astype(F32)
    hit2 = (lanef == i2).astype(F32)
    hits = hit1 + hit2
    earlier = (lax.broadcasted_iota(jnp.int32, (ts, ts), 1)
               < lax.broadcasted_iota(jnp.int32, (ts, ts), 0)).astype(F32).astype(BF16)
    before = _dot(earlier, hits.astype(BF16)) + cnt_ref[...]
    r1 = jnp.sum(before * hit1, axis=-1, keepdims=True)
    r2 = jnp.sum(before * hit2, axis=-1, keepdims=True)
    cnt_ref[...] += jnp.sum(hits, axis=0, keepdims=True)
    out = jnp.where(lane == 0, g1, 0.0)
    out = jnp.where(lane == 1, g2, out)
    out = jnp.where(lane == 2, i1 - N_GROUPS, out)
    out = jnp.where(lane == 3, i2 - N_GROUPS, out)
    out = jnp.where(lane == 4, r1, out)
    out = jnp.where(lane == 5, r2, out)
    r_ref[...] = out
    rt_ref[...] = out.T[:SUBLANES, :]


def _router(h, ln_w, wg, bg, we, be, ts=512):
    S, D = h.shape
    unused = LANES - N_GROUPS - N_EXPERTS
    w = jnp.concatenate([wg, we, jnp.zeros((D, unused), F32)], axis=1)
    b = jnp.concatenate([bg, be, jnp.zeros((unused,), F32)]).reshape(1, LANES)
    whi, wlo = _split(w)
    return pl.pallas_call(
        _router_body,
        grid=(S // ts,),
        in_specs=[pl.BlockSpec((ts, D), lambda i: (i, 0)),
                  pl.BlockSpec((1, D), lambda i: (0, 0)),
                  pl.BlockSpec((D, LANES), lambda i: (0, 0)),
                  pl.BlockSpec((D, LANES), lambda i: (0, 0)),
                  pl.BlockSpec((1, LANES), lambda i: (0, 0))],
        out_specs=[pl.BlockSpec((ts * SLAB, LANES), lambda i: (i, 0)),
                   pl.BlockSpec((ts, LANES), lambda i: (i, 0)),
                   pl.BlockSpec((SUBLANES, ts), lambda i: (0, i)),
                   pl.BlockSpec((1, LANES), lambda i: (0, 0))],
        out_shape=[jax.ShapeDtypeStruct((S * SLAB, LANES), F32), jax.ShapeDtypeStruct((S, LANES), F32),
                   jax.ShapeDtypeStruct((SUBLANES, S), F32), jax.ShapeDtypeStruct((1, LANES), F32)],
        compiler_params=_params("arbitrary"),
        name="moe_router",
    )(h, ln_w.reshape(1, D), whi, wlo, b)


def _row_tokens_body(pos_ref, tok_ref):
    n_tok = pos_ref.shape[0] // TOP_K

    def clear(j, carry):
        tok_ref[j] = 0
        return carry
    lax.fori_loop(0, tok_ref.shape[0], clear, 0, unroll=ISSUE_UNROLL)

    def put(t, carry):
        for k in range(TOP_K):
            tok_ref[pos_ref[k * n_tok + t]] = t
        return carry
    lax.fori_loop(0, n_tok, put, 0, unroll=ISSUE_UNROLL)


def _row_tokens(pos, n_rows):
    return pl.pallas_call(
        _row_tokens_body,
        grid_spec=pltpu.PrefetchScalarGridSpec(
            num_scalar_prefetch=1,
            grid=(1,),
            in_specs=[],
            out_specs=pl.BlockSpec(memory_space=pltpu.SMEM)),
        out_shape=jax.ShapeDtypeStruct((n_rows,), jnp.int32),
        compiler_params=_params("arbitrary"),
        name="moe_row_tokens",
    )(pos)


def _experts_body(layer, nblk_ref, be_ref, fresh_ref, wslot_ref, next_e_ref, tok_ref, hn_hbm, wgu_hbm, wd_hbm,
                  o_ref, xbuf, wgu_f, wd_f, wgu_b, wd_b, xsem, wsem):
    i = pl.program_id(0)
    n_used = nblk_ref[0]
    xslot = i % 2

    def start_rows(block, slot):
        def issue(r, carry):
            pltpu.make_async_copy(_slab(hn_hbm, tok_ref[block * ROW_BLOCK + r]), _slab(xbuf.at[slot], r, SLAB_PITCH),
                                  xsem.at[slot]).start(priority=1)
            return carry
        lax.fori_loop(0, ROW_BLOCK, issue, 0, unroll=ISSUE_UNROLL)

    def weight_copies(e, slot):
        return (pltpu.make_async_copy(wgu_hbm.at[layer, e], wgu_f.at[slot], wsem.at[0, slot]),
                pltpu.make_async_copy(wd_hbm.at[layer, e], wd_f.at[slot], wsem.at[1, slot]))

    @pl.when(i == 0)
    def _():
        for cp in weight_copies(be_ref[0], 0):
            cp.start()
        start_rows(0, 0)

    @pl.when(i + 1 < n_used)
    def _():
        start_rows(i + 1, 1 - xslot)

    @pl.when(jnp.logical_and(i < n_used, fresh_ref[i] == 1))
    def _():
        slot = wslot_ref[i]
        for cp in weight_copies(be_ref[i], slot):
            cp.wait()
        wgu_b[...] = wgu_f[slot].astype(BF16)
        wd_b[...] = wd_f[slot].astype(BF16)

        @pl.when(next_e_ref[i] >= 0)
        def _():
            for cp in weight_copies(next_e_ref[i], 1 - slot):
                cp.start()

    @pl.when(i < n_used)
    def _():
        pltpu.make_async_copy(hn_hbm.at[pl.ds(0, ROW_BLOCK * SLAB), :],
                              xbuf.at[xslot, pl.ds(0, ROW_BLOCK * SLAB), :], xsem.at[xslot]).wait()
        gu = _dot(_load_slabs(xbuf.at[xslot]).astype(BF16), wgu_b[...])
        gg = gu[:, :D_EXPERT]
        uu = gu[:, D_EXPERT:]
        act = (gg * (1.0 / (1.0 + jnp.exp(-gg))) * uu).astype(BF16)
        _store_slabs(o_ref, _dot(act, wd_b[...]))

    @pl.when(i >= n_used)
    def _():
        o_ref[...] = jnp.zeros_like(o_ref)


def _experts(n_used, block_e, row_tok, hn, w_gu, w_down, layer):
    n_rows = row_tok.shape[0]
    n_blocks = n_rows // ROW_BLOCK
    D = w_gu.shape[2]
    F2 = w_gu.shape[3]
    idx = jnp.arange(n_blocks, dtype=jnp.int32)
    used = idx < n_used[0]
    fresh = jnp.logical_and(used, jnp.concatenate([jnp.ones((1,), bool), block_e[1:] != block_e[:-1]]))
    run = jnp.cumsum(fresh.astype(jnp.int32)) - 1
    run_start = jnp.where(fresh, idx, n_blocks)
    next_start = lax.cummin(jnp.concatenate([run_start[1:], jnp.full((1,), n_blocks, jnp.int32)]), reverse=True)
    next_e = jnp.where(next_start < n_blocks, block_e[jnp.minimum(next_start, n_blocks - 1)], -1)
    return pl.pallas_call(
        functools.partial(_experts_body, layer),
        grid_spec=pltpu.PrefetchScalarGridSpec(
            num_scalar_prefetch=6,
            grid=(n_blocks,),
            in_specs=[pl.BlockSpec(memory_space=pl.ANY),
                      pl.BlockSpec(memory_space=pl.ANY),
                      pl.BlockSpec(memory_space=pl.ANY)],
            out_specs=pl.BlockSpec((ROW_BLOCK * SLAB, LANES), lambda i, *_: (i, 0)),
            scratch_shapes=[pltpu.VMEM((2, ROW_BLOCK * SLAB_PITCH, LANES), F32),
                            pltpu.VMEM((2, D, F2), F32), pltpu.VMEM((2, F2 // 2, D), F32),
                            pltpu.VMEM((D, F2), BF16), pltpu.VMEM((F2 // 2, D), BF16),
                            pltpu.SemaphoreType.DMA((2,)), pltpu.SemaphoreType.DMA((2, 2))]),
        out_shape=jax.ShapeDtypeStruct((n_rows * SLAB, LANES), F32),
        compiler_params=_params("arbitrary"),
        name="moe_experts",
    )(n_used, block_e, fresh.astype(jnp.int32), (run % 2).astype(jnp.int32), next_e.astype(jnp.int32), row_tok,
      hn, w_gu, w_down)


def _start_pair_gather(pos_ref, first_tok, n_tok, yb_hbm, ybuf, slot, sem, tb):
    def issue(r, carry):
        for k in range(TOP_K):
            pltpu.make_async_copy(_slab(yb_hbm, pos_ref[k * n_tok + first_tok + r]), _slab(ybuf.at[slot, k], r, SLAB_PITCH),
                                  sem.at[slot]).start(priority=k % 2)
        return carry
    lax.fori_loop(0, tb, issue, 0, unroll=ISSUE_UNROLL)


def _wait_pair_gather(yb_hbm, ybuf, slot, sem, tb):
    for k in range(TOP_K):
        pltpu.make_async_copy(yb_hbm.at[pl.ds(0, tb * SLAB), :], ybuf.at[slot, k, pl.ds(0, tb * SLAB), :],
                              sem.at[slot]).wait()


def _combine_body(final_norm, pos_ref, yb_hbm, h_ref, r_ref, lnw_ref, o_ref, ybuf, sem):
    i = pl.program_id(0)
    n = pl.num_programs(0)
    tb = h_ref.shape[0]
    slot = i % 2

    @pl.when(i == 0)
    def _():
        _start_pair_gather(pos_ref, 0, n * tb, yb_hbm, ybuf, 0, sem, tb)

    @pl.when(i + 1 < n)
    def _():
        _start_pair_gather(pos_ref, (i + 1) * tb, n * tb, yb_hbm, ybuf, 1 - slot, sem, tb)

    _wait_pair_gather(yb_hbm, ybuf, slot, sem, tb)
    out = h_ref[...]
    for k in range(TOP_K):
        out = out + _load_slabs(ybuf.at[slot, k]) * r_ref[:, k:k + 1]
    if final_norm:
        out = _rms(out, lnw_ref[...])
    o_ref[...] = out


def _combine(pos, yb, h, routed, ln_out_w, final_norm, tb=128):
    S, D = h.shape
    return pl.pallas_call(
        functools.partial(_combine_body, final_norm),
        grid_spec=pltpu.PrefetchScalarGridSpec(
            num_scalar_prefetch=1,
            grid=(S // tb,),
            in_specs=[pl.BlockSpec(memory_space=pl.ANY),
                      pl.BlockSpec((tb, D), lambda i, pos: (i, 0)),
                      pl.BlockSpec((tb, LANES), lambda i, pos: (i, 0)),
                      pl.BlockSpec((1, D), lambda i, pos: (0, 0))],
            out_specs=pl.BlockSpec((tb, D), lambda i, pos: (i, 0)),
            scratch_shapes=[pltpu.VMEM((2, TOP_K, tb * SLAB_PITCH, LANES), F32), pltpu.SemaphoreType.DMA((2,))]),
        out_shape=jax.ShapeDtypeStruct((S, D), F32),
        compiler_params=_params("arbitrary"),
        name="moe_combine",
    )(pos, yb, h, routed, ln_out_w.reshape(1, D))


def _moe_layer(h, ln_w, wg, bg, we, be, w_gu, w_down, layer, ln_out_w, final_norm):
    T, D = h.shape
    hn, routed, routed_t, counts = _router(h, ln_w, wg, bg, we, be)
    A = T * TOP_K
    n_blocks = -(-A // ROW_BLOCK) + N_EXPERTS
    n_rows = n_blocks * ROW_BLOCK
    counts = counts[0, N_GROUPS:N_GROUPS + N_EXPERTS].astype(jnp.int32)
    padded = (counts + ROW_BLOCK - 1) // ROW_BLOCK * ROW_BLOCK
    pad_end = jnp.cumsum(padded)
    pad_start = pad_end - padded
    expert_idx = routed_t[2:2 + TOP_K].astype(jnp.int32)
    slot = routed_t[4:4 + TOP_K].astype(jnp.int32)
    is_e = expert_idx[..., None] == jnp.arange(N_EXPERTS, dtype=jnp.int32)
    pos = (jnp.sum(jnp.where(is_e, pad_start, 0), axis=-1) + slot).reshape(A)
    block_start = jnp.arange(n_blocks, dtype=jnp.int32) * ROW_BLOCK
    block_e = jnp.minimum(jnp.sum(pad_end[None, :] <= block_start[:, None], axis=1), N_EXPERTS - 1).astype(jnp.int32)
    n_used = (pad_end[-1] // ROW_BLOCK).astype(jnp.int32).reshape(1)
    yb = _experts(n_used, block_e, _row_tokens(pos, n_rows), hn, w_gu, w_down, layer)
    return _combine(pos, yb, h, routed, ln_out_w, final_norm)


def _sigmoid(x):
    return 1.0 / (1.0 + jnp.exp(-x))


def _rwkv_prep_body(h_ref, halo_ref, lnw_ref, mu_ref, w0_ref, w1_ref, w2_ref, a0_ref, a1_ref, a2_ref, g1_ref,
                    g2_ref, xs_ref, lw_ref, a_ref, g_ref):
    i = pl.program_id(0)
    lnw = lnw_ref[...]
    hn = _rms(h_ref[...], lnw)
    ts = hn.shape[0]
    last = _rms(halo_ref[...], lnw)[SUBLANES - 1:SUBLANES, :] * (i > 0).astype(F32)
    row = lax.broadcasted_iota(jnp.int32, (ts, 1), 0)
    prev = jnp.where(row == 0, last, pltpu.roll(hn, 1, axis=0))
    xx = prev - hn
    mix = lambda n: (hn + xx * mu_ref[n:n + 1, :]).astype(BF16)
    n_out = xs_ref.shape[0]
    for n in range(n_out):
        xs_ref[n] = mix(n)
    z = w0_ref[...] + _dot(jnp.tanh(_dot(mix(n_out), w1_ref[...])).astype(BF16), w2_ref[...])
    u = -z
    softplus = jnp.maximum(u, 0.0) + jnp.log1p(jnp.exp(-jnp.abs(u)))
    lw_ref[...] = -jnp.exp(-softplus - 0.5)
    a_ref[...] = _sigmoid(a0_ref[...] + _dot(_dot(mix(n_out + 1), a1_ref[...]).astype(BF16), a2_ref[...]))
    g_ref[...] = _dot(_sigmoid(_dot(mix(n_out + 2), g1_ref[...])).astype(BF16), g2_ref[...])


def _rwkv_prep(h, ln_w, mu, w0, w1, w2, a0, a1, a2, g1, g2, ts=256):
    S, D = h.shape
    n_mix = mu.shape[0]
    n_out = n_mix - 3
    halo_blocks = ts // SUBLANES

    def pad_lora(wa, wb):
        r = wa.shape[1]
        rp = -(-r // LANES) * LANES
        return (jnp.pad(wa, ((0, 0), (0, rp - r))).astype(BF16), jnp.pad(wb, ((0, rp - r), (0, 0))).astype(BF16))

    w1p, w2p = pad_lora(w1, w2)
    a1p, a2p = pad_lora(a1, a2)
    g1p, g2p = pad_lora(g1, g2)
    full = lambda arr: pl.BlockSpec(arr.shape, lambda i: (0,) * arr.ndim)
    vec = pl.BlockSpec((1, D), lambda i: (0, 0))
    row = pl.BlockSpec((ts, D), lambda i: (i, 0))
    return pl.pallas_call(
        _rwkv_prep_body,
        grid=(S // ts,),
        in_specs=[row,
                  pl.BlockSpec((SUBLANES, D), lambda i: (jnp.maximum(i * halo_blocks - 1, 0), 0)),
                  vec, full(mu), vec, full(w1p), full(w2p), vec, full(a1p), full(a2p), full(g1p), full(g2p)],
        out_specs=[pl.BlockSpec((n_out, ts, D), lambda i: (0, i, 0)), row, row, row],
        out_shape=[jax.ShapeDtypeStruct((n_out, S, D), BF16)] + [jax.ShapeDtypeStruct((S, D), F32)] * 3,
        compiler_params=_params("arbitrary"),
        name="rwkv_prep",
    )(h, h, ln_w.reshape(1, D), mu, w0.reshape(1, D), w1p, w2p, a0.reshape(1, D), a1p, a2p, g1p, g2p)


def _scan_chunk(r, k, v, lw, a, g, kkw, kaw, rkw, lnw, lnb, state, c):
    masks, seg_f, seg_b, tri, strict, incl, eye = c
    bf = lambda x: x.astype(BF16)
    stack = lambda xb: jnp.concatenate([xb * m for m in masks], axis=0)
    rows = lambda *xs: jnp.concatenate(xs, axis=0)
    cols = lambda *xs: jnp.concatenate(xs, axis=1)

    kk = k * kkw
    k2 = k * (1.0 + (a - 1.0) * kaw)
    sums = _segsum(rows(kk * kk, r * k2 * rkw), seg_b)
    cw = _cumsum_rows(tri, lw)
    yield
    kk = kk / jnp.maximum(jnp.sqrt(sums[:CHUNK]), 1e-12)
    bonus = sums[CHUNK:] * v
    bv = kk * a
    cl = cw[CHUNK - 1:CHUNK, :]
    at = bf(-kk * jnp.exp(cw - lw))
    rt = bf(r * jnp.exp(cw))
    dinv = jnp.exp(-cw)
    drem = jnp.exp(cl - cw)
    vb = bf(v)
    vs = stack(vb)
    gram = _dot(rows(at, rt), rows(stack(bf(bv * dinv)), stack(bf(k2 * dinv))), NT)
    bk_t = bf(rows(bv * drem, k2 * drem).T)
    decay_rows = jnp.broadcast_to(jnp.exp(cl), (GROUP, GROUP)).T
    yield
    a_ab = jnp.where(strict, gram[:CHUNK, :GROUP], 0.0)
    a_ak = bf(jnp.where(strict, gram[:CHUNK, GROUP:], 0.0))
    a_rb = bf(jnp.where(incl, gram[CHUNK:, :GROUP], 0.0))
    a_rk = bf(jnp.where(incl, gram[CHUNK:, GROUP:], 0.0))
    inv = eye + a_ab
    pw = _dot(bf(a_ab), stack(bf(a_ab)))
    xy = _dot(rows(cols(at, a_ak), cols(rt, a_rk)), rows(bf(state), vs))
    yield
    span = 2
    while span < CHUNK // 2:
        pwb = bf(pw)
        both = _dot(rows(bf(inv), pwb), stack(pwb))
        yield
        inv = inv + both[:CHUNK]
        pw = both[CHUNK:]
        span *= 2
    inv = inv + _dot(bf(inv), stack(bf(pw)))
    yield
    ub = bf(_dot(bf(inv), stack(bf(xy[:CHUNK]))))
    yield
    y = xy[CHUNK:] + _dot(a_rb, stack(ub))
    new_state = state * decay_rows + seg_f * _dot(bk_t, rows(ub, vb))
    yield
    inv_n = 1.0 / RWKV_HEAD_DIM
    mean = _segsum(y, seg_b) * inv_n
    yield
    yc = y - mean
    var = _segsum(yc * yc, seg_b) * inv_n
    yield
    yn = yc * lax.rsqrt(var + GN_EPS) * lnw + lnb
    return ((yn + bonus) * g), new_state


def _run_interleaved(gens):
    results = [None] * len(gens)
    live = list(range(len(gens)))
    while live:
        for i in list(live):
            try:
                next(gens[i])
            except StopIteration as done:
                results[i] = done.value
                live.remove(i)
    return results


def _segsum(x, seg_b):
    return _dot(x.astype(BF16), seg_b)


def _cumsum_rows(tri, x):
    hi = x.astype(BF16)
    r1 = x - hi.astype(F32)
    mid = r1.astype(BF16)
    lo = (r1 - mid.astype(F32)).astype(BF16)
    return _dot(jnp.concatenate([tri, tri, tri], axis=1), jnp.concatenate([hi, mid, lo], axis=0))


def _scan_consts():
    shift = RWKV_HEAD_DIM.bit_length() - 1
    lane_head = lax.broadcasted_iota(jnp.int32, (1, GROUP), 1) >> shift
    masks = [(lane_head == h).astype(F32).astype(BF16) for h in range(GROUP_HEADS)]
    ri = lax.broadcasted_iota(jnp.int32, (GROUP, GROUP), 0) >> shift
    ci = lax.broadcasted_iota(jnp.int32, (GROUP, GROUP), 1) >> shift
    seg_f = (ri == ci).astype(F32)
    tri = (lax.broadcasted_iota(jnp.int32, (CHUNK, CHUNK), 1)
           <= lax.broadcasted_iota(jnp.int32, (CHUNK, CHUNK), 0)).astype(F32).astype(BF16)
    t = lax.broadcasted_iota(jnp.int32, (CHUNK, GROUP), 0)
    s = lax.broadcasted_iota(jnp.int32, (CHUNK, GROUP), 1) & (RWKV_HEAD_DIM - 1)
    return (masks, seg_f, seg_f.astype(BF16), tri, s < t, s <= t, (s == t).astype(F32))


def _scan_body(r_ref, k_ref, v_ref, lw_ref, a_ref, g_ref, kkw_ref, kaw_ref, rkw_ref, lnw_ref, lnb_ref,
               o_ref, state_ref):
    @pl.when(pl.program_id(1) == 0)
    def _():
        state_ref[...] = jnp.zeros_like(state_ref)

    n_groups = o_ref.shape[1] // GROUP

    def chunk_step(j, carry):
        consts = _scan_consts()
        rows = pl.ds(pl.multiple_of(j * CHUNK, CHUNK), CHUNK)
        lanes = [slice(p * GROUP, (p + 1) * GROUP) for p in range(n_groups)]
        args = [(r_ref[rows, ln], k_ref[rows, ln], v_ref[rows, ln], lw_ref[rows, ln], a_ref[rows, ln],
                 g_ref[rows, ln], kkw_ref[:, ln], kaw_ref[:, ln], rkw_ref[:, ln], lnw_ref[:, ln],
                 lnb_ref[:, ln], state_ref[p]) for p, ln in enumerate(lanes)]
        results = _run_interleaved([_scan_chunk(*a, consts) for a in args])
        for p, (ln, (out, new_state)) in enumerate(zip(lanes, results)):
            state_ref[p] = new_state
            o_ref[rows, ln] = out.astype(o_ref.dtype)
        return carry

    lax.fori_loop(0, o_ref.shape[0] // CHUNK, chunk_step, 0)


def _rwkv_scan(rkv, lw, a, g, k_k, k_a, r_k, ln_w, ln_b):
    _, S, D = rkv.shape
    tr, tl = min(SCAN_ROWS, S), SCAN_LANES
    blk = lambda n: pl.BlockSpec((None, tr, tl), lambda p, i: (n, i, p))
    row = pl.BlockSpec((tr, tl), lambda p, i: (i, p))
    vec = pl.BlockSpec((1, tl), lambda p, i: (0, p))
    as_row = lambda w: w.reshape(1, D)
    return pl.pallas_call(
        _scan_body,
        grid=(D // tl, S // tr),
        in_specs=[blk(0), blk(1), blk(2), row, row, row, vec, vec, vec, vec, vec],
        out_specs=row,
        out_shape=jax.ShapeDtypeStruct((S, D), BF16),
        scratch_shapes=[pltpu.VMEM((tl // GROUP, GROUP, GROUP), F32)],
        compiler_params=_params("arbitrary", "arbitrary"),
        name="rwkv_scan",
    )(rkv, rkv, rkv, lw, a, g, as_row(k_k), as_row(k_a), as_row(r_k), as_row(ln_w), as_row(ln_b))


def _rwkv_layer(h, ln_w, mu, w_rkv, w_o, j, w0, w1, w2, a0, a1, a2, g1, g2, k_k, k_a, r_k, ln_gn_w, ln_gn_b):
    xs, lw, a, g = _rwkv_prep(h, ln_w, mu, w0, w1, w2, a0, a1, a2, g1, g2)
    rkv = _rkv_proj(xs, w_rkv, j)
    z = _rwkv_scan(rkv, lw, a, g, k_k, k_a, r_k, ln_gn_w, ln_gn_b)
    return _out_proj(z, w_o, j, h)


def kernel(x, mem, ln_mix_w, ln_xattn_w, ln_mem_w, xattn_wq, xattn_wkv, xattn_wo, ln_ffn_w, router_group_w, router_group_b, router_expert_w, router_expert_b, moe_w_gate_up, moe_w_down, pool_w, pool_scale, rwkv_mu, rwkv_w_rkv, rwkv_w_o, rwkv_w0, rwkv_w1, rwkv_w2, rwkv_a0, rwkv_a1, rwkv_a2, rwkv_g1, rwkv_g2, rwkv_k_k, rwkv_k_a, rwkv_r_k, rwkv_ln_w, rwkv_ln_b, ln_out_w):
    B, S, D = x.shape
    depth = ln_mix_w.shape[0]
    outs = []
    for b in range(B):
        h = x[b]
        for i in range(depth):
            j = i // 2
            if i % 2 == 0:
                h = _pool_layer(h, ln_mix_w[i], pool_w[j], pool_scale[j])
            else:
                h = _rwkv_layer(h, ln_mix_w[i], rwkv_mu[j], rwkv_w_rkv, rwkv_w_o, j, rwkv_w0[j], rwkv_w1[j],
                                rwkv_w2[j], rwkv_a0[j], rwkv_a1[j], rwkv_a2[j], rwkv_g1[j], rwkv_g2[j],
                                rwkv_k_k[j], rwkv_k_a[j], rwkv_r_k[j].reshape(D), rwkv_ln_w[j], rwkv_ln_b[j])
            kv = _mem_kv(mem[b], ln_mem_w[i], xattn_wkv, i)
            h = _xattn_layer(h, ln_xattn_w[i], xattn_wq, kv, xattn_wo, i)
            h = _moe_layer(h, ln_ffn_w[i], router_group_w[i], router_group_b[i], router_expert_w[i],
                           router_expert_b[i], moe_w_gate_up, moe_w_down, i, ln_out_w,
                           final_norm=(i == depth - 1))
        outs.append(h)
    return outs[0][None] if B == 1 else jnp.stack(outs, axis=0)
```

```python
import functools

import jax
import jax.numpy as jnp
from jax import lax
from jax.experimental import pallas as pl
from jax.experimental.pallas import tpu as pltpu

F32 = jnp.float32
BF16 = jnp.bfloat16

D_MODEL = 2048
POOL_WINDOWS = (2, 4, 8, 16)
POOL_GROUP_DIM = D_MODEL // len(POOL_WINDOWS)
MAX_WINDOW = max(POOL_WINDOWS)
RWKV_HEAD_DIM = 64
GN_EPS = 64e-5
XATTN_HEADS = 4
XATTN_HEAD_DIM = D_MODEL // XATTN_HEADS
N_GROUPS = 8
EXPERTS_PER_GROUP = 8
N_EXPERTS = N_GROUPS * EXPERTS_PER_GROUP
TOP_K = 2
D_EXPERT = D_MODEL // 8
ROW_BLOCK = 128
RMS_EPS = 1e-6

LANES = 128
SUBLANES = 8
VMEM_LIMIT = 56 * 1024 * 1024

CHUNK = 64
GROUP_HEADS = 4
GROUP = GROUP_HEADS * RWKV_HEAD_DIM
SLAB = D_MODEL // LANES
SLAB_PITCH = SLAB + SUBLANES
ISSUE_UNROLL = 8
WEIGHT_STAGE_ROWS = 256
SCAN_ROWS = 256
SCAN_LANES = 2048

NN = (((1,), (0,)), ((), ()))
NT = (((1,), (1,)), ((), ()))


def _params(*sem):
    return pltpu.CompilerParams(dimension_semantics=sem, vmem_limit_bytes=VMEM_LIMIT)


def _rms(x, w):
    return x * lax.rsqrt(jnp.mean(x * x, axis=-1, keepdims=True) + RMS_EPS) * w


def _dot(a, b, dims=NN):
    return lax.dot_general(a, b, dims, preferred_element_type=F32)


def _split(x):
    hi = x.astype(BF16)
    return hi, (x - hi.astype(F32)).astype(BF16)


def _store_slabs(ref, x):
    for j in range(SLAB):
        ref[pl.ds(j, x.shape[0], stride=SLAB), :] = x[:, j * LANES:(j + 1) * LANES]


def _load_slabs(ref):
    n = ref.shape[0] // SLAB_PITCH
    return jnp.concatenate([ref[pl.ds(j, n, stride=SLAB_PITCH), :] for j in range(SLAB)], axis=-1)


def _slab(ref, row, pitch=SLAB):
    return ref.at[pl.ds(pl.multiple_of(row * pitch, SUBLANES), SLAB), :]


def _rkv_proj_body(x_ref, w_ref, o_ref, wb_ref):
    @pl.when(pl.program_id(2) == 0)
    def _():
        wb_ref[...] = w_ref[...].astype(BF16)

    o_ref[...] = _dot(x_ref[...], wb_ref[...])


def _rkv_proj(xs, w_rkv, j, tm=512, tn=1024):
    _, M, K = xs.shape
    B, N = w_rkv.shape[1], w_rkv.shape[3]
    return pl.pallas_call(
        _rkv_proj_body,
        grid=(B, N // tn, M // tm),
        in_specs=[pl.BlockSpec((None, tm, K), lambda b, n, i: (b, i, 0)),
                  pl.BlockSpec((None, None, K, tn), lambda b, n, i: (j, b, 0, n))],
        out_specs=pl.BlockSpec((None, tm, tn), lambda b, n, i: (b, i, n)),
        out_shape=jax.ShapeDtypeStruct((B, M, N), F32),
        scratch_shapes=[pltpu.VMEM((K, tn), BF16)],
        compiler_params=_params("arbitrary", "arbitrary", "arbitrary"),
        name="rwkv_rkv_proj",
    )(xs, w_rkv)


def _out_proj_body(x_ref, w_ref, r_ref, o_ref, wb_ref):
    @pl.when(pl.program_id(1) == 0)
    def _():
        wb_ref[...] = w_ref[...].astype(BF16)

    o_ref[...] = r_ref[...] + _dot(x_ref[...], wb_ref[...])


def _out_proj(x, w_o, j, res, tm=512, tn=1024):
    M, K = x.shape
    N = w_o.shape[2]
    return pl.pallas_call(
        _out_proj_body,
        grid=(N // tn, M // tm),
        in_specs=[pl.BlockSpec((tm, K), lambda n, i: (i, 0)),
                  pl.BlockSpec((None, K, tn), lambda n, i: (j, 0, n)),
                  pl.BlockSpec((tm, tn), lambda n, i: (i, n))],
        out_specs=pl.BlockSpec((tm, tn), lambda n, i: (i, n)),
        out_shape=jax.ShapeDtypeStruct((M, N), F32),
        scratch_shapes=[pltpu.VMEM((K, tn), BF16)],
        compiler_params=_params("arbitrary", "arbitrary"),
        name="rwkv_out_proj",
    )(x, w_o, res)


def _pool_body(h_ref, halo_ref, lnw_ref, pw_ref, ps_ref, o_ref):
    i = pl.program_id(0)
    ts = h_ref.shape[0]
    h = h_ref[...]
    lnw = lnw_ref[...]
    x = _rms(h, lnw)
    xh = _rms(halo_ref[...], lnw) * (i > 0).astype(F32)
    xe = jnp.concatenate([xh, x], axis=0)
    t = i * ts + lax.broadcasted_iota(jnp.int32, (ts, 1), 0)
    outs = []
    for g, win in enumerate(POOL_WINDOWS):
        sl = slice(g * POOL_GROUP_DIM, (g + 1) * POOL_GROUP_DIM)
        acc = xe[:, sl]
        span = 1
        while span < win:
            acc = acc + pltpu.roll(acc, span, axis=0)
            span *= 2
        cnt = jnp.minimum(t + 1, win).astype(F32)
        pooled = acc[MAX_WINDOW:, :] / cnt - x[:, sl]
        outs.append(_dot(pooled.astype(BF16), pw_ref[g]))
    y = jnp.concatenate(outs, axis=-1)
    o_ref[...] = h + y * ps_ref[...]


def _pool_layer(h, ln_w, pool_w, pool_scale, ts=512):
    S, D = h.shape
    G = pool_w.shape[0]
    halo_blocks = ts // MAX_WINDOW
    return pl.pallas_call(
        _pool_body,
        grid=(S // ts,),
        in_specs=[pl.BlockSpec((ts, D), lambda i: (i, 0)),
                  pl.BlockSpec((MAX_WINDOW, D), lambda i: (jnp.maximum(i * halo_blocks - 1, 0), 0)),
                  pl.BlockSpec((1, D), lambda i: (0, 0)),
                  pl.BlockSpec((G, POOL_GROUP_DIM, POOL_GROUP_DIM), lambda i: (0, 0, 0)),
                  pl.BlockSpec((1, D), lambda i: (0, 0))],
        out_specs=pl.BlockSpec((ts, D), lambda i: (i, 0)),
        out_shape=jax.ShapeDtypeStruct((S, D), F32),
        compiler_params=_params("arbitrary"),
        name="pool_layer",
    )(h, h, ln_w.reshape(1, D), pool_w.astype(BF16), pool_scale.reshape(1, D))


def _memkv_body(mem_ref, lnw_ref, w_ref, o_ref):
    mn = _rms(mem_ref[...], lnw_ref[...])
    o_ref[...] = _dot(mn.astype(BF16), w_ref[...].astype(BF16)).astype(o_ref.dtype)


def _mem_kv(mem, ln_w, wkv, layer, tn=1024):
    M, D = mem.shape
    N = wkv.shape[2]
    return pl.pallas_call(
        _memkv_body,
        grid=(N // tn,),
        in_specs=[pl.BlockSpec((M, D), lambda n: (0, 0)),
                  pl.BlockSpec((1, D), lambda n: (0, 0)),
                  pl.BlockSpec((None, D, tn), lambda n: (layer, 0, n))],
        out_specs=pl.BlockSpec((M, tn), lambda n: (0, n)),
        out_shape=jax.ShapeDtypeStruct((M, N), BF16),
        compiler_params=_params("arbitrary"),
        name="xattn_mem_kv",
    )(mem, ln_w.reshape(1, D), wkv)


def _load_weight_bf16(w_hbm, dst, stage, sem):
    rows = stage.shape[1]
    n_chunks = w_hbm.shape[0] // rows

    def chunk_copy(c):
        return pltpu.make_async_copy(w_hbm.at[pl.ds(c * rows, rows), :], stage.at[c % 2], sem.at[c % 2])

    chunk_copy(0).start()
    for c in range(n_chunks):
        if c + 1 < n_chunks:
            chunk_copy(c + 1).start()
        chunk_copy(c).wait()
        dst[pl.ds(c * rows, rows), :] = stage[c % 2].astype(BF16)


def _xattn_body(layer, h_ref, lnw_ref, k_ref, v_ref, wq_hbm, wo_hbm, o_ref, wq_ref, wo_ref, stage, sem):
    @pl.when(pl.program_id(0) == 0)
    def _():
        _load_weight_bf16(wq_hbm.at[layer], wq_ref, stage, sem)
        _load_weight_bf16(wo_hbm.at[layer], wo_ref, stage, sem)

    h = h_ref[...]
    hn = _rms(h, lnw_ref[...]).astype(BF16)
    q = _dot(hn, wq_ref[...])
    scale = XATTN_HEAD_DIM ** -0.5
    heads = []
    for hd in range(XATTN_HEADS):
        sl = slice(hd * XATTN_HEAD_DIM, (hd + 1) * XATTN_HEAD_DIM)
        s = _dot(q[:, sl].astype(BF16), k_ref[:, sl], NT) * scale
        s = s - jnp.max(s, axis=-1, keepdims=True)
        e = jnp.exp(s)
        p = e / jnp.sum(e, axis=-1, keepdims=True)
        heads.append(_dot(p.astype(BF16), v_ref[:, sl]))
    o = jnp.concatenate(heads, axis=-1).astype(BF16)
    o_ref[...] = h + _dot(o, wo_ref[...])


def _xattn_layer(h, ln_w, wq, kv, wo, layer, ts=256):
    S, D = h.shape
    M = kv.shape[0]
    return pl.pallas_call(
        functools.partial(_xattn_body, layer),
        grid=(S // ts,),
        in_specs=[pl.BlockSpec((ts, D), lambda i: (i, 0)),
                  pl.BlockSpec((1, D), lambda i: (0, 0)),
                  pl.BlockSpec((M, D), lambda i: (0, 0)),
                  pl.BlockSpec((M, D), lambda i: (0, 1)),
                  pl.BlockSpec(memory_space=pl.ANY),
                  pl.BlockSpec(memory_space=pl.ANY)],
        out_specs=pl.BlockSpec((ts, D), lambda i: (i, 0)),
        out_shape=jax.ShapeDtypeStruct((S, D), F32),
        scratch_shapes=[pltpu.VMEM((D, D), BF16), pltpu.VMEM((D, D), BF16),
                        pltpu.VMEM((2, WEIGHT_STAGE_ROWS, D), F32), pltpu.SemaphoreType.DMA((2,))],
        compiler_params=_params("arbitrary"),
        name="xattn_layer",
    )(h, ln_w.reshape(1, D), kv, kv, wq, wo)


def _router_body(h_ref, lnw_ref, whi_ref, wlo_ref, b_ref, hn_ref, r_ref, rt_ref, cnt_ref):
    i = pl.program_id(0)

    @pl.when(i == 0)
    def _():
        cnt_ref[...] = jnp.zeros_like(cnt_ref)

    hn = _rms(h_ref[...], lnw_ref[...])
    _store_slabs(hn_ref, hn)
    xh, xl = _split(hn)
    whi = whi_ref[...]
    lg = _dot(xh, whi) + (_dot(xh, wlo_ref[...]) + _dot(xl, whi)) + b_ref[...]
    ts = lg.shape[0]
    lane = lax.broadcasted_iota(jnp.int32, (ts, LANES), 1)
    lanef = lane.astype(F32)
    neg = jnp.float32(-jnp.inf)
    gl = jnp.where(lane < N_GROUPS, lg, neg)
    gmax = jnp.max(gl, axis=-1, keepdims=True)
    pg_top = 1.0 / jnp.sum(jnp.exp(gl - gmax), axis=-1, keepdims=True)
    g_idx = jnp.min(jnp.where(gl == gmax, lanef, float(LANES)), axis=-1, keepdims=True)
    lo_lane = g_idx * EXPERTS_PER_GROUP + N_GROUPS
    el = jnp.where(lanef >= lo_lane, jnp.where(lanef < lo_lane + EXPERTS_PER_GROUP, lg, neg), neg)
    m1 = jnp.max(el, axis=-1, keepdims=True)
    i1 = jnp.min(jnp.where(el == m1, lanef, float(LANES)), axis=-1, keepdims=True)
    el2 = jnp.where(lanef == i1, neg, el)
    m2 = jnp.max(el2, axis=-1, keepdims=True)
    i2 = jnp.min(jnp.where(el2 == m2, lanef, float(LANES)), axis=-1, keepdims=True)
    e2 = jnp.exp(m2 - m1)
    g1 = pg_top / (1.0 + e2)
    g2 = pg_top * e2 / (1.0 + e2)
    hit1 = (lanef == i1).astype(F32)
    hit2 = (lanef == i2).astype(F32)
    hits = hit1 + hit2
    earlier = (lax.broadcasted_iota(jnp.int32, (ts, ts), 1)
               < lax.broadcasted_iota(jnp.int32, (ts, ts), 0)).astype(F32).astype(BF16)
    before = _dot(earlier, hits.astype(BF16)) + cnt_ref[...]
    r1 = jnp.sum(before * hit1, axis=-1, keepdims=True)
    r2 = jnp.sum(before * hit2, axis=-1, keepdims=True)
    cnt_ref[...] += jnp.sum(hits, axis=0, keepdims=True)
    out = jnp.where(lane == 0, g1, 0.0)
    out = jnp.where(lane == 1, g2, out)
    out = jnp.where(lane == 2, i1 - N_GROUPS, out)
    out = jnp.where(lane == 3, i2 - N_GROUPS, out)
    out = jnp.where(lane == 4, r1, out)
    out = jnp.where(lane == 5, r2, out)
    r_ref[...] = out
    rt_ref[...] = out.T[:SUBLANES, :]


def _router(h, ln_w, wg, bg, we, be, ts=512):
    S, D = h.shape
    unused = LANES - N_GROUPS - N_EXPERTS
    w = jnp.concatenate([wg, we, jnp.zeros((D, unused), F32)], axis=1)
    b = jnp.concatenate([bg, be, jnp.zeros((unused,), F32)]).reshape(1, LANES)
    whi, wlo = _split(w)
    return pl.pallas_call(
        _router_body,
        grid=(S // ts,),
        in_specs=[pl.BlockSpec((ts, D), lambda i: (i, 0)),
                  pl.BlockSpec((1, D), lambda i: (0, 0)),
                  pl.BlockSpec((D, LANES), lambda i: (0, 0)),
                  pl.BlockSpec((D, LANES), lambda i: (0, 0)),
                  pl.BlockSpec((1, LANES), lambda i: (0, 0))],
        out_specs=[pl.BlockSpec((ts * SLAB, LANES), lambda i: (i, 0)),
                   pl.BlockSpec((ts, LANES), lambda i: (i, 0)),
                   pl.BlockSpec((SUBLANES, ts), lambda i: (0, i)),
                   pl.BlockSpec((1, LANES), lambda i: (0, 0))],
        out_shape=[jax.ShapeDtypeStruct((S * SLAB, LANES), F32), jax.ShapeDtypeStruct((S, LANES), F32),
                   jax.ShapeDtypeStruct((SUBLANES, S), F32), jax.ShapeDtypeStruct((1, LANES), F32)],
        compiler_params=_params("arbitrary"),
        name="moe_router",
    )(h, ln_w.reshape(1, D), whi, wlo, b)


def _row_tokens_body(pos_ref, tok_ref):
    n_tok = pos_ref.shape[0] // TOP_K

    def clear(j, carry):
        tok_ref[j] = 0
        return carry
    lax.fori_loop(0, tok_ref.shape[0], clear, 0, unroll=ISSUE_UNROLL)

    def put(t, carry):
        for k in range(TOP_K):
            tok_ref[pos_ref[k * n_tok + t]] = t
        return carry
    lax.fori_loop(0, n_tok, put, 0, unroll=ISSUE_UNROLL)


def _row_tokens(pos, n_rows):
    return pl.pallas_call(
        _row_tokens_body,
        grid_spec=pltpu.PrefetchScalarGridSpec(
            num_scalar_prefetch=1,
            grid=(1,),
            in_specs=[],
            out_specs=pl.BlockSpec(memory_space=pltpu.SMEM)),
        out_shape=jax.ShapeDtypeStruct((n_rows,), jnp.int32),
        compiler_params=_params("arbitrary"),
        name="moe_row_tokens",
    )(pos)


def _experts_body(layer, nblk_ref, be_ref, fresh_ref, wslot_ref, next_e_ref, tok_ref, hn_hbm, wgu_hbm, wd_hbm,
                  o_ref, xbuf, wgu_f, wd_f, wgu_b, wd_b, xsem, wsem):
    i = pl.program_id(0)
    n_used = nblk_ref[0]
    n_xbuf = xbuf.shape[0]
    xslot = i % n_xbuf

    def start_rows(block):
        slot = block % n_xbuf

        def issue(r, carry):
            pltpu.make_async_copy(_slab(hn_hbm, tok_ref[block * ROW_BLOCK + r]), _slab(xbuf.at[slot], r, SLAB_PITCH),
                                  xsem.at[slot]).start()
            return carry
        lax.fori_loop(0, ROW_BLOCK, issue, 0, unroll=ISSUE_UNROLL)

    def weight_copies(e, slot):
        return (pltpu.make_async_copy(wgu_hbm.at[layer, e], wgu_f.at[slot], wsem.at[0, slot]),
                pltpu.make_async_copy(wd_hbm.at[layer, e], wd_f.at[slot], wsem.at[1, slot]))

    @pl.when(i == 0)
    def _():
        for cp in weight_copies(be_ref[0], 0):
            cp.start(priority=1)
        for b in range(n_xbuf - 1):
            @pl.when(b < n_used)
            def _():
                start_rows(b)

    @pl.when(i + n_xbuf - 1 < n_used)
    def _():
        start_rows(i + n_xbuf - 1)

    @pl.when(jnp.logical_and(i < n_used, fresh_ref[i] == 1))
    def _():
        slot = wslot_ref[i]
        for cp in weight_copies(be_ref[i], slot):
            cp.wait()
        wgu_b[...] = wgu_f[slot].astype(BF16)
        wd_b[...] = wd_f[slot].astype(BF16)

        @pl.when(next_e_ref[i] >= 0)
        def _():
            for cp in weight_copies(next_e_ref[i], 1 - slot):
                cp.start(priority=1)

    @pl.when(i < n_used)
    def _():
        pltpu.make_async_copy(hn_hbm.at[pl.ds(0, ROW_BLOCK * SLAB), :],
                              xbuf.at[xslot, pl.ds(0, ROW_BLOCK * SLAB), :], xsem.at[xslot]).wait()
        gu = _dot(_load_slabs(xbuf.at[xslot]).astype(BF16), wgu_b[...])
        gg = gu[:, :D_EXPERT]
        uu = gu[:, D_EXPERT:]
        act = (gg * (1.0 / (1.0 + jnp.exp(-gg))) * uu).astype(BF16)
        _store_slabs(o_ref, _dot(act, wd_b[...]))

    @pl.when(i >= n_used)
    def _():
        o_ref[...] = jnp.zeros_like(o_ref)


def _experts(n_used, block_e, row_tok, hn, w_gu, w_down, layer):
    n_rows = row_tok.shape[0]
    n_blocks = n_rows // ROW_BLOCK
    D = w_gu.shape[2]
    F2 = w_gu.shape[3]
    idx = jnp.arange(n_blocks, dtype=jnp.int32)
    used = idx < n_used[0]
    fresh = jnp.logical_and(used, jnp.concatenate([jnp.ones((1,), bool), block_e[1:] != block_e[:-1]]))
    run = jnp.cumsum(fresh.astype(jnp.int32)) - 1
    run_start = jnp.where(fresh, idx, n_blocks)
    next_start = lax.cummin(jnp.concatenate([run_start[1:], jnp.full((1,), n_blocks, jnp.int32)]), reverse=True)
    next_e = jnp.where(next_start < n_blocks, block_e[jnp.minimum(next_start, n_blocks - 1)], -1)
    return pl.pallas_call(
        functools.partial(_experts_body, layer),
        grid_spec=pltpu.PrefetchScalarGridSpec(
            num_scalar_prefetch=6,
            grid=(n_blocks,),
            in_specs=[pl.BlockSpec(memory_space=pl.ANY),
                      pl.BlockSpec(memory_space=pl.ANY),
                      pl.BlockSpec(memory_space=pl.ANY)],
            out_specs=pl.BlockSpec((ROW_BLOCK * SLAB, LANES), lambda i, *_: (i, 0)),
            scratch_shapes=[pltpu.VMEM((3, ROW_BLOCK * SLAB_PITCH, LANES), F32),
                            pltpu.VMEM((2, D, F2), F32), pltpu.VMEM((2, F2 // 2, D), F32),
                            pltpu.VMEM((D, F2), BF16), pltpu.VMEM((F2 // 2, D), BF16),
                            pltpu.SemaphoreType.DMA((3,)), pltpu.SemaphoreType.DMA((2, 2))]),
        out_shape=jax.ShapeDtypeStruct((n_rows * SLAB, LANES), F32),
        compiler_params=_params("arbitrary"),
        name="moe_experts",
    )(n_used, block_e, fresh.astype(jnp.int32), (run % 2).astype(jnp.int32), next_e.astype(jnp.int32), row_tok,
      hn, w_gu, w_down)


def _start_pair_gather(pos_ref, first_tok, n_tok, yb_hbm, ybuf, slot, sem, tb):
    def issue(r, carry):
        for k in range(TOP_K):
            pltpu.make_async_copy(_slab(yb_hbm, pos_ref[k * n_tok + first_tok + r]), _slab(ybuf.at[slot, k], r, SLAB_PITCH),
                                  sem.at[slot]).start(priority=k % 2)
        return carry
    lax.fori_loop(0, tb, issue, 0, unroll=ISSUE_UNROLL)


def _wait_pair_gather(yb_hbm, ybuf, slot, sem, tb):
    for k in range(TOP_K):
        pltpu.make_async_copy(yb_hbm.at[pl.ds(0, tb * SLAB), :], ybuf.at[slot, k, pl.ds(0, tb * SLAB), :],
                              sem.at[slot]).wait()


def _combine_body(final_norm, pos_ref, yb_hbm, h_ref, r_ref, lnw_ref, o_ref, ybuf, sem):
    i = pl.program_id(0)
    n = pl.num_programs(0)
    tb = h_ref.shape[0]
    slot = i % 2

    @pl.when(i == 0)
    def _():
        _start_pair_gather(pos_ref, 0, n * tb, yb_hbm, ybuf, 0, sem, tb)

    @pl.when(i + 1 < n)
    def _():
        _start_pair_gather(pos_ref, (i + 1) * tb, n * tb, yb_hbm, ybuf, 1 - slot, sem, tb)

    _wait_pair_gather(yb_hbm, ybuf, slot, sem, tb)
    out = h_ref[...]
    for k in range(TOP_K):
        out = out + _load_slabs(ybuf.at[slot, k]) * r_ref[:, k:k + 1]
    if final_norm:
        out = _rms(out, lnw_ref[...])
    o_ref[...] = out


def _combine(pos, yb, h, routed, ln_out_w, final_norm, tb=128):
    S, D = h.shape
    return pl.pallas_call(
        functools.partial(_combine_body, final_norm),
        grid_spec=pltpu.PrefetchScalarGridSpec(
            num_scalar_prefetch=1,
            grid=(S // tb,),
            in_specs=[pl.BlockSpec(memory_space=pl.ANY),
                      pl.BlockSpec((tb, D), lambda i, pos: (i, 0)),
                      pl.BlockSpec((tb, LANES), lambda i, pos: (i, 0)),
                      pl.BlockSpec((1, D), lambda i, pos: (0, 0))],
            out_specs=pl.BlockSpec((tb, D), lambda i, pos: (i, 0)),
            scratch_shapes=[pltpu.VMEM((2, TOP_K, tb * SLAB_PITCH, LANES), F32), pltpu.SemaphoreType.DMA((2,))]),
        out_shape=jax.ShapeDtypeStruct((S, D), F32),
        compiler_params=_params("arbitrary"),
        name="moe_combine",
    )(pos, yb, h, routed, ln_out_w.reshape(1, D))


def _moe_layer(h, ln_w, wg, bg, we, be, w_gu, w_down, layer, ln_out_w, final_norm):
    T, D = h.shape
    hn, routed, routed_t, counts = _router(h, ln_w, wg, bg, we, be)
    A = T * TOP_K
    n_blocks = -(-A // ROW_BLOCK) + N_EXPERTS
    n_rows = n_blocks * ROW_BLOCK
    counts = counts[0, N_GROUPS:N_GROUPS + N_EXPERTS].astype(jnp.int32)
    padded = (counts + ROW_BLOCK - 1) // ROW_BLOCK * ROW_BLOCK
    pad_end = jnp.cumsum(padded)
    pad_start = pad_end - padded
    expert_idx = routed_t[2:2 + TOP_K].astype(jnp.int32)
    slot = routed_t[4:4 + TOP_K].astype(jnp.int32)
    is_e = expert_idx[..., None] == jnp.arange(N_EXPERTS, dtype=jnp.int32)
    pos = (jnp.sum(jnp.where(is_e, pad_start, 0), axis=-1) + slot).reshape(A)
    block_start = jnp.arange(n_blocks, dtype=jnp.int32) * ROW_BLOCK
    block_e = jnp.minimum(jnp.sum(pad_end[None, :] <= block_start[:, None], axis=1), N_EXPERTS - 1).astype(jnp.int32)
    n_used = (pad_end[-1] // ROW_BLOCK).astype(jnp.int32).reshape(1)
    yb = _experts(n_used, block_e, _row_tokens(pos, n_rows), hn, w_gu, w_down, layer)
    return _combine(pos, yb, h, routed, ln_out_w, final_norm)


def _sigmoid(x):
    return 1.0 / (1.0 + jnp.exp(-x))


def _rwkv_prep_body(h_ref, halo_ref, lnw_ref, mu_ref, w0_ref, w1_ref, w2_ref, a0_ref, a1_ref, a2_ref, g1_ref,
                    g2_ref, xs_ref, lw_ref, a_ref, g_ref):
    i = pl.program_id(0)
    lnw = lnw_ref[...]
    hn = _rms(h_ref[...], lnw)
    ts = hn.shape[0]
    last = _rms(halo_ref[...], lnw)[SUBLANES - 1:SUBLANES, :] * (i > 0).astype(F32)
    row = lax.broadcasted_iota(jnp.int32, (ts, 1), 0)
    prev = jnp.where(row == 0, last, pltpu.roll(hn, 1, axis=0))
    xx = prev - hn
    mix = lambda n: (hn + xx * mu_ref[n:n + 1, :]).astype(BF16)
    n_out = xs_ref.shape[0]
    for n in range(n_out):
        xs_ref[n] = mix(n)
    z = w0_ref[...] + _dot(jnp.tanh(_dot(mix(n_out), w1_ref[...])).astype(BF16), w2_ref[...])
    u = -z
    softplus = jnp.maximum(u, 0.0) + jnp.log1p(jnp.exp(-jnp.abs(u)))
    lw_ref[...] = -jnp.exp(-softplus - 0.5)
    a_ref[...] = _sigmoid(a0_ref[...] + _dot(_dot(mix(n_out + 1), a1_ref[...]).astype(BF16), a2_ref[...]))
    g_ref[...] = _dot(_sigmoid(_dot(mix(n_out + 2), g1_ref[...])).astype(BF16), g2_ref[...])


def _rwkv_prep(h, ln_w, mu, w0, w1, w2, a0, a1, a2, g1, g2, ts=256):
    S, D = h.shape
    n_mix = mu.shape[0]
    n_out = n_mix - 3
    halo_blocks = ts // SUBLANES

    def pad_lora(wa, wb):
        r = wa.shape[1]
        rp = -(-r // LANES) * LANES
        return (jnp.pad(wa, ((0, 0), (0, rp - r))).astype(BF16), jnp.pad(wb, ((0, rp - r), (0, 0))).astype(BF16))

    w1p, w2p = pad_lora(w1, w2)
    a1p, a2p = pad_lora(a1, a2)
    g1p, g2p = pad_lora(g1, g2)
    full = lambda arr: pl.BlockSpec(arr.shape, lambda i: (0,) * arr.ndim)
    vec = pl.BlockSpec((1, D), lambda i: (0, 0))
    row = pl.BlockSpec((ts, D), lambda i: (i, 0))
    return pl.pallas_call(
        _rwkv_prep_body,
        grid=(S // ts,),
        in_specs=[row,
                  pl.BlockSpec((SUBLANES, D), lambda i: (jnp.maximum(i * halo_blocks - 1, 0), 0)),
                  vec, full(mu), vec, full(w1p), full(w2p), vec, full(a1p), full(a2p), full(g1p), full(g2p)],
        out_specs=[pl.BlockSpec((n_out, ts, D), lambda i: (0, i, 0)), row, row, row],
        out_shape=[jax.ShapeDtypeStruct((n_out, S, D), BF16)] + [jax.ShapeDtypeStruct((S, D), F32)] * 3,
        compiler_params=_params("arbitrary"),
        name="rwkv_prep",
    )(h, h, ln_w.reshape(1, D), mu, w0.reshape(1, D), w1p, w2p, a0.reshape(1, D), a1p, a2p, g1p, g2p)


def _scan_chunk(r, k, v, lw, a, g, kkw, kaw, rkw, lnw, lnb, state, c):
    masks, seg_f, seg_b, tri, strict, incl, eye = c
    bf = lambda x: x.astype(BF16)
    stack = lambda xb: jnp.concatenate([xb * m for m in masks], axis=0)
    rows = lambda *xs: jnp.concatenate(xs, axis=0)
    cols = lambda *xs: jnp.concatenate(xs, axis=1)

    kk = k * kkw
    k2 = k * (1.0 + (a - 1.0) * kaw)
    sums = _segsum(rows(kk * kk, r * k2 * rkw), seg_b)
    cw = _cumsum_rows(tri, lw)
    yield
    kk = kk / jnp.maximum(jnp.sqrt(sums[:CHUNK]), 1e-12)
    bonus = sums[CHUNK:] * v
    bv = kk * a
    cl = cw[CHUNK - 1:CHUNK, :]
    at = bf(-kk * jnp.exp(cw - lw))
    rt = bf(r * jnp.exp(cw))
    dinv = jnp.exp(-cw)
    drem = jnp.exp(cl - cw)
    vb = bf(v)
    vs = stack(vb)
    gram = _dot(rows(at, rt), rows(stack(bf(bv * dinv)), stack(bf(k2 * dinv))), NT)
    bk_t = bf(rows(bv * drem, k2 * drem).T)
    decay_rows = jnp.broadcast_to(jnp.exp(cl), (GROUP, GROUP)).T
    yield
    a_ab = jnp.where(strict, gram[:CHUNK, :GROUP], 0.0)
    a_ak = bf(jnp.where(strict, gram[:CHUNK, GROUP:], 0.0))
    a_rb = bf(jnp.where(incl, gram[CHUNK:, :GROUP], 0.0))
    a_rk = bf(jnp.where(incl, gram[CHUNK:, GROUP:], 0.0))
    inv = eye + a_ab
    pw = _dot(bf(a_ab), stack(bf(a_ab)))
    xy = _dot(rows(cols(at, a_ak), cols(rt, a_rk)), rows(bf(state), vs))
    yield
    span = 2
    while span < CHUNK // 2:
        pwb = bf(pw)
        both = _dot(rows(bf(inv), pwb), stack(pwb))
        yield
        inv = inv + both[:CHUNK]
        pw = both[CHUNK:]
        span *= 2
    inv = inv + _dot(bf(inv), stack(bf(pw)))
    yield
    ub = bf(_dot(bf(inv), stack(bf(xy[:CHUNK]))))
    yield
    y = xy[CHUNK:] + _dot(a_rb, stack(ub))
    new_state = state * decay_rows + seg_f * _dot(bk_t, rows(ub, vb))
    yield
    inv_n = 1.0 / RWKV_HEAD_DIM
    mean = _segsum(y, seg_b) * inv_n
    yield
    yc = y - mean
    var = _segsum(yc * yc, seg_b) * inv_n
    yield
    yn = yc * lax.rsqrt(var + GN_EPS) * lnw + lnb
    return ((yn + bonus) * g), new_state


def _run_interleaved(gens):
    results = [None] * len(gens)
    live = list(range(len(gens)))
    while live:
        for i in list(live):
            try:
                next(gens[i])
            except StopIteration as done:
                results[i] = done.value
                live.remove(i)
    return results


def _segsum(x, seg_b):
    return _dot(x.astype(BF16), seg_b)


def _cumsum_rows(tri, x):
    hi = x.astype(BF16)
    r1 = x - hi.astype(F32)
    mid = r1.astype(BF16)
    lo = (r1 - mid.astype(F32)).astype(BF16)
    return _dot(jnp.concatenate([tri, tri, tri], axis=1), jnp.concatenate([hi, mid, lo], axis=0))


def _scan_consts():
    shift = RWKV_HEAD_DIM.bit_length() - 1
    lane_head = lax.broadcasted_iota(jnp.int32, (1, GROUP), 1) >> shift
    masks = [(lane_head == h).astype(F32).astype(BF16) for h in range(GROUP_HEADS)]
    ri = lax.broadcasted_iota(jnp.int32, (GROUP, GROUP), 0) >> shift
    ci = lax.broadcasted_iota(jnp.int32, (GROUP, GROUP), 1) >> shift
    seg_f = (ri == ci).astype(F32)
    tri = (lax.broadcasted_iota(jnp.int32, (CHUNK, CHUNK), 1)
           <= lax.broadcasted_iota(jnp.int32, (CHUNK, CHUNK), 0)).astype(F32).astype(BF16)
    t = lax.broadcasted_iota(jnp.int32, (CHUNK, GROUP), 0)
    s = lax.broadcasted_iota(jnp.int32, (CHUNK, GROUP), 1) & (RWKV_HEAD_DIM - 1)
    return (masks, seg_f, seg_f.astype(BF16), tri, s < t, s <= t, (s == t).astype(F32))


def _scan_body(r_ref, k_ref, v_ref, lw_ref, a_ref, g_ref, kkw_ref, kaw_ref, rkw_ref, lnw_ref, lnb_ref,
               o_ref, state_ref):
    @pl.when(pl.program_id(1) == 0)
    def _():
        state_ref[...] = jnp.zeros_like(state_ref)

    n_groups = o_ref.shape[1] // GROUP

    def chunk_step(j, carry):
        consts = _scan_consts()
        rows = pl.ds(pl.multiple_of(j * CHUNK, CHUNK), CHUNK)
        lanes = [slice(p * GROUP, (p + 1) * GROUP) for p in range(n_groups)]
        args = [(r_ref[rows, ln], k_ref[rows, ln], v_ref[rows, ln], lw_ref[rows, ln], a_ref[rows, ln],
                 g_ref[rows, ln], kkw_ref[:, ln], kaw_ref[:, ln], rkw_ref[:, ln], lnw_ref[:, ln],
                 lnb_ref[:, ln], state_ref[p]) for p, ln in enumerate(lanes)]
        results = _run_interleaved([_scan_chunk(*a, consts) for a in args])
        for p, (ln, (out, new_state)) in enumerate(zip(lanes, results)):
            state_ref[p] = new_state
            o_ref[rows, ln] = out.astype(o_ref.dtype)
        return carry

    lax.fori_loop(0, o_ref.shape[0] // CHUNK, chunk_step, 0)


def _rwkv_scan(rkv, lw, a, g, k_k, k_a, r_k, ln_w, ln_b):
    _, S, D = rkv.shape
    tr, tl = min(SCAN_ROWS, S), SCAN_LANES
    blk = lambda n: pl.BlockSpec((None, tr, tl), lambda p, i: (n, i, p))
    row = pl.BlockSpec((tr, tl), lambda p, i: (i, p))
    vec = pl.BlockSpec((1, tl), lambda p, i: (0, p))
    as_row = lambda w: w.reshape(1, D)
    return pl.pallas_call(
        _scan_body,
        grid=(D // tl, S // tr),
        in_specs=[blk(0), blk(1), blk(2), row, row, row, vec, vec, vec, vec, vec],
        out_specs=row,
        out_shape=jax.ShapeDtypeStruct((S, D), BF16),
        scratch_shapes=[pltpu.VMEM((tl // GROUP, GROUP, GROUP), F32)],
        compiler_params=_params("arbitrary", "arbitrary"),
        name="rwkv_scan",
    )(rkv, rkv, rkv, lw, a, g, as_row(k_k), as_row(k_a), as_row(r_k), as_row(ln_w), as_row(ln_b))


def _rwkv_layer(h, ln_w, mu, w_rkv, w_o, j, w0, w1, w2, a0, a1, a2, g1, g2, k_k, k_a, r_k, ln_gn_w, ln_gn_b):
    xs, lw, a, g = _rwkv_prep(h, ln_w, mu, w0, w1, w2, a0, a1, a2, g1, g2)
    rkv = _rkv_proj(xs, w_rkv, j)
    z = _rwkv_scan(rkv, lw, a, g, k_k, k_a, r_k, ln_gn_w, ln_gn_b)
    return _out_proj(z, w_o, j, h)


def kernel(x, mem, ln_mix_w, ln_xattn_w, ln_mem_w, xattn_wq, xattn_wkv, xattn_wo, ln_ffn_w, router_group_w, router_group_b, router_expert_w, router_expert_b, moe_w_gate_up, moe_w_down, pool_w, pool_scale, rwkv_mu, rwkv_w_rkv, rwkv_w_o, rwkv_w0, rwkv_w1, rwkv_w2, rwkv_a0, rwkv_a1, rwkv_a2, rwkv_g1, rwkv_g2, rwkv_k_k, rwkv_k_a, rwkv_r_k, rwkv_ln_w, rwkv_ln_b, ln_out_w):
    B, S, D = x.shape
    depth = ln_mix_w.shape[0]
    outs = []
    for b in range(B):
        h = x[b]
        for i in range(depth):
            j = i // 2
            if i % 2 == 0:
                h = _pool_layer(h, ln_mix_w[i], pool_w[j], pool_scale[j])
            else:
                h = _rwkv_layer(h, ln_mix_w[i], rwkv_mu[j], rwkv_w_rkv, rwkv_w_o, j, rwkv_w0[j], rwkv_w1[j],
                                rwkv_w2[j], rwkv_a0[j], rwkv_a1[j], rwkv_a2[j], rwkv_g1[j], rwkv_g2[j],
                                rwkv_k_k[j], rwkv_k_a[j], rwkv_r_k[j].reshape(D), rwkv_ln_w[j], rwkv_ln_b[j])
            kv = _mem_kv(mem[b], ln_mem_w[i], xattn_wkv, i)
            h = _xattn_layer(h, ln_xattn_w[i], xattn_wq, kv, xattn_wo, i)
            h = _moe_layer(h, ln_ffn_w[i], router_group_w[i], router_group_b[i], router_expert_w[i],
                           router_expert_b[i], moe_w_gate_up, moe_w_down, i, ln_out_w,
                           final_norm=(i == depth - 1))
        outs.append(h)
    return outs[0][None] if B == 1 else jnp.stack(outs, axis=0)
```

```python
import functools

import jax
import jax.numpy as jnp
from jax import lax
from jax.experimental import pallas as pl
from jax.experimental.pallas import tpu as pltpu

F32 = jnp.float32
BF16 = jnp.bfloat16

D_MODEL = 2048
POOL_WINDOWS = (2, 4, 8, 16)
POOL_GROUP_DIM = D_MODEL // len(POOL_WINDOWS)
MAX_WINDOW = max(POOL_WINDOWS)
RWKV_HEAD_DIM = 64
GN_EPS = 64e-5
XATTN_HEADS = 4
XATTN_HEAD_DIM = D_MODEL // XATTN_HEADS
N_GROUPS = 8
EXPERTS_PER_GROUP = 8
N_EXPERTS = N_GROUPS * EXPERTS_PER_GROUP
TOP_K = 2
D_EXPERT = D_MODEL // 8
ROW_BLOCK = 128
RMS_EPS = 1e-6

LANES = 128
SUBLANES = 8
VMEM_LIMIT = 56 * 1024 * 1024

CHUNK = 64
GROUP_HEADS = 4
GROUP = GROUP_HEADS * RWKV_HEAD_DIM
SLAB = D_MODEL // LANES
SLAB_PITCH = SLAB + SUBLANES
ISSUE_UNROLL = 8
WEIGHT_STAGE_ROWS = 256
SCAN_ROWS = 256
SCAN_LANES = 2048

NN = (((1,), (0,)), ((), ()))
NT = (((1,), (1,)), ((), ()))


def _params(*sem):
    return pltpu.CompilerParams(dimension_semantics=sem, vmem_limit_bytes=VMEM_LIMIT)


def _rms(x, w):
    return x * lax.rsqrt(jnp.mean(x * x, axis=-1, keepdims=True) + RMS_EPS) * w


def _dot(a, b, dims=NN):
    return lax.dot_general(a, b, dims, preferred_element_type=F32)


def _split(x):
    hi = x.astype(BF16)
    return hi, (x - hi.astype(F32)).astype(BF16)


def _store_slabs(ref, x):
    for j in range(SLAB):
        ref[pl.ds(j, x.shape[0], stride=SLAB), :] = x[:, j * LANES:(j + 1) * LANES]


def _load_slabs(ref, pitch=SLAB):
    n = ref.shape[0] // pitch
    return jnp.concatenate([ref[pl.ds(j, n, stride=pitch), :] for j in range(SLAB)], axis=-1)


def _slab(ref, row, pitch=SLAB):
    return ref.at[pl.ds(pl.multiple_of(row * pitch, SUBLANES), SLAB), :]


def _rkv_proj_body(x_ref, w_ref, o_ref, wb_ref):
    @pl.when(pl.program_id(2) == 0)
    def _():
        wb_ref[...] = w_ref[...].astype(BF16)

    o_ref[...] = _dot(x_ref[...], wb_ref[...])


def _rkv_proj(xs, w_rkv, j, tm=512, tn=1024):
    _, M, K = xs.shape
    B, N = w_rkv.shape[1], w_rkv.shape[3]
    return pl.pallas_call(
        _rkv_proj_body,
        grid=(B, N // tn, M // tm),
        in_specs=[pl.BlockSpec((None, tm, K), lambda b, n, i: (b, i, 0)),
                  pl.BlockSpec((None, None, K, tn), lambda b, n, i: (j, b, 0, n))],
        out_specs=pl.BlockSpec((None, tm, tn), lambda b, n, i: (b, i, n)),
        out_shape=jax.ShapeDtypeStruct((B, M, N), F32),
        scratch_shapes=[pltpu.VMEM((K, tn), BF16)],
        compiler_params=_params("arbitrary", "arbitrary", "arbitrary"),
        name="rwkv_rkv_proj",
    )(xs, w_rkv)


def _out_proj_body(x_ref, w_ref, r_ref, o_ref, wb_ref):
    @pl.when(pl.program_id(1) == 0)
    def _():
        wb_ref[...] = w_ref[...].astype(BF16)

    o_ref[...] = r_ref[...] + _dot(x_ref[...], wb_ref[...])


def _out_proj(x, w_o, j, res, tm=512, tn=1024):
    M, K = x.shape
    N = w_o.shape[2]
    return pl.pallas_call(
        _out_proj_body,
        grid=(N // tn, M // tm),
        in_specs=[pl.BlockSpec((tm, K), lambda n, i: (i, 0)),
                  pl.BlockSpec((None, K, tn), lambda n, i: (j, 0, n)),
                  pl.BlockSpec((tm, tn), lambda n, i: (i, n))],
        out_specs=pl.BlockSpec((tm, tn), lambda n, i: (i, n)),
        out_shape=jax.ShapeDtypeStruct((M, N), F32),
        scratch_shapes=[pltpu.VMEM((K, tn), BF16)],
        compiler_params=_params("arbitrary", "arbitrary"),
        name="rwkv_out_proj",
    )(x, w_o, res)


def _pool_body(h_ref, halo_ref, lnw_ref, pw_ref, ps_ref, o_ref):
    i = pl.program_id(0)
    ts = h_ref.shape[0]
    h = h_ref[...]
    lnw = lnw_ref[...]
    x = _rms(h, lnw)
    xh = _rms(halo_ref[...], lnw) * (i > 0).astype(F32)
    xe = jnp.concatenate([xh, x], axis=0)
    t = i * ts + lax.broadcasted_iota(jnp.int32, (ts, 1), 0)
    outs = []
    for g, win in enumerate(POOL_WINDOWS):
        sl = slice(g * POOL_GROUP_DIM, (g + 1) * POOL_GROUP_DIM)
        acc = xe[:, sl]
        span = 1
        while span < win:
            acc = acc + pltpu.roll(acc, span, axis=0)
            span *= 2
        cnt = jnp.minimum(t + 1, win).astype(F32)
        pooled = acc[MAX_WINDOW:, :] / cnt - x[:, sl]
        outs.append(_dot(pooled.astype(BF16), pw_ref[g]))
    y = jnp.concatenate(outs, axis=-1)
    o_ref[...] = h + y * ps_ref[...]


def _pool_layer(h, ln_w, pool_w, pool_scale, ts=512):
    S, D = h.shape
    G = pool_w.shape[0]
    halo_blocks = ts // MAX_WINDOW
    return pl.pallas_call(
        _pool_body,
        grid=(S // ts,),
        in_specs=[pl.BlockSpec((ts, D), lambda i: (i, 0)),
                  pl.BlockSpec((MAX_WINDOW, D), lambda i: (jnp.maximum(i * halo_blocks - 1, 0), 0)),
                  pl.BlockSpec((1, D), lambda i: (0, 0)),
                  pl.BlockSpec((G, POOL_GROUP_DIM, POOL_GROUP_DIM), lambda i: (0, 0, 0)),
                  pl.BlockSpec((1, D), lambda i: (0, 0))],
        out_specs=pl.BlockSpec((ts, D), lambda i: (i, 0)),
        out_shape=jax.ShapeDtypeStruct((S, D), F32),
        compiler_params=_params("arbitrary"),
        name="pool_layer",
    )(h, h, ln_w.reshape(1, D), pool_w.astype(BF16), pool_scale.reshape(1, D))


def _memkv_body(mem_ref, lnw_ref, w_ref, o_ref):
    mn = _rms(mem_ref[...], lnw_ref[...])
    o_ref[...] = _dot(mn.astype(BF16), w_ref[...].astype(BF16)).astype(o_ref.dtype)


def _mem_kv(mem, ln_w, wkv, layer, tn=1024):
    M, D = mem.shape
    N = wkv.shape[2]
    return pl.pallas_call(
        _memkv_body,
        grid=(N // tn,),
        in_specs=[pl.BlockSpec((M, D), lambda n: (0, 0)),
                  pl.BlockSpec((1, D), lambda n: (0, 0)),
                  pl.BlockSpec((None, D, tn), lambda n: (layer, 0, n))],
        out_specs=pl.BlockSpec((M, tn), lambda n: (0, n)),
        out_shape=jax.ShapeDtypeStruct((M, N), BF16),
        compiler_params=_params("arbitrary"),
        name="xattn_mem_kv",
    )(mem, ln_w.reshape(1, D), wkv)


def _load_weight_bf16(w_hbm, dst, stage, sem):
    rows = stage.shape[1]
    n_chunks = w_hbm.shape[0] // rows

    def chunk_copy(c):
        return pltpu.make_async_copy(w_hbm.at[pl.ds(c * rows, rows), :], stage.at[c % 2], sem.at[c % 2])

    chunk_copy(0).start()
    for c in range(n_chunks):
        if c + 1 < n_chunks:
            chunk_copy(c + 1).start()
        chunk_copy(c).wait()
        dst[pl.ds(c * rows, rows), :] = stage[c % 2].astype(BF16)


def _xattn_body(layer, h_ref, lnw_ref, k_ref, v_ref, wq_hbm, wo_hbm, o_ref, wq_ref, wo_ref, stage, sem):
    @pl.when(pl.program_id(0) == 0)
    def _():
        _load_weight_bf16(wq_hbm.at[layer], wq_ref, stage, sem)
        _load_weight_bf16(wo_hbm.at[layer], wo_ref, stage, sem)

    h = h_ref[...]
    hn = _rms(h, lnw_ref[...]).astype(BF16)
    q = _dot(hn, wq_ref[...])
    scale = XATTN_HEAD_DIM ** -0.5
    heads = []
    for hd in range(XATTN_HEADS):
        sl = slice(hd * XATTN_HEAD_DIM, (hd + 1) * XATTN_HEAD_DIM)
        s = _dot(q[:, sl].astype(BF16), k_ref[:, sl], NT) * scale
        s = s - jnp.max(s, axis=-1, keepdims=True)
        e = jnp.exp(s)
        p = e / jnp.sum(e, axis=-1, keepdims=True)
        heads.append(_dot(p.astype(BF16), v_ref[:, sl]))
    o = jnp.concatenate(heads, axis=-1).astype(BF16)
    o_ref[...] = h + _dot(o, wo_ref[...])


def _xattn_layer(h, ln_w, wq, kv, wo, layer, ts=256):
    S, D = h.shape
    M = kv.shape[0]
    return pl.pallas_call(
        functools.partial(_xattn_body, layer),
        grid=(S // ts,),
        in_specs=[pl.BlockSpec((ts, D), lambda i: (i, 0)),
                  pl.BlockSpec((1, D), lambda i: (0, 0)),
                  pl.BlockSpec((M, D), lambda i: (0, 0)),
                  pl.BlockSpec((M, D), lambda i: (0, 1)),
                  pl.BlockSpec(memory_space=pl.ANY),
                  pl.BlockSpec(memory_space=pl.ANY)],
        out_specs=pl.BlockSpec((ts, D), lambda i: (i, 0)),
        out_shape=jax.ShapeDtypeStruct((S, D), F32),
        scratch_shapes=[pltpu.VMEM((D, D), BF16), pltpu.VMEM((D, D), BF16),
                        pltpu.VMEM((2, WEIGHT_STAGE_ROWS, D), F32), pltpu.SemaphoreType.DMA((2,))],
        compiler_params=_params("arbitrary"),
        name="xattn_layer",
    )(h, ln_w.reshape(1, D), kv, kv, wq, wo)


def _router_body(h_ref, lnw_ref, whi_ref, wlo_ref, b_ref, hn_ref, r_ref, rt_ref, cnt_ref):
    i = pl.program_id(0)

    @pl.when(i == 0)
    def _():
        cnt_ref[...] = jnp.zeros_like(cnt_ref)

    hn = _rms(h_ref[...], lnw_ref[...])
    _store_slabs(hn_ref, hn)
    xh, xl = _split(hn)
    whi = whi_ref[...]
    lg = _dot(xh, whi) + (_dot(xh, wlo_ref[...]) + _dot(xl, whi)) + b_ref[...]
    ts = lg.shape[0]
    lane = lax.broadcasted_iota(jnp.int32, (ts, LANES), 1)
    lanef = lane.astype(F32)
    neg = jnp.float32(-jnp.inf)
    gl = jnp.where(lane < N_GROUPS, lg, neg)
    gmax = jnp.max(gl, axis=-1, keepdims=True)
    pg_top = 1.0 / jnp.sum(jnp.exp(gl - gmax), axis=-1, keepdims=True)
    g_idx = jnp.min(jnp.where(gl == gmax, lanef, float(LANES)), axis=-1, keepdims=True)
    lo_lane = g_idx * EXPERTS_PER_GROUP + N_GROUPS
    el = jnp.where(lanef >= lo_lane, jnp.where(lanef < lo_lane + EXPERTS_PER_GROUP, lg, neg), neg)
    m1 = jnp.max(el, axis=-1, keepdims=True)
    i1 = jnp.min(jnp.where(el == m1, lanef, float(LANES)), axis=-1, keepdims=True)
    el2 = jnp.where(lanef == i1, neg, el)
    m2 = jnp.max(el2, axis=-1, keepdims=True)
    i2 = jnp.min(jnp.where(el2 == m2, lanef, float(LANES)), axis=-1, keepdims=True)
    e2 = jnp.exp(m2 - m1)
    g1 = pg_top / (1.0 + e2)
    g2 = pg_top * e2 / (1.0 + e2)
    hit1 = (lanef == i1).astype(F32)
    hit2 = (lanef == i2).astype(F32)
    hits = hit1 + hit2
    earlier = (lax.broadcasted_iota(jnp.int32, (ts, ts), 1)
               < lax.broadcasted_iota(jnp.int32, (ts, ts), 0)).astype(F32).astype(BF16)
    before = _dot(earlier, hits.astype(BF16)) + cnt_ref[...]
    r1 = jnp.sum(before * hit1, axis=-1, keepdims=True)
    r2 = jnp.sum(before * hit2, axis=-1, keepdims=True)
    cnt_ref[...] += jnp.sum(hits, axis=0, keepdims=True)
    out = jnp.where(lane == 0, g1, 0.0)
    out = jnp.where(lane == 1, g2, out)
    out = jnp.where(lane == 2, i1 - N_GROUPS, out)
    out = jnp.where(lane == 3, i2 - N_GROUPS, out)
    out = jnp.where(lane == 4, r1, out)
    out = jnp.where(lane == 5, r2, out)
    r_ref[...] = out
    rt_ref[...] = out.T[:SUBLANES, :]


def _router(h, ln_w, wg, bg, we, be, ts=512):
    S, D = h.shape
    unused = LANES - N_GROUPS - N_EXPERTS
    w = jnp.concatenate([wg, we, jnp.zeros((D, unused), F32)], axis=1)
    b = jnp.concatenate([bg, be, jnp.zeros((unused,), F32)]).reshape(1, LANES)
    whi, wlo = _split(w)
    return pl.pallas_call(
        _router_body,
        grid=(S // ts,),
        in_specs=[pl.BlockSpec((ts, D), lambda i: (i, 0)),
                  pl.BlockSpec((1, D), lambda i: (0, 0)),
                  pl.BlockSpec((D, LANES), lambda i: (0, 0)),
                  pl.BlockSpec((D, LANES), lambda i: (0, 0)),
                  pl.BlockSpec((1, LANES), lambda i: (0, 0))],
        out_specs=[pl.BlockSpec((ts * SLAB, LANES), lambda i: (i, 0)),
                   pl.BlockSpec((ts, LANES), lambda i: (i, 0)),
                   pl.BlockSpec((SUBLANES, ts), lambda i: (0, i)),
                   pl.BlockSpec((1, LANES), lambda i: (0, 0))],
        out_shape=[jax.ShapeDtypeStruct((S * SLAB, LANES), F32), jax.ShapeDtypeStruct((S, LANES), F32),
                   jax.ShapeDtypeStruct((SUBLANES, S), F32), jax.ShapeDtypeStruct((1, LANES), F32)],
        compiler_params=_params("arbitrary"),
        name="moe_router",
    )(h, ln_w.reshape(1, D), whi, wlo, b)


def _dispatch_body(pos_ref, fill_lo_ref, fill_hi_ref, nblk_ref, hn_hbm, xs_hbm, zeros, sem, zsem):
    i = pl.program_id(0)
    n = pl.num_programs(0)
    tb = hn_hbm.shape[0] // SLAB // n
    n_tok = n * tb
    slot = i % 2

    def wait_step(s):
        for _ in range(TOP_K):
            pltpu.make_async_copy(hn_hbm.at[pl.ds(0, tb * SLAB), :], xs_hbm.at[pl.ds(0, tb * SLAB), :],
                                  sem.at[s]).wait()

    def issue(r, carry):
        t = i * tb + r
        for k in range(TOP_K):
            pltpu.make_async_copy(_slab(hn_hbm, t), _slab(xs_hbm, pos_ref[k * n_tok + t]),
                                  sem.at[slot]).start(priority=k % 2)
        return carry
    lax.fori_loop(0, tb, issue, 0, unroll=ISSUE_UNROLL)

    @pl.when(i >= 1)
    def _():
        wait_step(1 - slot)

    @pl.when(i == n - 1)
    def _():
        zeros[...] = jnp.zeros_like(zeros)

        def zero_row(row):
            return pltpu.make_async_copy(zeros.at[pl.ds(0, SLAB), :], _slab(xs_hbm, row), zsem)

        def zero_block(b):
            return pltpu.make_async_copy(zeros, _slab_block(xs_hbm, b), zsem)

        def per_expert(e, carry):
            lo, hi = fill_lo_ref[e], fill_hi_ref[e]
            lax.fori_loop(lo, hi, lambda row, c: (zero_row(row).start(), c)[1], 0)
            lax.fori_loop(lo, hi, lambda row, c: (zero_row(row).wait(), c)[1], 0)
            return carry
        lax.fori_loop(0, N_EXPERTS, per_expert, 0)
        n_blocks = xs_hbm.shape[0] // (ROW_BLOCK * SLAB)
        lax.fori_loop(nblk_ref[0], n_blocks, lambda b, c: (zero_block(b).start(), c)[1], 0)
        lax.fori_loop(nblk_ref[0], n_blocks, lambda b, c: (zero_block(b).wait(), c)[1], 0)
        wait_step(slot)


def _slab_block(ref, block):
    rows = ROW_BLOCK * SLAB
    return ref.at[pl.ds(pl.multiple_of(block * rows, rows), rows), :]


def _dispatch(pos, fill_lo, fill_hi, n_used, hn, n_rows, tb=512):
    S = hn.shape[0] // SLAB
    return pl.pallas_call(
        _dispatch_body,
        grid_spec=pltpu.PrefetchScalarGridSpec(
            num_scalar_prefetch=4,
            grid=(S // tb,),
            in_specs=[pl.BlockSpec(memory_space=pl.ANY)],
            out_specs=pl.BlockSpec(memory_space=pl.ANY),
            scratch_shapes=[pltpu.VMEM((ROW_BLOCK * SLAB, LANES), F32),
                            pltpu.SemaphoreType.DMA((2,)), pltpu.SemaphoreType.DMA(())]),
        out_shape=jax.ShapeDtypeStruct((n_rows * SLAB, LANES), F32),
        compiler_params=_params("arbitrary"),
        name="moe_dispatch",
    )(pos, fill_lo, fill_hi, n_used, hn)


def _experts_body(layer, nblk_ref, be_ref, fresh_ref, wslot_ref, next_e_ref, x_ref, wgu_hbm, wd_hbm, o_ref,
                  wgu_f, wd_f, wgu_b, wd_b, wsem):
    i = pl.program_id(0)
    n_used = nblk_ref[0]

    def weight_copies(e, slot):
        return (pltpu.make_async_copy(wgu_hbm.at[layer, e], wgu_f.at[slot], wsem.at[0, slot]),
                pltpu.make_async_copy(wd_hbm.at[layer, e], wd_f.at[slot], wsem.at[1, slot]))

    @pl.when(i == 0)
    def _():
        for cp in weight_copies(be_ref[0], 0):
            cp.start()

    @pl.when(jnp.logical_and(i < n_used, fresh_ref[i] == 1))
    def _():
        slot = wslot_ref[i]
        for cp in weight_copies(be_ref[i], slot):
            cp.wait()
        wgu_b[...] = wgu_f[slot].astype(BF16)
        wd_b[...] = wd_f[slot].astype(BF16)

        @pl.when(next_e_ref[i] >= 0)
        def _():
            for cp in weight_copies(next_e_ref[i], 1 - slot):
                cp.start()

    @pl.when(i < n_used)
    def _():
        gu = _dot(_load_slabs(x_ref).astype(BF16), wgu_b[...])
        gg = gu[:, :D_EXPERT]
        uu = gu[:, D_EXPERT:]
        act = (gg * (1.0 / (1.0 + jnp.exp(-gg))) * uu).astype(BF16)
        _store_slabs(o_ref, _dot(act, wd_b[...]))

    @pl.when(i >= n_used)
    def _():
        o_ref[...] = jnp.zeros_like(o_ref)


def _experts(n_used, block_e, xs, w_gu, w_down, layer):
    rows = ROW_BLOCK * SLAB
    n_blocks = xs.shape[0] // rows
    D = w_gu.shape[2]
    F2 = w_gu.shape[3]
    idx = jnp.arange(n_blocks, dtype=jnp.int32)
    used = idx < n_used[0]
    fresh = jnp.logical_and(used, jnp.concatenate([jnp.ones((1,), bool), block_e[1:] != block_e[:-1]]))
    run = jnp.cumsum(fresh.astype(jnp.int32)) - 1
    run_start = jnp.where(fresh, idx, n_blocks)
    next_start = lax.cummin(jnp.concatenate([run_start[1:], jnp.full((1,), n_blocks, jnp.int32)]), reverse=True)
    next_e = jnp.where(next_start < n_blocks, block_e[jnp.minimum(next_start, n_blocks - 1)], -1)
    return pl.pallas_call(
        functools.partial(_experts_body, layer),
        grid_spec=pltpu.PrefetchScalarGridSpec(
            num_scalar_prefetch=5,
            grid=(n_blocks,),
            in_specs=[pl.BlockSpec((rows, LANES), lambda i, nb, *_: (jnp.minimum(i, nb[0] - 1), 0)),
                      pl.BlockSpec(memory_space=pl.ANY),
                      pl.BlockSpec(memory_space=pl.ANY)],
            out_specs=pl.BlockSpec((rows, LANES), lambda i, *_: (i, 0)),
            scratch_shapes=[pltpu.VMEM((2, D, F2), F32), pltpu.VMEM((2, F2 // 2, D), F32),
                            pltpu.VMEM((D, F2), BF16), pltpu.VMEM((F2 // 2, D), BF16),
                            pltpu.SemaphoreType.DMA((2, 2))]),
        out_shape=jax.ShapeDtypeStruct(xs.shape, F32),
        compiler_params=_params("arbitrary"),
        name="moe_experts",
    )(n_used, block_e, fresh.astype(jnp.int32), (run % 2).astype(jnp.int32), next_e.astype(jnp.int32),
      xs, w_gu, w_down)


def _start_pair_gather(pos_ref, first_tok, n_tok, yb_hbm, ybuf, slot, sem, tb):
    def issue(r, carry):
        for k in range(TOP_K):
            pltpu.make_async_copy(_slab(yb_hbm, pos_ref[k * n_tok + first_tok + r]), _slab(ybuf.at[slot, k], r, SLAB_PITCH),
                                  sem.at[slot]).start(priority=k % 2)
        return carry
    lax.fori_loop(0, tb, issue, 0, unroll=ISSUE_UNROLL)


def _wait_pair_gather(yb_hbm, ybuf, slot, sem, tb):
    for k in range(TOP_K):
        pltpu.make_async_copy(yb_hbm.at[pl.ds(0, tb * SLAB), :], ybuf.at[slot, k, pl.ds(0, tb * SLAB), :],
                              sem.at[slot]).wait()


def _combine_body(final_norm, pos_ref, yb_hbm, h_ref, r_ref, lnw_ref, o_ref, ybuf, sem):
    i = pl.program_id(0)
    n = pl.num_programs(0)
    tb = h_ref.shape[0]
    slot = i % 2

    @pl.when(i == 0)
    def _():
        _start_pair_gather(pos_ref, 0, n * tb, yb_hbm, ybuf, 0, sem, tb)

    @pl.when(i + 1 < n)
    def _():
        _start_pair_gather(pos_ref, (i + 1) * tb, n * tb, yb_hbm, ybuf, 1 - slot, sem, tb)

    _wait_pair_gather(yb_hbm, ybuf, slot, sem, tb)
    out = h_ref[...]
    for k in range(TOP_K):
        out = out + _load_slabs(ybuf.at[slot, k], SLAB_PITCH) * r_ref[:, k:k + 1]
    if final_norm:
        out = _rms(out, lnw_ref[...])
    o_ref[...] = out


def _combine(pos, yb, h, routed, ln_out_w, final_norm, tb=128):
    S, D = h.shape
    return pl.pallas_call(
        functools.partial(_combine_body, final_norm),
        grid_spec=pltpu.PrefetchScalarGridSpec(
            num_scalar_prefetch=1,
            grid=(S // tb,),
            in_specs=[pl.BlockSpec(memory_space=pl.ANY),
                      pl.BlockSpec((tb, D), lambda i, pos: (i, 0)),
                      pl.BlockSpec((tb, LANES), lambda i, pos: (i, 0)),
                      pl.BlockSpec((1, D), lambda i, pos: (0, 0))],
            out_specs=pl.BlockSpec((tb, D), lambda i, pos: (i, 0)),
            scratch_shapes=[pltpu.VMEM((2, TOP_K, tb * SLAB_PITCH, LANES), F32), pltpu.SemaphoreType.DMA((2,))]),
        out_shape=jax.ShapeDtypeStruct((S, D), F32),
        compiler_params=_params("arbitrary"),
        name="moe_combine",
    )(pos, yb, h, routed, ln_out_w.reshape(1, D))


def _moe_layer(h, ln_w, wg, bg, we, be, w_gu, w_down, layer, ln_out_w, final_norm):
    T, D = h.shape
    hn, routed, routed_t, counts = _router(h, ln_w, wg, bg, we, be)
    A = T * TOP_K
    n_blocks = -(-A // ROW_BLOCK) + N_EXPERTS
    n_rows = n_blocks * ROW_BLOCK
    counts = counts[0, N_GROUPS:N_GROUPS + N_EXPERTS].astype(jnp.int32)
    padded = (counts + ROW_BLOCK - 1) // ROW_BLOCK * ROW_BLOCK
    pad_end = jnp.cumsum(padded)
    pad_start = pad_end - padded
    expert_idx = routed_t[2:2 + TOP_K].astype(jnp.int32)
    slot = routed_t[4:4 + TOP_K].astype(jnp.int32)
    is_e = expert_idx[..., None] == jnp.arange(N_EXPERTS, dtype=jnp.int32)
    pos = (jnp.sum(jnp.where(is_e, pad_start, 0), axis=-1) + slot).reshape(A)
    block_start = jnp.arange(n_blocks, dtype=jnp.int32) * ROW_BLOCK
    block_e = jnp.minimum(jnp.sum(pad_end[None, :] <= block_start[:, None], axis=1), N_EXPERTS - 1).astype(jnp.int32)
    n_used = (pad_end[-1] // ROW_BLOCK).astype(jnp.int32).reshape(1)
    xs = _dispatch(pos, pad_start + counts, pad_end, n_used, hn, n_rows)
    yb = _experts(n_used, block_e, xs, w_gu, w_down, layer)
    return _combine(pos, yb, h, routed, ln_out_w, final_norm)


def _sigmoid(x):
    return 1.0 / (1.0 + jnp.exp(-x))


def _rwkv_prep_body(h_ref, halo_ref, lnw_ref, mu_ref, w0_ref, w1_ref, w2_ref, a0_ref, a1_ref, a2_ref, g1_ref,
                    g2_ref, xs_ref, lw_ref, a_ref, g_ref):
    i = pl.program_id(0)
    lnw = lnw_ref[...]
    hn = _rms(h_ref[...], lnw)
    ts = hn.shape[0]
    last = _rms(halo_ref[...], lnw)[SUBLANES - 1:SUBLANES, :] * (i > 0).astype(F32)
    row = lax.broadcasted_iota(jnp.int32, (ts, 1), 0)
    prev = jnp.where(row == 0, last, pltpu.roll(hn, 1, axis=0))
    xx = prev - hn
    mix = lambda n: (hn + xx * mu_ref[n:n + 1, :]).astype(BF16)
    n_out = xs_ref.shape[0]
    for n in range(n_out):
        xs_ref[n] = mix(n)
    z = w0_ref[...] + _dot(jnp.tanh(_dot(mix(n_out), w1_ref[...])).astype(BF16), w2_ref[...])
    u = -z
    softplus = jnp.maximum(u, 0.0) + jnp.log1p(jnp.exp(-jnp.abs(u)))
    lw_ref[...] = -jnp.exp(-softplus - 0.5)
    a_ref[...] = _sigmoid(a0_ref[...] + _dot(_dot(mix(n_out + 1), a1_ref[...]).astype(BF16), a2_ref[...]))
    g_ref[...] = _dot(_sigmoid(_dot(mix(n_out + 2), g1_ref[...])).astype(BF16), g2_ref[...])


def _rwkv_prep(h, ln_w, mu, w0, w1, w2, a0, a1, a2, g1, g2, ts=256):
    S, D = h.shape
    n_mix = mu.shape[0]
    n_out = n_mix - 3
    halo_blocks = ts // SUBLANES

    def pad_lora(wa, wb):
        r = wa.shape[1]
        rp = -(-r // LANES) * LANES
        return (jnp.pad(wa, ((0, 0), (0, rp - r))).astype(BF16), jnp.pad(wb, ((0, rp - r), (0, 0))).astype(BF16))

    w1p, w2p = pad_lora(w1, w2)
    a1p, a2p = pad_lora(a1, a2)
    g1p, g2p = pad_lora(g1, g2)
    full = lambda arr: pl.BlockSpec(arr.shape, lambda i: (0,) * arr.ndim)
    vec = pl.BlockSpec((1, D), lambda i: (0, 0))
    row = pl.BlockSpec((ts, D), lambda i: (i, 0))
    return pl.pallas_call(
        _rwkv_prep_body,
        grid=(S // ts,),
        in_specs=[row,
                  pl.BlockSpec((SUBLANES, D), lambda i: (jnp.maximum(i * halo_blocks - 1, 0), 0)),
                  vec, full(mu), vec, full(w1p), full(w2p), vec, full(a1p), full(a2p), full(g1p), full(g2p)],
        out_specs=[pl.BlockSpec((n_out, ts, D), lambda i: (0, i, 0)), row, row, row],
        out_shape=[jax.ShapeDtypeStruct((n_out, S, D), BF16)] + [jax.ShapeDtypeStruct((S, D), F32)] * 3,
        compiler_params=_params("arbitrary"),
        name="rwkv_prep",
    )(h, h, ln_w.reshape(1, D), mu, w0.reshape(1, D), w1p, w2p, a0.reshape(1, D), a1p, a2p, g1p, g2p)


def _scan_chunk(r, k, v, lw, a, g, kkw, kaw, rkw, lnw, lnb, state, c):
    masks, seg_f, seg_b, tri, strict, incl, eye = c
    bf = lambda x: x.astype(BF16)
    stack = lambda xb: jnp.concatenate([xb * m for m in masks], axis=0)
    rows = lambda *xs: jnp.concatenate(xs, axis=0)
    cols = lambda *xs: jnp.concatenate(xs, axis=1)

    kk = k * kkw
    k2 = k * (1.0 + (a - 1.0) * kaw)
    sums = _segsum(rows(kk * kk, r * k2 * rkw), seg_b)
    cw = _cumsum_rows(tri, lw)
    yield
    kk = kk / jnp.maximum(jnp.sqrt(sums[:CHUNK]), 1e-12)
    bonus = sums[CHUNK:] * v
    bv = kk * a
    cl = cw[CHUNK - 1:CHUNK, :]
    at = bf(-kk * jnp.exp(cw - lw))
    rt = bf(r * jnp.exp(cw))
    dinv = jnp.exp(-cw)
    drem = jnp.exp(cl - cw)
    vb = bf(v)
    vs = stack(vb)
    gram = _dot(rows(at, rt), rows(stack(bf(bv * dinv)), stack(bf(k2 * dinv))), NT)
    bk_t = bf(rows(bv * drem, k2 * drem).T)
    decay_rows = jnp.broadcast_to(jnp.exp(cl), (GROUP, GROUP)).T
    yield
    a_ab = jnp.where(strict, gram[:CHUNK, :GROUP], 0.0)
    a_ak = bf(jnp.where(strict, gram[:CHUNK, GROUP:], 0.0))
    a_rb = bf(jnp.where(incl, gram[CHUNK:, :GROUP], 0.0))
    a_rk = bf(jnp.where(incl, gram[CHUNK:, GROUP:], 0.0))
    inv = eye + a_ab
    pw = _dot(bf(a_ab), stack(bf(a_ab)))
    xy = _dot(rows(cols(at, a_ak), cols(rt, a_rk)), rows(bf(state), vs))
    yield
    span = 2
    while span < CHUNK // 2:
        pwb = bf(pw)
        both = _dot(rows(bf(inv), pwb), stack(pwb))
        yield
        inv = inv + both[:CHUNK]
        pw = both[CHUNK:]
        span *= 2
    inv = inv + _dot(bf(inv), stack(bf(pw)))
    yield
    ub = bf(_dot(bf(inv), stack(bf(xy[:CHUNK]))))
    yield
    y = xy[CHUNK:] + _dot(a_rb, stack(ub))
    new_state = state * decay_rows + seg_f * _dot(bk_t, rows(ub, vb))
    yield
    inv_n = 1.0 / RWKV_HEAD_DIM
    mean = _segsum(y, seg_b) * inv_n
    yield
    yc = y - mean
    var = _segsum(yc * yc, seg_b) * inv_n
    yield
    yn = yc * lax.rsqrt(var + GN_EPS) * lnw + lnb
    return ((yn + bonus) * g), new_state


def _run_interleaved(gens):
    results = [None] * len(gens)
    live = list(range(len(gens)))
    while live:
        for i in list(live):
            try:
                next(gens[i])
            except StopIteration as done:
                results[i] = done.value
                live.remove(i)
    return results


def _segsum(x, seg_b):
    return _dot(x.astype(BF16), seg_b)


def _cumsum_rows(tri, x):
    hi = x.astype(BF16)
    r1 = x - hi.astype(F32)
    mid = r1.astype(BF16)
    lo = (r1 - mid.astype(F32)).astype(BF16)
    return _dot(jnp.concatenate([tri, tri, tri], axis=1), jnp.concatenate([hi, mid, lo], axis=0))


def _scan_consts():
    shift = RWKV_HEAD_DIM.bit_length() - 1
    lane_head = lax.broadcasted_iota(jnp.int32, (1, GROUP), 1) >> shift
    masks = [(lane_head == h).astype(F32).astype(BF16) for h in range(GROUP_HEADS)]
    ri = lax.broadcasted_iota(jnp.int32, (GROUP, GROUP), 0) >> shift
    ci = lax.broadcasted_iota(jnp.int32, (GROUP, GROUP), 1) >> shift
    seg_f = (ri == ci).astype(F32)
    tri = (lax.broadcasted_iota(jnp.int32, (CHUNK, CHUNK), 1)
           <= lax.broadcasted_iota(jnp.int32, (CHUNK, CHUNK), 0)).astype(F32).astype(BF16)
    t = lax.broadcasted_iota(jnp.int32, (CHUNK, GROUP), 0)
    s = lax.broadcasted_iota(jnp.int32, (CHUNK, GROUP), 1) & (RWKV_HEAD_DIM - 1)
    return (masks, seg_f, seg_f.astype(BF16), tri, s < t, s <= t, (s == t).astype(F32))


def _scan_body(r_ref, k_ref, v_ref, lw_ref, a_ref, g_ref, kkw_ref, kaw_ref, rkw_ref, lnw_ref, lnb_ref,
               o_ref, state_ref):
    @pl.when(pl.program_id(1) == 0)
    def _():
        state_ref[...] = jnp.zeros_like(state_ref)

    n_groups = o_ref.shape[1] // GROUP

    def chunk_step(j, carry):
        consts = _scan_consts()
        rows = pl.ds(pl.multiple_of(j * CHUNK, CHUNK), CHUNK)
        lanes = [slice(p * GROUP, (p + 1) * GROUP) for p in range(n_groups)]
        args = [(r_ref[rows, ln], k_ref[rows, ln], v_ref[rows, ln], lw_ref[rows, ln], a_ref[rows, ln],
                 g_ref[rows, ln], kkw_ref[:, ln], kaw_ref[:, ln], rkw_ref[:, ln], lnw_ref[:, ln],
                 lnb_ref[:, ln], state_ref[p]) for p, ln in enumerate(lanes)]
        results = _run_interleaved([_scan_chunk(*a, consts) for a in args])
        for p, (ln, (out, new_state)) in enumerate(zip(lanes, results)):
            state_ref[p] = new_state
            o_ref[rows, ln] = out.astype(o_ref.dtype)
        return carry

    lax.fori_loop(0, o_ref.shape[0] // CHUNK, chunk_step, 0)


def _rwkv_scan(rkv, lw, a, g, k_k, k_a, r_k, ln_w, ln_b):
    _, S, D = rkv.shape
    tr, tl = min(SCAN_ROWS, S), SCAN_LANES
    blk = lambda n: pl.BlockSpec((None, tr, tl), lambda p, i: (n, i, p))
    row = pl.BlockSpec((tr, tl), lambda p, i: (i, p))
    vec = pl.BlockSpec((1, tl), lambda p, i: (0, p))
    as_row = lambda w: w.reshape(1, D)
    return pl.pallas_call(
        _scan_body,
        grid=(D // tl, S // tr),
        in_specs=[blk(0), blk(1), blk(2), row, row, row, vec, vec, vec, vec, vec],
        out_specs=row,
        out_shape=jax.ShapeDtypeStruct((S, D), BF16),
        scratch_shapes=[pltpu.VMEM((tl // GROUP, GROUP, GROUP), F32)],
        compiler_params=_params("arbitrary", "arbitrary"),
        name="rwkv_scan",
    )(rkv, rkv, rkv, lw, a, g, as_row(k_k), as_row(k_a), as_row(r_k), as_row(ln_w), as_row(ln_b))


def _rwkv_layer(h, ln_w, mu, w_rkv, w_o, j, w0, w1, w2, a0, a1, a2, g1, g2, k_k, k_a, r_k, ln_gn_w, ln_gn_b):
    xs, lw, a, g = _rwkv_prep(h, ln_w, mu, w0, w1, w2, a0, a1, a2, g1, g2)
    rkv = _rkv_proj(xs, w_rkv, j)
    z = _rwkv_scan(rkv, lw, a, g, k_k, k_a, r_k, ln_gn_w, ln_gn_b)
    return _out_proj(z, w_o, j, h)


def kernel(x, mem, ln_mix_w, ln_xattn_w, ln_mem_w, xattn_wq, xattn_wkv, xattn_wo, ln_ffn_w, router_group_w, router_group_b, router_expert_w, router_expert_b, moe_w_gate_up, moe_w_down, pool_w, pool_scale, rwkv_mu, rwkv_w_rkv, rwkv_w_o, rwkv_w0, rwkv_w1, rwkv_w2, rwkv_a0, rwkv_a1, rwkv_a2, rwkv_g1, rwkv_g2, rwkv_k_k, rwkv_k_a, rwkv_r_k, rwkv_ln_w, rwkv_ln_b, ln_out_w):
    B, S, D = x.shape
    depth = ln_mix_w.shape[0]
    outs = []
    for b in range(B):
        h = x[b]
        for i in range(depth):
            j = i // 2
            if i % 2 == 0:
                h = _pool_layer(h, ln_mix_w[i], pool_w[j], pool_scale[j])
            else:
                h = _rwkv_layer(h, ln_mix_w[i], rwkv_mu[j], rwkv_w_rkv, rwkv_w_o, j, rwkv_w0[j], rwkv_w1[j],
                                rwkv_w2[j], rwkv_a0[j], rwkv_a1[j], rwkv_a2[j], rwkv_g1[j], rwkv_g2[j],
                                rwkv_k_k[j], rwkv_k_a[j], rwkv_r_k[j].reshape(D), rwkv_ln_w[j], rwkv_ln_b[j])
            kv = _mem_kv(mem[b], ln_mem_w[i], xattn_wkv, i)
            h = _xattn_layer(h, ln_xattn_w[i], xattn_wq, kv, xattn_wo, i)
            h = _moe_layer(h, ln_ffn_w[i], router_group_w[i], router_group_b[i], router_expert_w[i],
                           router_expert_b[i], moe_w_gate_up, moe_w_down, i, ln_out_w,
                           final_norm=(i == depth - 1))
        outs.append(h)
    return outs[0][None] if B == 1 else jnp.stack(outs, axis=0)
```

```python
import functools

import jax
import jax.numpy as jnp
from jax import lax
from jax.experimental import pallas as pl
from jax.experimental.pallas import tpu as pltpu

F32 = jnp.float32
BF16 = jnp.bfloat16

D_MODEL = 2048
POOL_WINDOWS = (2, 4, 8, 16)
POOL_GROUP_DIM = D_MODEL // len(POOL_WINDOWS)
MAX_WINDOW = max(POOL_WINDOWS)
RWKV_HEAD_DIM = 64
GN_EPS = 64e-5
XATTN_HEADS = 4
XATTN_HEAD_DIM = D_MODEL // XATTN_HEADS
N_GROUPS = 8
EXPERTS_PER_GROUP = 8
N_EXPERTS = N_GROUPS * EXPERTS_PER_GROUP
TOP_K = 2
D_EXPERT = D_MODEL // 8
ROW_BLOCK = 128
RMS_EPS = 1e-6

LANES = 128
SUBLANES = 8
VMEM_LIMIT = 56 * 1024 * 1024

CHUNK = 64
GROUP_HEADS = 4
GROUP = GROUP_HEADS * RWKV_HEAD_DIM
SLAB = D_MODEL // LANES
SLAB_PITCH = SLAB + SUBLANES
ISSUE_UNROLL = 8
WEIGHT_STAGE_ROWS = 256
SCAN_ROWS = 256
SCAN_LANES = 2048

NN = (((1,), (0,)), ((), ()))
NT = (((1,), (1,)), ((), ()))


def _params(*sem):
    return pltpu.CompilerParams(dimension_semantics=sem, vmem_limit_bytes=VMEM_LIMIT)


def _rms(x, w):
    return x * lax.rsqrt(jnp.mean(x * x, axis=-1, keepdims=True) + RMS_EPS) * w


def _dot(a, b, dims=NN):
    return lax.dot_general(a, b, dims, preferred_element_type=F32)


def _split(x):
    hi = x.astype(BF16)
    return hi, (x - hi.astype(F32)).astype(BF16)


def _store_slabs(ref, x):
    for j in range(SLAB):
        ref[pl.ds(j, x.shape[0], stride=SLAB), :] = x[:, j * LANES:(j + 1) * LANES]


def _load_slabs(ref, pitch=SLAB):
    n = ref.shape[0] // pitch
    return jnp.concatenate([ref[pl.ds(j, n, stride=pitch), :] for j in range(SLAB)], axis=-1)


def _slab(ref, row, pitch=SLAB):
    return ref.at[pl.ds(pl.multiple_of(row * pitch, SUBLANES), SLAB), :]


def _rkv_proj_body(x_ref, w_ref, o_ref, wb_ref):
    @pl.when(pl.program_id(2) == 0)
    def _():
        wb_ref[...] = w_ref[...].astype(BF16)

    o_ref[...] = _dot(x_ref[...], wb_ref[...])


def _rkv_proj(xs, w_rkv, j, tm=512, tn=1024):
    _, M, K = xs.shape
    B, N = w_rkv.shape[1], w_rkv.shape[3]
    return pl.pallas_call(
        _rkv_proj_body,
        grid=(B, N // tn, M // tm),
        in_specs=[pl.BlockSpec((None, tm, K), lambda b, n, i: (b, i, 0)),
                  pl.BlockSpec((None, None, K, tn), lambda b, n, i: (j, b, 0, n))],
        out_specs=pl.BlockSpec((None, tm, tn), lambda b, n, i: (b, i, n)),
        out_shape=jax.ShapeDtypeStruct((B, M, N), F32),
        scratch_shapes=[pltpu.VMEM((K, tn), BF16)],
        compiler_params=_params("arbitrary", "arbitrary", "arbitrary"),
        name="rwkv_rkv_proj",
    )(xs, w_rkv)


def _out_proj_body(x_ref, w_ref, r_ref, o_ref, wb_ref):
    @pl.when(pl.program_id(1) == 0)
    def _():
        wb_ref[...] = w_ref[...].astype(BF16)

    o_ref[...] = r_ref[...] + _dot(x_ref[...], wb_ref[...])


def _out_proj(x, w_o, j, res, tm=512, tn=1024):
    M, K = x.shape
    N = w_o.shape[2]
    return pl.pallas_call(
        _out_proj_body,
        grid=(N // tn, M // tm),
        in_specs=[pl.BlockSpec((tm, K), lambda n, i: (i, 0)),
                  pl.BlockSpec((None, K, tn), lambda n, i: (j, 0, n)),
                  pl.BlockSpec((tm, tn), lambda n, i: (i, n))],
        out_specs=pl.BlockSpec((tm, tn), lambda n, i: (i, n)),
        out_shape=jax.ShapeDtypeStruct((M, N), F32),
        scratch_shapes=[pltpu.VMEM((K, tn), BF16)],
        compiler_params=_params("arbitrary", "arbitrary"),
        name="rwkv_out_proj",
    )(x, w_o, res)


def _pool_body(h_ref, halo_ref, lnw_ref, pw_ref, ps_ref, o_ref):
    i = pl.program_id(0)
    ts = h_ref.shape[0]
    h = h_ref[...]
    lnw = lnw_ref[...]
    x = _rms(h, lnw)
    xh = _rms(halo_ref[...], lnw) * (i > 0).astype(F32)
    xe = jnp.concatenate([xh, x], axis=0)
    t = i * ts + lax.broadcasted_iota(jnp.int32, (ts, 1), 0)
    outs = []
    for g, win in enumerate(POOL_WINDOWS):
        sl = slice(g * POOL_GROUP_DIM, (g + 1) * POOL_GROUP_DIM)
        acc = xe[:, sl]
        span = 1
        while span < win:
            acc = acc + pltpu.roll(acc, span, axis=0)
            span *= 2
        cnt = jnp.minimum(t + 1, win).astype(F32)
        pooled = acc[MAX_WINDOW:, :] / cnt - x[:, sl]
        outs.append(_dot(pooled.astype(BF16), pw_ref[g]))
    y = jnp.concatenate(outs, axis=-1)
    o_ref[...] = h + y * ps_ref[...]


def _pool_layer(h, ln_w, pool_w, pool_scale, ts=512):
    S, D = h.shape
    G = pool_w.shape[0]
    halo_blocks = ts // MAX_WINDOW
    return pl.pallas_call(
        _pool_body,
        grid=(S // ts,),
        in_specs=[pl.BlockSpec((ts, D), lambda i: (i, 0)),
                  pl.BlockSpec((MAX_WINDOW, D), lambda i: (jnp.maximum(i * halo_blocks - 1, 0), 0)),
                  pl.BlockSpec((1, D), lambda i: (0, 0)),
                  pl.BlockSpec((G, POOL_GROUP_DIM, POOL_GROUP_DIM), lambda i: (0, 0, 0)),
                  pl.BlockSpec((1, D), lambda i: (0, 0))],
        out_specs=pl.BlockSpec((ts, D), lambda i: (i, 0)),
        out_shape=jax.ShapeDtypeStruct((S, D), F32),
        compiler_params=_params("arbitrary"),
        name="pool_layer",
    )(h, h, ln_w.reshape(1, D), pool_w.astype(BF16), pool_scale.reshape(1, D))


def _memkv_body(mem_ref, lnw_ref, w_ref, o_ref):
    mn = _rms(mem_ref[...], lnw_ref[...])
    o_ref[...] = _dot(mn.astype(BF16), w_ref[...].astype(BF16)).astype(o_ref.dtype)


def _mem_kv(mem, ln_w, wkv, layer, tn=1024):
    M, D = mem.shape
    N = wkv.shape[2]
    return pl.pallas_call(
        _memkv_body,
        grid=(N // tn,),
        in_specs=[pl.BlockSpec((M, D), lambda n: (0, 0)),
                  pl.BlockSpec((1, D), lambda n: (0, 0)),
                  pl.BlockSpec((None, D, tn), lambda n: (layer, 0, n))],
        out_specs=pl.BlockSpec((M, tn), lambda n: (0, n)),
        out_shape=jax.ShapeDtypeStruct((M, N), BF16),
        compiler_params=_params("arbitrary"),
        name="xattn_mem_kv",
    )(mem, ln_w.reshape(1, D), wkv)


def _load_weight_bf16(w_hbm, dst, stage, sem):
    rows = stage.shape[1]
    n_chunks = w_hbm.shape[0] // rows

    def chunk_copy(c):
        return pltpu.make_async_copy(w_hbm.at[pl.ds(c * rows, rows), :], stage.at[c % 2], sem.at[c % 2])

    chunk_copy(0).start()
    for c in range(n_chunks):
        if c + 1 < n_chunks:
            chunk_copy(c + 1).start()
        chunk_copy(c).wait()
        dst[pl.ds(c * rows, rows), :] = stage[c % 2].astype(BF16)


def _xattn_body(layer, h_ref, lnw_ref, k_ref, v_ref, wq_hbm, wo_hbm, o_ref, wq_ref, wo_ref, stage, sem):
    @pl.when(pl.program_id(0) == 0)
    def _():
        _load_weight_bf16(wq_hbm.at[layer], wq_ref, stage, sem)
        _load_weight_bf16(wo_hbm.at[layer], wo_ref, stage, sem)

    h = h_ref[...]
    hn = _rms(h, lnw_ref[...]).astype(BF16)
    q = _dot(hn, wq_ref[...])
    scale = XATTN_HEAD_DIM ** -0.5
    heads = []
    for hd in range(XATTN_HEADS):
        sl = slice(hd * XATTN_HEAD_DIM, (hd + 1) * XATTN_HEAD_DIM)
        s = _dot(q[:, sl].astype(BF16), k_ref[:, sl], NT) * scale
        s = s - jnp.max(s, axis=-1, keepdims=True)
        e = jnp.exp(s)
        p = e / jnp.sum(e, axis=-1, keepdims=True)
        heads.append(_dot(p.astype(BF16), v_ref[:, sl]))
    o = jnp.concatenate(heads, axis=-1).astype(BF16)
    o_ref[...] = h + _dot(o, wo_ref[...])


def _xattn_layer(h, ln_w, wq, kv, wo, layer, ts=256):
    S, D = h.shape
    M = kv.shape[0]
    return pl.pallas_call(
        functools.partial(_xattn_body, layer),
        grid=(S // ts,),
        in_specs=[pl.BlockSpec((ts, D), lambda i: (i, 0)),
                  pl.BlockSpec((1, D), lambda i: (0, 0)),
                  pl.BlockSpec((M, D), lambda i: (0, 0)),
                  pl.BlockSpec((M, D), lambda i: (0, 1)),
                  pl.BlockSpec(memory_space=pl.ANY),
                  pl.BlockSpec(memory_space=pl.ANY)],
        out_specs=pl.BlockSpec((ts, D), lambda i: (i, 0)),
        out_shape=jax.ShapeDtypeStruct((S, D), F32),
        scratch_shapes=[pltpu.VMEM((D, D), BF16), pltpu.VMEM((D, D), BF16),
                        pltpu.VMEM((2, WEIGHT_STAGE_ROWS, D), F32), pltpu.SemaphoreType.DMA((2,))],
        compiler_params=_params("arbitrary"),
        name="xattn_layer",
    )(h, ln_w.reshape(1, D), kv, kv, wq, wo)


def _router_body(h_ref, lnw_ref, whi_ref, wlo_ref, b_ref, hn_ref, r_ref, rt_ref, cnt_ref):
    i = pl.program_id(0)

    @pl.when(i == 0)
    def _():
        cnt_ref[...] = jnp.zeros_like(cnt_ref)

    hn = _rms(h_ref[...], lnw_ref[...])
    _store_slabs(hn_ref, hn)
    xh, xl = _split(hn)
    whi = whi_ref[...]
    lg = _dot(xh, whi) + (_dot(xh, wlo_ref[...]) + _dot(xl, whi)) + b_ref[...]
    ts = lg.shape[0]
    lane = lax.broadcasted_iota(jnp.int32, (ts, LANES), 1)
    lanef = lane.astype(F32)
    neg = jnp.float32(-jnp.inf)
    gl = jnp.where(lane < N_GROUPS, lg, neg)
    gmax = jnp.max(gl, axis=-1, keepdims=True)
    pg_top = 1.0 / jnp.sum(jnp.exp(gl - gmax), axis=-1, keepdims=True)
    g_idx = jnp.min(jnp.where(gl == gmax, lanef, float(LANES)), axis=-1, keepdims=True)
    lo_lane = g_idx * EXPERTS_PER_GROUP + N_GROUPS
    el = jnp.where(lanef >= lo_lane, jnp.where(lanef < lo_lane + EXPERTS_PER_GROUP, lg, neg), neg)
    m1 = jnp.max(el, axis=-1, keepdims=True)
    i1 = jnp.min(jnp.where(el == m1, lanef, float(LANES)), axis=-1, keepdims=True)
    el2 = jnp.where(lanef == i1, neg, el)
    m2 = jnp.max(el2, axis=-1, keepdims=True)
    i2 = jnp.min(jnp.where(el2 == m2, lanef, float(LANES)), axis=-1, keepdims=True)
    e2 = jnp.exp(m2 - m1)
    g1 = pg_top / (1.0 + e2)
    g2 = pg_top * e2 / (1.0 + e2)
    hit1 = (lanef == i1).astype(F32)
    hit2 = (lanef == i2).astype(F32)
    hits = hit1 + hit2
    earlier = (lax.broadcasted_iota(jnp.int32, (ts, ts), 1)
               < lax.broadcasted_iota(jnp.int32, (ts, ts), 0)).astype(F32).astype(BF16)
    before = _dot(earlier, hits.astype(BF16)) + cnt_ref[...]
    r1 = jnp.sum(before * hit1, axis=-1, keepdims=True)
    r2 = jnp.sum(before * hit2, axis=-1, keepdims=True)
    cnt_ref[...] += jnp.sum(hits, axis=0, keepdims=True)
    out = jnp.where(lane == 0, g1, 0.0)
    out = jnp.where(lane == 1, g2, out)
    out = jnp.where(lane == 2, i1 - N_GROUPS, out)
    out = jnp.where(lane == 3, i2 - N_GROUPS, out)
    out = jnp.where(lane == 4, r1, out)
    out = jnp.where(lane == 5, r2, out)
    r_ref[...] = out
    rt_ref[...] = out.T[:SUBLANES, :]


def _router(h, ln_w, wg, bg, we, be, ts=512):
    S, D = h.shape
    unused = LANES - N_GROUPS - N_EXPERTS
    w = jnp.concatenate([wg, we, jnp.zeros((D, unused), F32)], axis=1)
    b = jnp.concatenate([bg, be, jnp.zeros((unused,), F32)]).reshape(1, LANES)
    whi, wlo = _split(w)
    return pl.pallas_call(
        _router_body,
        grid=(S // ts,),
        in_specs=[pl.BlockSpec((ts, D), lambda i: (i, 0)),
                  pl.BlockSpec((1, D), lambda i: (0, 0)),
                  pl.BlockSpec((D, LANES), lambda i: (0, 0)),
                  pl.BlockSpec((D, LANES), lambda i: (0, 0)),
                  pl.BlockSpec((1, LANES), lambda i: (0, 0))],
        out_specs=[pl.BlockSpec((ts * SLAB, LANES), lambda i: (i, 0)),
                   pl.BlockSpec((ts, LANES), lambda i: (i, 0)),
                   pl.BlockSpec((SUBLANES, ts), lambda i: (0, i)),
                   pl.BlockSpec((1, LANES), lambda i: (0, 0))],
        out_shape=[jax.ShapeDtypeStruct((S * SLAB, LANES), F32), jax.ShapeDtypeStruct((S, LANES), F32),
                   jax.ShapeDtypeStruct((SUBLANES, S), F32), jax.ShapeDtypeStruct((1, LANES), F32)],
        compiler_params=_params("arbitrary"),
        name="moe_router",
    )(h, ln_w.reshape(1, D), whi, wlo, b)


def _dispatch_body(pos_ref, fill_lo_ref, fill_hi_ref, nblk_ref, hn_ref, xs_hbm, stage, zeros, sem, zsem):
    i = pl.program_id(0)
    n = pl.num_programs(0)
    tb = hn_ref.shape[0] // SLAB
    n_tok = n * tb
    slot = i % 2

    def wait_step(s):
        for _ in range(TOP_K):
            pltpu.make_async_copy(stage.at[s], xs_hbm.at[pl.ds(0, tb * SLAB), :], sem.at[s]).wait()

    @pl.when(i >= 2)
    def _():
        wait_step(slot)

    stage[slot] = hn_ref[...]

    def issue(r, carry):
        for k in range(TOP_K):
            pltpu.make_async_copy(_slab(stage.at[slot], r), _slab(xs_hbm, pos_ref[k * n_tok + i * tb + r]),
                                  sem.at[slot]).start(priority=k % 2)
        return carry
    lax.fori_loop(0, tb, issue, 0, unroll=ISSUE_UNROLL)

    @pl.when(jnp.logical_and(i == n - 1, n >= 2))
    def _():
        wait_step(1 - slot)

    @pl.when(i == n - 1)
    def _():
        zeros[...] = jnp.zeros_like(zeros)

        def zero_row(row):
            return pltpu.make_async_copy(zeros.at[pl.ds(0, SLAB), :], _slab(xs_hbm, row), zsem)

        def zero_block(b):
            return pltpu.make_async_copy(zeros, _slab_block(xs_hbm, b), zsem)

        def per_expert(e, carry):
            lo, hi = fill_lo_ref[e], fill_hi_ref[e]
            lax.fori_loop(lo, hi, lambda row, c: (zero_row(row).start(), c)[1], 0)
            lax.fori_loop(lo, hi, lambda row, c: (zero_row(row).wait(), c)[1], 0)
            return carry
        lax.fori_loop(0, N_EXPERTS, per_expert, 0)
        n_blocks = xs_hbm.shape[0] // (ROW_BLOCK * SLAB)
        lax.fori_loop(nblk_ref[0], n_blocks, lambda b, c: (zero_block(b).start(), c)[1], 0)
        lax.fori_loop(nblk_ref[0], n_blocks, lambda b, c: (zero_block(b).wait(), c)[1], 0)
        wait_step(slot)


def _slab_block(ref, block):
    rows = ROW_BLOCK * SLAB
    return ref.at[pl.ds(pl.multiple_of(block * rows, rows), rows), :]


def _dispatch(pos, fill_lo, fill_hi, n_used, hn, n_rows, tb=256):
    S = hn.shape[0] // SLAB
    return pl.pallas_call(
        _dispatch_body,
        grid_spec=pltpu.PrefetchScalarGridSpec(
            num_scalar_prefetch=4,
            grid=(S // tb,),
            in_specs=[pl.BlockSpec((tb * SLAB, LANES), lambda i, *_: (i, 0))],
            out_specs=pl.BlockSpec(memory_space=pl.ANY),
            scratch_shapes=[pltpu.VMEM((2, tb * SLAB, LANES), F32), pltpu.VMEM((ROW_BLOCK * SLAB, LANES), F32),
                            pltpu.SemaphoreType.DMA((2,)), pltpu.SemaphoreType.DMA(())]),
        out_shape=jax.ShapeDtypeStruct((n_rows * SLAB, LANES), F32),
        compiler_params=_params("arbitrary"),
        name="moe_dispatch",
    )(pos, fill_lo, fill_hi, n_used, hn)


def _experts_body(layer, nblk_ref, be_ref, fresh_ref, wslot_ref, next_e_ref, x_ref, wgu_hbm, wd_hbm, o_ref,
                  wgu_f, wd_f, wgu_b, wd_b, wsem):
    i = pl.program_id(0)
    n_used = nblk_ref[0]

    def weight_copies(e, slot):
        return (pltpu.make_async_copy(wgu_hbm.at[layer, e], wgu_f.at[slot], wsem.at[0, slot]),
                pltpu.make_async_copy(wd_hbm.at[layer, e], wd_f.at[slot], wsem.at[1, slot]))

    @pl.when(i == 0)
    def _():
        for cp in weight_copies(be_ref[0], 0):
            cp.start()

    @pl.when(jnp.logical_and(i < n_used, fresh_ref[i] == 1))
    def _():
        slot = wslot_ref[i]
        for cp in weight_copies(be_ref[i], slot):
            cp.wait()
        wgu_b[...] = wgu_f[slot].astype(BF16)
        wd_b[...] = wd_f[slot].astype(BF16)

        @pl.when(next_e_ref[i] >= 0)
        def _():
            for cp in weight_copies(next_e_ref[i], 1 - slot):
                cp.start()

    @pl.when(i < n_used)
    def _():
        gu = _dot(_load_slabs(x_ref).astype(BF16), wgu_b[...])
        gg = gu[:, :D_EXPERT]
        uu = gu[:, D_EXPERT:]
        act = (gg * (1.0 / (1.0 + jnp.exp(-gg))) * uu).astype(BF16)
        _store_slabs(o_ref, _dot(act, wd_b[...]))

    @pl.when(i >= n_used)
    def _():
        o_ref[...] = jnp.zeros_like(o_ref)


def _experts(n_used, block_e, xs, w_gu, w_down, layer):
    rows = ROW_BLOCK * SLAB
    n_blocks = xs.shape[0] // rows
    D = w_gu.shape[2]
    F2 = w_gu.shape[3]
    idx = jnp.arange(n_blocks, dtype=jnp.int32)
    used = idx < n_used[0]
    fresh = jnp.logical_and(used, jnp.concatenate([jnp.ones((1,), bool), block_e[1:] != block_e[:-1]]))
    run = jnp.cumsum(fresh.astype(jnp.int32)) - 1
    run_start = jnp.where(fresh, idx, n_blocks)
    next_start = lax.cummin(jnp.concatenate([run_start[1:], jnp.full((1,), n_blocks, jnp.int32)]), reverse=True)
    next_e = jnp.where(next_start < n_blocks, block_e[jnp.minimum(next_start, n_blocks - 1)], -1)
    return pl.pallas_call(
        functools.partial(_experts_body, layer),
        grid_spec=pltpu.PrefetchScalarGridSpec(
            num_scalar_prefetch=5,
            grid=(n_blocks,),
            in_specs=[pl.BlockSpec((rows, LANES), lambda i, nb, *_: (jnp.minimum(i, nb[0] - 1), 0)),
                      pl.BlockSpec(memory_space=pl.ANY),
                      pl.BlockSpec(memory_space=pl.ANY)],
            out_specs=pl.BlockSpec((rows, LANES), lambda i, *_: (i, 0)),
            scratch_shapes=[pltpu.VMEM((2, D, F2), F32), pltpu.VMEM((2, F2 // 2, D), F32),
                            pltpu.VMEM((D, F2), BF16), pltpu.VMEM((F2 // 2, D), BF16),
                            pltpu.SemaphoreType.DMA((2, 2))]),
        out_shape=jax.ShapeDtypeStruct(xs.shape, F32),
        compiler_params=_params("arbitrary"),
        name="moe_experts",
    )(n_used, block_e, fresh.astype(jnp.int32), (run % 2).astype(jnp.int32), next_e.astype(jnp.int32),
      xs, w_gu, w_down)


def _start_pair_gather(pos_ref, first_tok, n_tok, yb_hbm, ybuf, slot, sem, tb):
    def issue(r, carry):
        for k in range(TOP_K):
            pltpu.make_async_copy(_slab(yb_hbm, pos_ref[k * n_tok + first_tok + r]), _slab(ybuf.at[slot, k], r, SLAB_PITCH),
                                  sem.at[slot]).start(priority=k % 2)
        return carry
    lax.fori_loop(0, tb, issue, 0, unroll=ISSUE_UNROLL)


def _wait_pair_gather(yb_hbm, ybuf, slot, sem, tb):
    for k in range(TOP_K):
        pltpu.make_async_copy(yb_hbm.at[pl.ds(0, tb * SLAB), :], ybuf.at[slot, k, pl.ds(0, tb * SLAB), :],
                              sem.at[slot]).wait()


def _combine_body(final_norm, pos_ref, yb_hbm, h_ref, r_ref, lnw_ref, o_ref, ybuf, sem):
    i = pl.program_id(0)
    n = pl.num_programs(0)
    tb = h_ref.shape[0]
    slot = i % 2

    @pl.when(i == 0)
    def _():
        _start_pair_gather(pos_ref, 0, n * tb, yb_hbm, ybuf, 0, sem, tb)

    @pl.when(i + 1 < n)
    def _():
        _start_pair_gather(pos_ref, (i + 1) * tb, n * tb, yb_hbm, ybuf, 1 - slot, sem, tb)

    _wait_pair_gather(yb_hbm, ybuf, slot, sem, tb)
    out = h_ref[...]
    for k in range(TOP_K):
        out = out + _load_slabs(ybuf.at[slot, k], SLAB_PITCH) * r_ref[:, k:k + 1]
    if final_norm:
        out = _rms(out, lnw_ref[...])
    o_ref[...] = out


def _combine(pos, yb, h, routed, ln_out_w, final_norm, tb=128):
    S, D = h.shape
    return pl.pallas_call(
        functools.partial(_combine_body, final_norm),
        grid_spec=pltpu.PrefetchScalarGridSpec(
            num_scalar_prefetch=1,
            grid=(S // tb,),
            in_specs=[pl.BlockSpec(memory_space=pl.ANY),
                      pl.BlockSpec((tb, D), lambda i, pos: (i, 0)),
                      pl.BlockSpec((tb, LANES), lambda i, pos: (i, 0)),
                      pl.BlockSpec((1, D), lambda i, pos: (0, 0))],
            out_specs=pl.BlockSpec((tb, D), lambda i, pos: (i, 0)),
            scratch_shapes=[pltpu.VMEM((2, TOP_K, tb * SLAB_PITCH, LANES), F32), pltpu.SemaphoreType.DMA((2,))]),
        out_shape=jax.ShapeDtypeStruct((S, D), F32),
        compiler_params=_params("arbitrary"),
        name="moe_combine",
    )(pos, yb, h, routed, ln_out_w.reshape(1, D))


def _moe_layer(h, ln_w, wg, bg, we, be, w_gu, w_down, layer, ln_out_w, final_norm):
    T, D = h.shape
    hn, routed, routed_t, counts = _router(h, ln_w, wg, bg, we, be)
    A = T * TOP_K
    n_blocks = -(-A // ROW_BLOCK) + N_EXPERTS
    n_rows = n_blocks * ROW_BLOCK
    counts = counts[0, N_GROUPS:N_GROUPS + N_EXPERTS].astype(jnp.int32)
    padded = (counts + ROW_BLOCK - 1) // ROW_BLOCK * ROW_BLOCK
    pad_end = jnp.cumsum(padded)
    pad_start = pad_end - padded
    expert_idx = routed_t[2:2 + TOP_K].astype(jnp.int32)
    slot = routed_t[4:4 + TOP_K].astype(jnp.int32)
    is_e = expert_idx[..., None] == jnp.arange(N_EXPERTS, dtype=jnp.int32)
    pos = (jnp.sum(jnp.where(is_e, pad_start, 0), axis=-1) + slot).reshape(A)
    block_start = jnp.arange(n_blocks, dtype=jnp.int32) * ROW_BLOCK
    block_e = jnp.minimum(jnp.sum(pad_end[None, :] <= block_start[:, None], axis=1), N_EXPERTS - 1).astype(jnp.int32)
    n_used = (pad_end[-1] // ROW_BLOCK).astype(jnp.int32).reshape(1)
    xs = _dispatch(pos, pad_start + counts, pad_end, n_used, hn, n_rows)
    yb = _experts(n_used, block_e, xs, w_gu, w_down, layer)
    return _combine(pos, yb, h, routed, ln_out_w, final_norm)


def _sigmoid(x):
    return 1.0 / (1.0 + jnp.exp(-x))


def _rwkv_prep_body(h_ref, halo_ref, lnw_ref, mu_ref, w0_ref, w1_ref, w2_ref, a0_ref, a1_ref, a2_ref, g1_ref,
                    g2_ref, xs_ref, lw_ref, a_ref, g_ref):
    i = pl.program_id(0)
    lnw = lnw_ref[...]
    hn = _rms(h_ref[...], lnw)
    ts = hn.shape[0]
    last = _rms(halo_ref[...], lnw)[SUBLANES - 1:SUBLANES, :] * (i > 0).astype(F32)
    row = lax.broadcasted_iota(jnp.int32, (ts, 1), 0)
    prev = jnp.where(row == 0, last, pltpu.roll(hn, 1, axis=0))
    xx = prev - hn
    mix = lambda n: (hn + xx * mu_ref[n:n + 1, :]).astype(BF16)
    n_out = xs_ref.shape[0]
    for n in range(n_out):
        xs_ref[n] = mix(n)
    z = w0_ref[...] + _dot(jnp.tanh(_dot(mix(n_out), w1_ref[...])).astype(BF16), w2_ref[...])
    u = -z
    softplus = jnp.maximum(u, 0.0) + jnp.log1p(jnp.exp(-jnp.abs(u)))
    lw_ref[...] = -jnp.exp(-softplus - 0.5)
    a_ref[...] = _sigmoid(a0_ref[...] + _dot(_dot(mix(n_out + 1), a1_ref[...]).astype(BF16), a2_ref[...]))
    g_ref[...] = _dot(_sigmoid(_dot(mix(n_out + 2), g1_ref[...])).astype(BF16), g2_ref[...])


def _rwkv_prep(h, ln_w, mu, w0, w1, w2, a0, a1, a2, g1, g2, ts=256):
    S, D = h.shape
    n_mix = mu.shape[0]
    n_out = n_mix - 3
    halo_blocks = ts // SUBLANES

    def pad_lora(wa, wb):
        r = wa.shape[1]
        rp = -(-r // LANES) * LANES
        return (jnp.pad(wa, ((0, 0), (0, rp - r))).astype(BF16), jnp.pad(wb, ((0, rp - r), (0, 0))).astype(BF16))

    w1p, w2p = pad_lora(w1, w2)
    a1p, a2p = pad_lora(a1, a2)
    g1p, g2p = pad_lora(g1, g2)
    full = lambda arr: pl.BlockSpec(arr.shape, lambda i: (0,) * arr.ndim)
    vec = pl.BlockSpec((1, D), lambda i: (0, 0))
    row = pl.BlockSpec((ts, D), lambda i: (i, 0))
    return pl.pallas_call(
        _rwkv_prep_body,
        grid=(S // ts,),
        in_specs=[row,
                  pl.BlockSpec((SUBLANES, D), lambda i: (jnp.maximum(i * halo_blocks - 1, 0), 0)),
                  vec, full(mu), vec, full(w1p), full(w2p), vec, full(a1p), full(a2p), full(g1p), full(g2p)],
        out_specs=[pl.BlockSpec((n_out, ts, D), lambda i: (0, i, 0)), row, row, row],
        out_shape=[jax.ShapeDtypeStruct((n_out, S, D), BF16)] + [jax.ShapeDtypeStruct((S, D), F32)] * 3,
        compiler_params=_params("arbitrary"),
        name="rwkv_prep",
    )(h, h, ln_w.reshape(1, D), mu, w0.reshape(1, D), w1p, w2p, a0.reshape(1, D), a1p, a2p, g1p, g2p)


def _scan_chunk(r, k, v, lw, a, g, kkw, kaw, rkw, lnw, lnb, state, c):
    masks, seg_f, seg_b, tri, strict, incl, eye = c
    bf = lambda x: x.astype(BF16)
    stack = lambda xb: jnp.concatenate([xb * m for m in masks], axis=0)
    rows = lambda *xs: jnp.concatenate(xs, axis=0)
    cols = lambda *xs: jnp.concatenate(xs, axis=1)

    kk = k * kkw
    k2 = k * (1.0 + (a - 1.0) * kaw)
    sums = _segsum(rows(kk * kk, r * k2 * rkw), seg_b)
    cw = _cumsum_rows(tri, lw)
    yield
    kk = kk / jnp.maximum(jnp.sqrt(sums[:CHUNK]), 1e-12)
    bonus = sums[CHUNK:] * v
    bv = kk * a
    cl = cw[CHUNK - 1:CHUNK, :]
    at = bf(-kk * jnp.exp(cw - lw))
    rt = bf(r * jnp.exp(cw))
    dinv = jnp.exp(-cw)
    drem = jnp.exp(cl - cw)
    vb = bf(v)
    vs = stack(vb)
    gram = _dot(rows(at, rt), rows(stack(bf(bv * dinv)), stack(bf(k2 * dinv))), NT)
    bk_t = bf(rows(bv * drem, k2 * drem).T)
    decay_rows = jnp.broadcast_to(jnp.exp(cl), (GROUP, GROUP)).T
    yield
    a_ab = jnp.where(strict, gram[:CHUNK, :GROUP], 0.0)
    a_ak = bf(jnp.where(strict, gram[:CHUNK, GROUP:], 0.0))
    a_rb = bf(jnp.where(incl, gram[CHUNK:, :GROUP], 0.0))
    a_rk = bf(jnp.where(incl, gram[CHUNK:, GROUP:], 0.0))
    inv = eye + a_ab
    pw = _dot(bf(a_ab), stack(bf(a_ab)))
    xy = _dot(rows(cols(at, a_ak), cols(rt, a_rk)), rows(bf(state), vs))
    yield
    span = 2
    while span < CHUNK // 2:
        pwb = bf(pw)
        both = _dot(rows(bf(inv), pwb), stack(pwb))
        yield
        inv = inv + both[:CHUNK]
        pw = both[CHUNK:]
        span *= 2
    inv = inv + _dot(bf(inv), stack(bf(pw)))
    yield
    ub = bf(_dot(bf(inv), stack(bf(xy[:CHUNK]))))
    yield
    y = xy[CHUNK:] + _dot(a_rb, stack(ub))
    new_state = state * decay_rows + seg_f * _dot(bk_t, rows(ub, vb))
    yield
    inv_n = 1.0 / RWKV_HEAD_DIM
    mean = _segsum(y, seg_b) * inv_n
    yield
    yc = y - mean
    var = _segsum(yc * yc, seg_b) * inv_n
    yield
    yn = yc * lax.rsqrt(var + GN_EPS) * lnw + lnb
    return ((yn + bonus) * g), new_state


def _run_interleaved(gens):
    results = [None] * len(gens)
    live = list(range(len(gens)))
    while live:
        for i in list(live):
            try:
                next(gens[i])
            except StopIteration as done:
                results[i] = done.value
                live.remove(i)
    return results


def _segsum(x, seg_b):
    return _dot(x.astype(BF16), seg_b)


def _cumsum_rows(tri, x):
    hi = x.astype(BF16)
    r1 = x - hi.astype(F32)
    mid = r1.astype(BF16)
    lo = (r1 - mid.astype(F32)).astype(BF16)
    return _dot(jnp.concatenate([tri, tri, tri], axis=1), jnp.concatenate([hi, mid, lo], axis=0))


def _scan_consts():
    shift = RWKV_HEAD_DIM.bit_length() - 1
    lane_head = lax.broadcasted_iota(jnp.int32, (1, GROUP), 1) >> shift
    masks = [(lane_head == h).astype(F32).astype(BF16) for h in range(GROUP_HEADS)]
    ri = lax.broadcasted_iota(jnp.int32, (GROUP, GROUP), 0) >> shift
    ci = lax.broadcasted_iota(jnp.int32, (GROUP, GROUP), 1) >> shift
    seg_f = (ri == ci).astype(F32)
    tri = (lax.broadcasted_iota(jnp.int32, (CHUNK, CHUNK), 1)
           <= lax.broadcasted_iota(jnp.int32, (CHUNK, CHUNK), 0)).astype(F32).astype(BF16)
    t = lax.broadcasted_iota(jnp.int32, (CHUNK, GROUP), 0)
    s = lax.broadcasted_iota(jnp.int32, (CHUNK, GROUP), 1) & (RWKV_HEAD_DIM - 1)
    return (masks, seg_f, seg_f.astype(BF16), tri, s < t, s <= t, (s == t).astype(F32))


def _scan_body(r_ref, k_ref, v_ref, lw_ref, a_ref, g_ref, kkw_ref, kaw_ref, rkw_ref, lnw_ref, lnb_ref,
               o_ref, state_ref):
    @pl.when(pl.program_id(1) == 0)
    def _():
        state_ref[...] = jnp.zeros_like(state_ref)

    n_groups = o_ref.shape[1] // GROUP

    def chunk_step(j, carry):
        consts = _scan_consts()
        rows = pl.ds(pl.multiple_of(j * CHUNK, CHUNK), CHUNK)
        lanes = [slice(p * GROUP, (p + 1) * GROUP) for p in range(n_groups)]
        args = [(r_ref[rows, ln], k_ref[rows, ln], v_ref[rows, ln], lw_ref[rows, ln], a_ref[rows, ln],
                 g_ref[rows, ln], kkw_ref[:, ln], kaw_ref[:, ln], rkw_ref[:, ln], lnw_ref[:, ln],
                 lnb_ref[:, ln], state_ref[p]) for p, ln in enumerate(lanes)]
        results = _run_interleaved([_scan_chunk(*a, consts) for a in args])
        for p, (ln, (out, new_state)) in enumerate(zip(lanes, results)):
            state_ref[p] = new_state
            o_ref[rows, ln] = out.astype(o_ref.dtype)
        return carry

    lax.fori_loop(0, o_ref.shape[0] // CHUNK, chunk_step, 0)


def _rwkv_scan(rkv, lw, a, g, k_k, k_a, r_k, ln_w, ln_b):
    _, S, D = rkv.shape
    tr, tl = min(SCAN_ROWS, S), SCAN_LANES
    blk = lambda n: pl.BlockSpec((None, tr, tl), lambda p, i: (n, i, p))
    row = pl.BlockSpec((tr, tl), lambda p, i: (i, p))
    vec = pl.BlockSpec((1, tl), lambda p, i: (0, p))
    as_row = lambda w: w.reshape(1, D)
    return pl.pallas_call(
        _scan_body,
        grid=(D // tl, S // tr),
        in_specs=[blk(0), blk(1), blk(2), row, row, row, vec, vec, vec, vec, vec],
        out_specs=row,
        out_shape=jax.ShapeDtypeStruct((S, D), BF16),
        scratch_shapes=[pltpu.VMEM((tl // GROUP, GROUP, GROUP), F32)],
        compiler_params=_params("arbitrary", "arbitrary"),
        name="rwkv_scan",
    )(rkv, rkv, rkv, lw, a, g, as_row(k_k), as_row(k_a), as_row(r_k), as_row(ln_w), as_row(ln_b))


def _rwkv_layer(h, ln_w, mu, w_rkv, w_o, j, w0, w1, w2, a0, a1, a2, g1, g2, k_k, k_a, r_k, ln_gn_w, ln_gn_b):
    xs, lw, a, g = _rwkv_prep(h, ln_w, mu, w0, w1, w2, a0, a1, a2, g1, g2)
    rkv = _rkv_proj(xs, w_rkv, j)
    z = _rwkv_scan(rkv, lw, a, g, k_k, k_a, r_k, ln_gn_w, ln_gn_b)
    return _out_proj(z, w_o, j, h)


def kernel(x, mem, ln_mix_w, ln_xattn_w, ln_mem_w, xattn_wq, xattn_wkv, xattn_wo, ln_ffn_w, router_group_w, router_group_b, router_expert_w, router_expert_b, moe_w_gate_up, moe_w_down, pool_w, pool_scale, rwkv_mu, rwkv_w_rkv, rwkv_w_o, rwkv_w0, rwkv_w1, rwkv_w2, rwkv_a0, rwkv_a1, rwkv_a2, rwkv_g1, rwkv_g2, rwkv_k_k, rwkv_k_a, rwkv_r_k, rwkv_ln_w, rwkv_ln_b, ln_out_w):
    B, S, D = x.shape
    depth = ln_mix_w.shape[0]
    outs = []
    for b in range(B):
        h = x[b]
        for i in range(depth):
            j = i // 2
            if i % 2 == 0:
                h = _pool_layer(h, ln_mix_w[i], pool_w[j], pool_scale[j])
            else:
                h = _rwkv_layer(h, ln_mix_w[i], rwkv_mu[j], rwkv_w_rkv, rwkv_w_o, j, rwkv_w0[j], rwkv_w1[j],
                                rwkv_w2[j], rwkv_a0[j], rwkv_a1[j], rwkv_a2[j], rwkv_g1[j], rwkv_g2[j],
                                rwkv_k_k[j], rwkv_k_a[j], rwkv_r_k[j].reshape(D), rwkv_ln_w[j], rwkv_ln_b[j])
            kv = _mem_kv(mem[b], ln_mem_w[i], xattn_wkv, i)
            h = _xattn_layer(h, ln_xattn_w[i], xattn_wq, kv, xattn_wo, i)
            h = _moe_layer(h, ln_ffn_w[i], router_group_w[i], router_group_b[i], router_expert_w[i],
                           router_expert_b[i], moe_w_gate_up, moe_w_down, i, ln_out_w,
                           final_norm=(i == depth - 1))
        outs.append(h)
    return outs[0][None] if B == 1 else jnp.stack(outs, axis=0)
```

```python
import functools

import jax
import jax.numpy as jnp
from jax import lax
from jax.experimental import pallas as pl
from jax.experimental.pallas import tpu as pltpu

F32 = jnp.float32
BF16 = jnp.bfloat16

D_MODEL = 2048
POOL_WINDOWS = (2, 4, 8, 16)
POOL_GROUP_DIM = D_MODEL // len(POOL_WINDOWS)
MAX_WINDOW = max(POOL_WINDOWS)
RWKV_HEAD_DIM = 64
GN_EPS = 64e-5
XATTN_HEADS = 4
XATTN_HEAD_DIM = D_MODEL // XATTN_HEADS
N_GROUPS = 8
EXPERTS_PER_GROUP = 8
N_EXPERTS = N_GROUPS * EXPERTS_PER_GROUP
TOP_K = 2
D_EXPERT = D_MODEL // 8
ROW_BLOCK = 128
RMS_EPS = 1e-6

LANES = 128
SUBLANES = 8
VMEM_LIMIT = 56 * 1024 * 1024

CHUNK = 64
GROUP_HEADS = 4
GROUP = GROUP_HEADS * RWKV_HEAD_DIM
SLAB = D_MODEL // LANES
SLAB_PITCH = SLAB + SUBLANES
ISSUE_UNROLL = 8
WEIGHT_STAGE_ROWS = 256
SCAN_ROWS = 256
SCAN_LANES = 2048

NN = (((1,), (0,)), ((), ()))
NT = (((1,), (1,)), ((), ()))


def _params(*sem):
    return pltpu.CompilerParams(dimension_semantics=sem, vmem_limit_bytes=VMEM_LIMIT)


def _rms(x, w):
    return x * lax.rsqrt(jnp.mean(x * x, axis=-1, keepdims=True) + RMS_EPS) * w


def _dot(a, b, dims=NN):
    return lax.dot_general(a, b, dims, preferred_element_type=F32)


def _split(x):
    hi = x.astype(BF16)
    return hi, (x - hi.astype(F32)).astype(BF16)


def _store_slabs(ref, x):
    for j in range(SLAB):
        ref[pl.ds(j, x.shape[0], stride=SLAB), :] = x[:, j * LANES:(j + 1) * LANES]


def _load_slabs(ref, pitch=SLAB):
    n = ref.shape[0] // pitch
    return jnp.concatenate([ref[pl.ds(j, n, stride=pitch), :] for j in range(SLAB)], axis=-1)


def _slab(ref, row, pitch=SLAB):
    return ref.at[pl.ds(pl.multiple_of(row * pitch, SUBLANES), SLAB), :]


def _rkv_proj_body(x_ref, w_ref, o_ref, wb_ref):
    @pl.when(pl.program_id(2) == 0)
    def _():
        wb_ref[...] = w_ref[...].astype(BF16)

    o_ref[...] = _dot(x_ref[...], wb_ref[...])


def _rkv_proj(xs, w_rkv, j, tm=512, tn=1024):
    _, M, K = xs.shape
    B, N = w_rkv.shape[1], w_rkv.shape[3]
    return pl.pallas_call(
        _rkv_proj_body,
        grid=(B, N // tn, M // tm),
        in_specs=[pl.BlockSpec((None, tm, K), lambda b, n, i: (b, i, 0)),
                  pl.BlockSpec((None, None, K, tn), lambda b, n, i: (j, b, 0, n))],
        out_specs=pl.BlockSpec((None, tm, tn), lambda b, n, i: (b, i, n)),
        out_shape=jax.ShapeDtypeStruct((B, M, N), F32),
        scratch_shapes=[pltpu.VMEM((K, tn), BF16)],
        compiler_params=_params("arbitrary", "arbitrary", "arbitrary"),
        name="rwkv_rkv_proj",
    )(xs, w_rkv)


def _out_proj_body(x_ref, w_ref, r_ref, o_ref, wb_ref):
    @pl.when(pl.program_id(1) == 0)
    def _():
        wb_ref[...] = w_ref[...].astype(BF16)

    o_ref[...] = r_ref[...] + _dot(x_ref[...], wb_ref[...])


def _out_proj(x, w_o, j, res, tm=512, tn=1024):
    M, K = x.shape
    N = w_o.shape[2]
    return pl.pallas_call(
        _out_proj_body,
        grid=(N // tn, M // tm),
        in_specs=[pl.BlockSpec((tm, K), lambda n, i: (i, 0)),
                  pl.BlockSpec((None, K, tn), lambda n, i: (j, 0, n)),
                  pl.BlockSpec((tm, tn), lambda n, i: (i, n))],
        out_specs=pl.BlockSpec((tm, tn), lambda n, i: (i, n)),
        out_shape=jax.ShapeDtypeStruct((M, N), F32),
        scratch_shapes=[pltpu.VMEM((K, tn), BF16)],
        compiler_params=_params("arbitrary", "arbitrary"),
        name="rwkv_out_proj",
    )(x, w_o, res)


def _pool_body(h_ref, halo_ref, lnw_ref, pw_ref, ps_ref, o_ref):
    i = pl.program_id(0)
    ts = h_ref.shape[0]
    h = h_ref[...]
    lnw = lnw_ref[...]
    x = _rms(h, lnw)
    xh = _rms(halo_ref[...], lnw) * (i > 0).astype(F32)
    xe = jnp.concatenate([xh, x], axis=0)
    t = i * ts + lax.broadcasted_iota(jnp.int32, (ts, 1), 0)
    outs = []
    for g, win in enumerate(POOL_WINDOWS):
        sl = slice(g * POOL_GROUP_DIM, (g + 1) * POOL_GROUP_DIM)
        acc = xe[:, sl]
        span = 1
        while span < win:
            acc = acc + pltpu.roll(acc, span, axis=0)
            span *= 2
        cnt = jnp.minimum(t + 1, win).astype(F32)
        pooled = acc[MAX_WINDOW:, :] / cnt - x[:, sl]
        outs.append(_dot(pooled.astype(BF16), pw_ref[g]))
    y = jnp.concatenate(outs, axis=-1)
    o_ref[...] = h + y * ps_ref[...]


def _pool_layer(h, ln_w, pool_w, pool_scale, ts=512):
    S, D = h.shape
    G = pool_w.shape[0]
    halo_blocks = ts // MAX_WINDOW
    return pl.pallas_call(
        _pool_body,
        grid=(S // ts,),
        in_specs=[pl.BlockSpec((ts, D), lambda i: (i, 0)),
                  pl.BlockSpec((MAX_WINDOW, D), lambda i: (jnp.maximum(i * halo_blocks - 1, 0), 0)),
                  pl.BlockSpec((1, D), lambda i: (0, 0)),
                  pl.BlockSpec((G, POOL_GROUP_DIM, POOL_GROUP_DIM), lambda i: (0, 0, 0)),
                  pl.BlockSpec((1, D), lambda i: (0, 0))],
        out_specs=pl.BlockSpec((ts, D), lambda i: (i, 0)),
        out_shape=jax.ShapeDtypeStruct((S, D), F32),
        compiler_params=_params("arbitrary"),
        name="pool_layer",
    )(h, h, ln_w.reshape(1, D), pool_w.astype(BF16), pool_scale.reshape(1, D))


def _memkv_body(mem_ref, lnw_ref, w_ref, o_ref):
    mn = _rms(mem_ref[...], lnw_ref[...])
    o_ref[...] = _dot(mn.astype(BF16), w_ref[...].astype(BF16)).astype(o_ref.dtype)


def _mem_kv(mem, ln_w, wkv, layer, tn=1024):
    M, D = mem.shape
    N = wkv.shape[2]
    return pl.pallas_call(
        _memkv_body,
        grid=(N // tn,),
        in_specs=[pl.BlockSpec((M, D), lambda n: (0, 0)),
                  pl.BlockSpec((1, D), lambda n: (0, 0)),
                  pl.BlockSpec((None, D, tn), lambda n: (layer, 0, n))],
        out_specs=pl.BlockSpec((M, tn), lambda n: (0, n)),
        out_shape=jax.ShapeDtypeStruct((M, N), BF16),
        compiler_params=_params("arbitrary"),
        name="xattn_mem_kv",
    )(mem, ln_w.reshape(1, D), wkv)


def _load_weight_bf16(w_hbm, dst, stage, sem):
    rows = stage.shape[1]
    n_chunks = w_hbm.shape[0] // rows

    def chunk_copy(c):
        return pltpu.make_async_copy(w_hbm.at[pl.ds(c * rows, rows), :], stage.at[c % 2], sem.at[c % 2])

    chunk_copy(0).start()
    for c in range(n_chunks):
        if c + 1 < n_chunks:
            chunk_copy(c + 1).start()
        chunk_copy(c).wait()
        dst[pl.ds(c * rows, rows), :] = stage[c % 2].astype(BF16)


def _xattn_body(layer, h_ref, lnw_ref, k_ref, v_ref, wq_hbm, wo_hbm, o_ref, wq_ref, wo_ref, stage, sem):
    @pl.when(pl.program_id(0) == 0)
    def _():
        _load_weight_bf16(wq_hbm.at[layer], wq_ref, stage, sem)
        _load_weight_bf16(wo_hbm.at[layer], wo_ref, stage, sem)

    h = h_ref[...]
    hn = _rms(h, lnw_ref[...]).astype(BF16)
    q = _dot(hn, wq_ref[...])
    scale = XATTN_HEAD_DIM ** -0.5
    heads = []
    for hd in range(XATTN_HEADS):
        sl = slice(hd * XATTN_HEAD_DIM, (hd + 1) * XATTN_HEAD_DIM)
        s = _dot(q[:, sl].astype(BF16), k_ref[:, sl], NT) * scale
        s = s - jnp.max(s, axis=-1, keepdims=True)
        e = jnp.exp(s)
        p = e / jnp.sum(e, axis=-1, keepdims=True)
        heads.append(_dot(p.astype(BF16), v_ref[:, sl]))
    o = jnp.concatenate(heads, axis=-1).astype(BF16)
    o_ref[...] = h + _dot(o, wo_ref[...])


def _xattn_layer(h, ln_w, wq, kv, wo, layer, ts=512):
    S, D = h.shape
    M = kv.shape[0]
    return pl.pallas_call(
        functools.partial(_xattn_body, layer),
        grid=(S // ts,),
        in_specs=[pl.BlockSpec((ts, D), lambda i: (i, 0)),
                  pl.BlockSpec((1, D), lambda i: (0, 0)),
                  pl.BlockSpec((M, D), lambda i: (0, 0)),
                  pl.BlockSpec((M, D), lambda i: (0, 1)),
                  pl.BlockSpec(memory_space=pl.ANY),
                  pl.BlockSpec(memory_space=pl.ANY)],
        out_specs=pl.BlockSpec((ts, D), lambda i: (i, 0)),
        out_shape=jax.ShapeDtypeStruct((S, D), F32),
        scratch_shapes=[pltpu.VMEM((D, D), BF16), pltpu.VMEM((D, D), BF16),
                        pltpu.VMEM((2, WEIGHT_STAGE_ROWS, D), F32), pltpu.SemaphoreType.DMA((2,))],
        compiler_params=_params("arbitrary"),
        name="xattn_layer",
    )(h, ln_w.reshape(1, D), kv, kv, wq, wo)


def _router_body(h_ref, lnw_ref, whi_ref, wlo_ref, b_ref, hn_ref, r_ref, rt_ref, cnt_ref):
    i = pl.program_id(0)

    @pl.when(i == 0)
    def _():
        cnt_ref[...] = jnp.zeros_like(cnt_ref)

    hn = _rms(h_ref[...], lnw_ref[...])
    _store_slabs(hn_ref, hn)
    xh, xl = _split(hn)
    whi = whi_ref[...]
    lg = _dot(xh, whi) + (_dot(xh, wlo_ref[...]) + _dot(xl, whi)) + b_ref[...]
    ts = lg.shape[0]
    lane = lax.broadcasted_iota(jnp.int32, (ts, LANES), 1)
    lanef = lane.astype(F32)
    neg = jnp.float32(-jnp.inf)
    gl = jnp.where(lane < N_GROUPS, lg, neg)
    gmax = jnp.max(gl, axis=-1, keepdims=True)
    pg_top = 1.0 / jnp.sum(jnp.exp(gl - gmax), axis=-1, keepdims=True)
    g_idx = jnp.min(jnp.where(gl == gmax, lanef, float(LANES)), axis=-1, keepdims=True)
    lo_lane = g_idx * EXPERTS_PER_GROUP + N_GROUPS
    el = jnp.where(lanef >= lo_lane, jnp.where(lanef < lo_lane + EXPERTS_PER_GROUP, lg, neg), neg)
    m1 = jnp.max(el, axis=-1, keepdims=True)
    i1 = jnp.min(jnp.where(el == m1, lanef, float(LANES)), axis=-1, keepdims=True)
    el2 = jnp.where(lanef == i1, neg, el)
    m2 = jnp.max(el2, axis=-1, keepdims=True)
    i2 = jnp.min(jnp.where(el2 == m2, lanef, float(LANES)), axis=-1, keepdims=True)
    e2 = jnp.exp(m2 - m1)
    g1 = pg_top / (1.0 + e2)
    g2 = pg_top * e2 / (1.0 + e2)
    hit1 = (lanef == i1).astype(F32)
    hit2 = (lanef == i2).astype(F32)
    hits = hit1 + hit2
    earlier = (lax.broadcasted_iota(jnp.int32, (ts, ts), 1)
               < lax.broadcasted_iota(jnp.int32, (ts, ts), 0)).astype(F32).astype(BF16)
    before = _dot(earlier, hits.astype(BF16)) + cnt_ref[...]
    r1 = jnp.sum(before * hit1, axis=-1, keepdims=True)
    r2 = jnp.sum(before * hit2, axis=-1, keepdims=True)
    cnt_ref[...] += jnp.sum(hits, axis=0, keepdims=True)
    out = jnp.where(lane == 0, g1, 0.0)
    out = jnp.where(lane == 1, g2, out)
    out = jnp.where(lane == 2, i1 - N_GROUPS, out)
    out = jnp.where(lane == 3, i2 - N_GROUPS, out)
    out = jnp.where(lane == 4, r1, out)
    out = jnp.where(lane == 5, r2, out)
    r_ref[...] = out
    rt_ref[...] = out.T[:SUBLANES, :]


def _router(h, ln_w, wg, bg, we, be, ts=512):
    S, D = h.shape
    unused = LANES - N_GROUPS - N_EXPERTS
    w = jnp.concatenate([wg, we, jnp.zeros((D, unused), F32)], axis=1)
    b = jnp.concatenate([bg, be, jnp.zeros((unused,), F32)]).reshape(1, LANES)
    whi, wlo = _split(w)
    return pl.pallas_call(
        _router_body,
        grid=(S // ts,),
        in_specs=[pl.BlockSpec((ts, D), lambda i: (i, 0)),
                  pl.BlockSpec((1, D), lambda i: (0, 0)),
                  pl.BlockSpec((D, LANES), lambda i: (0, 0)),
                  pl.BlockSpec((D, LANES), lambda i: (0, 0)),
                  pl.BlockSpec((1, LANES), lambda i: (0, 0))],
        out_specs=[pl.BlockSpec((ts * SLAB, LANES), lambda i: (i, 0)),
                   pl.BlockSpec((ts, LANES), lambda i: (i, 0)),
                   pl.BlockSpec((SUBLANES, ts), lambda i: (0, i)),
                   pl.BlockSpec((1, LANES), lambda i: (0, 0))],
        out_shape=[jax.ShapeDtypeStruct((S * SLAB, LANES), F32), jax.ShapeDtypeStruct((S, LANES), F32),
                   jax.ShapeDtypeStruct((SUBLANES, S), F32), jax.ShapeDtypeStruct((1, LANES), F32)],
        compiler_params=_params("arbitrary"),
        name="moe_router",
    )(h, ln_w.reshape(1, D), whi, wlo, b)


def _dispatch_body(pos_ref, fill_lo_ref, fill_hi_ref, nblk_ref, hn_ref, xs_hbm, stage, zeros, sem, zsem):
    i = pl.program_id(0)
    n = pl.num_programs(0)
    tb = hn_ref.shape[0] // SLAB
    n_tok = n * tb
    slot = i % 2

    def wait_step(s):
        for _ in range(TOP_K):
            pltpu.make_async_copy(stage.at[s], xs_hbm.at[pl.ds(0, tb * SLAB), :], sem.at[s]).wait()

    @pl.when(i >= 2)
    def _():
        wait_step(slot)

    stage[slot] = hn_ref[...]

    def issue(r, carry):
        for k in range(TOP_K):
            pltpu.make_async_copy(_slab(stage.at[slot], r), _slab(xs_hbm, pos_ref[k * n_tok + i * tb + r]),
                                  sem.at[slot]).start(priority=k % 2)
        return carry
    lax.fori_loop(0, tb, issue, 0, unroll=ISSUE_UNROLL)

    @pl.when(jnp.logical_and(i == n - 1, n >= 2))
    def _():
        wait_step(1 - slot)

    @pl.when(i == n - 1)
    def _():
        zeros[...] = jnp.zeros_like(zeros)

        def zero_row(row):
            return pltpu.make_async_copy(zeros.at[pl.ds(0, SLAB), :], _slab(xs_hbm, row), zsem)

        def zero_block(b):
            return pltpu.make_async_copy(zeros, _slab_block(xs_hbm, b), zsem)

        def per_expert(e, carry):
            lo, hi = fill_lo_ref[e], fill_hi_ref[e]
            lax.fori_loop(lo, hi, lambda row, c: (zero_row(row).start(), c)[1], 0)
            lax.fori_loop(lo, hi, lambda row, c: (zero_row(row).wait(), c)[1], 0)
            return carry
        lax.fori_loop(0, N_EXPERTS, per_expert, 0)
        n_blocks = xs_hbm.shape[0] // (ROW_BLOCK * SLAB)
        lax.fori_loop(nblk_ref[0], n_blocks, lambda b, c: (zero_block(b).start(), c)[1], 0)
        lax.fori_loop(nblk_ref[0], n_blocks, lambda b, c: (zero_block(b).wait(), c)[1], 0)
        wait_step(slot)


def _slab_block(ref, block):
    rows = ROW_BLOCK * SLAB
    return ref.at[pl.ds(pl.multiple_of(block * rows, rows), rows), :]


def _dispatch(pos, fill_lo, fill_hi, n_used, hn, n_rows, tb=512):
    S = hn.shape[0] // SLAB
    return pl.pallas_call(
        _dispatch_body,
        grid_spec=pltpu.PrefetchScalarGridSpec(
            num_scalar_prefetch=4,
            grid=(S // tb,),
            in_specs=[pl.BlockSpec((tb * SLAB, LANES), lambda i, *_: (i, 0))],
            out_specs=pl.BlockSpec(memory_space=pl.ANY),
            scratch_shapes=[pltpu.VMEM((2, tb * SLAB, LANES), F32), pltpu.VMEM((ROW_BLOCK * SLAB, LANES), F32),
                            pltpu.SemaphoreType.DMA((2,)), pltpu.SemaphoreType.DMA(())]),
        out_shape=jax.ShapeDtypeStruct((n_rows * SLAB, LANES), F32),
        compiler_params=_params("arbitrary"),
        name="moe_dispatch",
    )(pos, fill_lo, fill_hi, n_used, hn)


def _experts_body(layer, nblk_ref, be_ref, fresh_ref, wslot_ref, next_e_ref, x_ref, wgu_hbm, wd_hbm, o_ref,
                  wgu_f, wd_f, wgu_b, wd_b, wsem):
    i = pl.program_id(0)
    n_used = nblk_ref[0]

    def weight_copies(e, slot):
        return (pltpu.make_async_copy(wgu_hbm.at[layer, e], wgu_f.at[slot], wsem.at[0, slot]),
                pltpu.make_async_copy(wd_hbm.at[layer, e], wd_f.at[slot], wsem.at[1, slot]))

    @pl.when(i == 0)
    def _():
        for cp in weight_copies(be_ref[0], 0):
            cp.start()

    @pl.when(jnp.logical_and(i < n_used, fresh_ref[i] == 1))
    def _():
        slot = wslot_ref[i]
        for cp in weight_copies(be_ref[i], slot):
            cp.wait()
        wgu_b[...] = wgu_f[slot].astype(BF16)
        wd_b[...] = wd_f[slot].astype(BF16)

        @pl.when(next_e_ref[i] >= 0)
        def _():
            for cp in weight_copies(next_e_ref[i], 1 - slot):
                cp.start()

    @pl.when(i < n_used)
    def _():
        gu = _dot(_load_slabs(x_ref).astype(BF16), wgu_b[...])
        gg = gu[:, :D_EXPERT]
        uu = gu[:, D_EXPERT:]
        act = (gg * (1.0 / (1.0 + jnp.exp(-gg))) * uu).astype(BF16)
        _store_slabs(o_ref, _dot(act, wd_b[...]))

    @pl.when(i >= n_used)
    def _():
        o_ref[...] = jnp.zeros_like(o_ref)


def _experts(n_used, block_e, xs, w_gu, w_down, layer):
    rows = ROW_BLOCK * SLAB
    n_blocks = xs.shape[0] // rows
    D = w_gu.shape[2]
    F2 = w_gu.shape[3]
    idx = jnp.arange(n_blocks, dtype=jnp.int32)
    used = idx < n_used[0]
    fresh = jnp.logical_and(used, jnp.concatenate([jnp.ones((1,), bool), block_e[1:] != block_e[:-1]]))
    run = jnp.cumsum(fresh.astype(jnp.int32)) - 1
    run_start = jnp.where(fresh, idx, n_blocks)
    next_start = lax.cummin(jnp.concatenate([run_start[1:], jnp.full((1,), n_blocks, jnp.int32)]), reverse=True)
    next_e = jnp.where(next_start < n_blocks, block_e[jnp.minimum(next_start, n_blocks - 1)], -1)
    return pl.pallas_call(
        functools.partial(_experts_body, layer),
        grid_spec=pltpu.PrefetchScalarGridSpec(
            num_scalar_prefetch=5,
            grid=(n_blocks,),
            in_specs=[pl.BlockSpec((rows, LANES), lambda i, nb, *_: (jnp.minimum(i, nb[0] - 1), 0)),
                      pl.BlockSpec(memory_space=pl.ANY),
                      pl.BlockSpec(memory_space=pl.ANY)],
            out_specs=pl.BlockSpec((rows, LANES), lambda i, *_: (i, 0)),
            scratch_shapes=[pltpu.VMEM((2, D, F2), F32), pltpu.VMEM((2, F2 // 2, D), F32),
                            pltpu.VMEM((D, F2), BF16), pltpu.VMEM((F2 // 2, D), BF16),
                            pltpu.SemaphoreType.DMA((2, 2))]),
        out_shape=jax.ShapeDtypeStruct(xs.shape, F32),
        compiler_params=_params("arbitrary"),
        name="moe_experts",
    )(n_used, block_e, fresh.astype(jnp.int32), (run % 2).astype(jnp.int32), next_e.astype(jnp.int32),
      xs, w_gu, w_down)


def _start_pair_gather(pos_ref, first_tok, n_tok, yb_hbm, ybuf, slot, sem, tb):
    def issue(r, carry):
        for k in range(TOP_K):
            pltpu.make_async_copy(_slab(yb_hbm, pos_ref[k * n_tok + first_tok + r]), _slab(ybuf.at[slot, k], r, SLAB_PITCH),
                                  sem.at[slot]).start(priority=k % 2)
        return carry
    lax.fori_loop(0, tb, issue, 0, unroll=ISSUE_UNROLL)


def _wait_pair_gather(yb_hbm, ybuf, slot, sem, tb):
    for k in range(TOP_K):
        pltpu.make_async_copy(yb_hbm.at[pl.ds(0, tb * SLAB), :], ybuf.at[slot, k, pl.ds(0, tb * SLAB), :],
                              sem.at[slot]).wait()


def _combine_body(final_norm, pos_ref, yb_hbm, h_ref, r_ref, lnw_ref, o_ref, ybuf, sem):
    i = pl.program_id(0)
    n = pl.num_programs(0)
    tb = h_ref.shape[0]
    slot = i % 2

    @pl.when(i == 0)
    def _():
        _start_pair_gather(pos_ref, 0, n * tb, yb_hbm, ybuf, 0, sem, tb)

    @pl.when(i + 1 < n)
    def _():
        _start_pair_gather(pos_ref, (i + 1) * tb, n * tb, yb_hbm, ybuf, 1 - slot, sem, tb)

    _wait_pair_gather(yb_hbm, ybuf, slot, sem, tb)
    out = h_ref[...]
    for k in range(TOP_K):
        out = out + _load_slabs(ybuf.at[slot, k], SLAB_PITCH) * r_ref[:, k:k + 1]
    if final_norm:
        out = _rms(out, lnw_ref[...])
    o_ref[...] = out


def _combine(pos, yb, h, routed, ln_out_w, final_norm, tb=256):
    S, D = h.shape
    return pl.pallas_call(
        functools.partial(_combine_body, final_norm),
        grid_spec=pltpu.PrefetchScalarGridSpec(
            num_scalar_prefetch=1,
            grid=(S // tb,),
            in_specs=[pl.BlockSpec(memory_space=pl.ANY),
                      pl.BlockSpec((tb, D), lambda i, pos: (i, 0)),
                      pl.BlockSpec((tb, LANES), lambda i, pos: (i, 0)),
                      pl.BlockSpec((1, D), lambda i, pos: (0, 0))],
            out_specs=pl.BlockSpec((tb, D), lambda i, pos: (i, 0)),
            scratch_shapes=[pltpu.VMEM((2, TOP_K, tb * SLAB_PITCH, LANES), F32), pltpu.SemaphoreType.DMA((2,))]),
        out_shape=jax.ShapeDtypeStruct((S, D), F32),
        compiler_params=_params("arbitrary"),
        name="moe_combine",
    )(pos, yb, h, routed, ln_out_w.reshape(1, D))


def _moe_layer(h, ln_w, wg, bg, we, be, w_gu, w_down, layer, ln_out_w, final_norm):
    T, D = h.shape
    hn, routed, routed_t, counts = _router(h, ln_w, wg, bg, we, be)
    A = T * TOP_K
    n_blocks = -(-A // ROW_BLOCK) + N_EXPERTS
    n_rows = n_blocks * ROW_BLOCK
    counts = counts[0, N_GROUPS:N_GROUPS + N_EXPERTS].astype(jnp.int32)
    padded = (counts + ROW_BLOCK - 1) // ROW_BLOCK * ROW_BLOCK
    pad_end = jnp.cumsum(padded)
    pad_start = pad_end - padded
    expert_idx = routed_t[2:2 + TOP_K].astype(jnp.int32)
    slot = routed_t[4:4 + TOP_K].astype(jnp.int32)
    is_e = expert_idx[..., None] == jnp.arange(N_EXPERTS, dtype=jnp.int32)
    pos = (jnp.sum(jnp.where(is_e, pad_start, 0), axis=-1) + slot).reshape(A)
    block_start = jnp.arange(n_blocks, dtype=jnp.int32) * ROW_BLOCK
    block_e = jnp.minimum(jnp.sum(pad_end[None, :] <= block_start[:, None], axis=1), N_EXPERTS - 1).astype(jnp.int32)
    n_used = (pad_end[-1] // ROW_BLOCK).astype(jnp.int32).reshape(1)
    xs = _dispatch(pos, pad_start + counts, pad_end, n_used, hn, n_rows)
    yb = _experts(n_used, block_e, xs, w_gu, w_down, layer)
    return _combine(pos, yb, h, routed, ln_out_w, final_norm)


def _sigmoid(x):
    return 1.0 / (1.0 + jnp.exp(-x))


def _rwkv_prep_body(h_ref, halo_ref, lnw_ref, mu_ref, w0_ref, w1_ref, w2_ref, a0_ref, a1_ref, a2_ref, g1_ref,
                    g2_ref, xs_ref, lw_ref, a_ref, g_ref):
    i = pl.program_id(0)
    lnw = lnw_ref[...]
    hn = _rms(h_ref[...], lnw)
    ts = hn.shape[0]
    last = _rms(halo_ref[...], lnw)[SUBLANES - 1:SUBLANES, :] * (i > 0).astype(F32)
    row = lax.broadcasted_iota(jnp.int32, (ts, 1), 0)
    prev = jnp.where(row == 0, last, pltpu.roll(hn, 1, axis=0))
    xx = prev - hn
    mix = lambda n: (hn + xx * mu_ref[n:n + 1, :]).astype(BF16)
    n_out = xs_ref.shape[0]
    for n in range(n_out):
        xs_ref[n] = mix(n)
    z = w0_ref[...] + _dot(jnp.tanh(_dot(mix(n_out), w1_ref[...])).astype(BF16), w2_ref[...])
    u = -z
    softplus = jnp.maximum(u, 0.0) + jnp.log1p(jnp.exp(-jnp.abs(u)))
    lw_ref[...] = -jnp.exp(-softplus - 0.5)
    a_ref[...] = _sigmoid(a0_ref[...] + _dot(_dot(mix(n_out + 1), a1_ref[...]).astype(BF16), a2_ref[...]))
    g_ref[...] = _dot(_sigmoid(_dot(mix(n_out + 2), g1_ref[...])).astype(BF16), g2_ref[...])


def _rwkv_prep(h, ln_w, mu, w0, w1, w2, a0, a1, a2, g1, g2, ts=256):
    S, D = h.shape
    n_mix = mu.shape[0]
    n_out = n_mix - 3
    halo_blocks = ts // SUBLANES

    def pad_lora(wa, wb):
        r = wa.shape[1]
        rp = -(-r // LANES) * LANES
        return (jnp.pad(wa, ((0, 0), (0, rp - r))).astype(BF16), jnp.pad(wb, ((0, rp - r), (0, 0))).astype(BF16))

    w1p, w2p = pad_lora(w1, w2)
    a1p, a2p = pad_lora(a1, a2)
    g1p, g2p = pad_lora(g1, g2)
    full = lambda arr: pl.BlockSpec(arr.shape, lambda i: (0,) * arr.ndim)
    vec = pl.BlockSpec((1, D), lambda i: (0, 0))
    row = pl.BlockSpec((ts, D), lambda i: (i, 0))
    return pl.pallas_call(
        _rwkv_prep_body,
        grid=(S // ts,),
        in_specs=[row,
                  pl.BlockSpec((SUBLANES, D), lambda i: (jnp.maximum(i * halo_blocks - 1, 0), 0)),
                  vec, full(mu), vec, full(w1p), full(w2p), vec, full(a1p), full(a2p), full(g1p), full(g2p)],
        out_specs=[pl.BlockSpec((n_out, ts, D), lambda i: (0, i, 0)), row, row, row],
        out_shape=[jax.ShapeDtypeStruct((n_out, S, D), BF16)] + [jax.ShapeDtypeStruct((S, D), F32)] * 3,
        compiler_params=_params("arbitrary"),
        name="rwkv_prep",
    )(h, h, ln_w.reshape(1, D), mu, w0.reshape(1, D), w1p, w2p, a0.reshape(1, D), a1p, a2p, g1p, g2p)


def _scan_chunk(r, k, v, lw, a, g, kkw, kaw, rkw, lnw, lnb, state, c):
    masks, seg_f, seg_b, tri, strict, incl, eye = c
    bf = lambda x: x.astype(BF16)
    stack = lambda xb: jnp.concatenate([xb * m for m in masks], axis=0)
    rows = lambda *xs: jnp.concatenate(xs, axis=0)
    cols = lambda *xs: jnp.concatenate(xs, axis=1)

    kk = k * kkw
    k2 = k * (1.0 + (a - 1.0) * kaw)
    sums = _segsum(rows(kk * kk, r * k2 * rkw), seg_b)
    cw = _cumsum_rows(tri, lw)
    yield
    kk = kk / jnp.maximum(jnp.sqrt(sums[:CHUNK]), 1e-12)
    bonus = sums[CHUNK:] * v
    bv = kk * a
    cl = cw[CHUNK - 1:CHUNK, :]
    at = bf(-kk * jnp.exp(cw - lw))
    rt = bf(r * jnp.exp(cw))
    dinv = jnp.exp(-cw)
    drem = jnp.exp(cl - cw)
    vb = bf(v)
    vs = stack(vb)
    gram = _dot(rows(at, rt), rows(stack(bf(bv * dinv)), stack(bf(k2 * dinv))), NT)
    bk_t = bf(rows(bv * drem, k2 * drem).T)
    decay_rows = jnp.broadcast_to(jnp.exp(cl), (GROUP, GROUP)).T
    yield
    a_ab = jnp.where(strict, gram[:CHUNK, :GROUP], 0.0)
    a_ak = bf(jnp.where(strict, gram[:CHUNK, GROUP:], 0.0))
    a_rb = bf(jnp.where(incl, gram[CHUNK:, :GROUP], 0.0))
    a_rk = bf(jnp.where(incl, gram[CHUNK:, GROUP:], 0.0))
    inv = eye + a_ab
    pw = _dot(bf(a_ab), stack(bf(a_ab)))
    xy = _dot(rows(cols(at, a_ak), cols(rt, a_rk)), rows(bf(state), vs))
    yield
    span = 2
    while span < CHUNK // 2:
        pwb = bf(pw)
        both = _dot(rows(bf(inv), pwb), stack(pwb))
        yield
        inv = inv + both[:CHUNK]
        pw = both[CHUNK:]
        span *= 2
    inv = inv + _dot(bf(inv), stack(bf(pw)))
    yield
    ub = bf(_dot(bf(inv), stack(bf(xy[:CHUNK]))))
    yield
    y = xy[CHUNK:] + _dot(a_rb, stack(ub))
    new_state = state * decay_rows + seg_f * _dot(bk_t, rows(ub, vb))
    yield
    inv_n = 1.0 / RWKV_HEAD_DIM
    mean = _segsum(y, seg_b) * inv_n
    yield
    yc = y - mean
    var = _segsum(yc * yc, seg_b) * inv_n
    yield
    yn = yc * lax.rsqrt(var + GN_EPS) * lnw + lnb
    return ((yn + bonus) * g), new_state


def _run_interleaved(gens):
    results = [None] * len(gens)
    live = list(range(len(gens)))
    while live:
        for i in list(live):
            try:
                next(gens[i])
            except StopIteration as done:
                results[i] = done.value
                live.remove(i)
    return results


def _segsum(x, seg_b):
    return _dot(x.astype(BF16), seg_b)


def _cumsum_rows(tri, x):
    hi = x.astype(BF16)
    r1 = x - hi.astype(F32)
    mid = r1.astype(BF16)
    lo = (r1 - mid.astype(F32)).astype(BF16)
    return _dot(jnp.concatenate([tri, tri, tri], axis=1), jnp.concatenate([hi, mid, lo], axis=0))


def _scan_consts():
    shift = RWKV_HEAD_DIM.bit_length() - 1
    lane_head = lax.broadcasted_iota(jnp.int32, (1, GROUP), 1) >> shift
    masks = [(lane_head == h).astype(F32).astype(BF16) for h in range(GROUP_HEADS)]
    ri = lax.broadcasted_iota(jnp.int32, (GROUP, GROUP), 0) >> shift
    ci = lax.broadcasted_iota(jnp.int32, (GROUP, GROUP), 1) >> shift
    seg_f = (ri == ci).astype(F32)
    tri = (lax.broadcasted_iota(jnp.int32, (CHUNK, CHUNK), 1)
           <= lax.broadcasted_iota(jnp.int32, (CHUNK, CHUNK), 0)).astype(F32).astype(BF16)
    t = lax.broadcasted_iota(jnp.int32, (CHUNK, GROUP), 0)
    s = lax.broadcasted_iota(jnp.int32, (CHUNK, GROUP), 1) & (RWKV_HEAD_DIM - 1)
    return (masks, seg_f, seg_f.astype(BF16), tri, s < t, s <= t, (s == t).astype(F32))


def _scan_body(r_ref, k_ref, v_ref, lw_ref, a_ref, g_ref, kkw_ref, kaw_ref, rkw_ref, lnw_ref, lnb_ref,
               o_ref, state_ref):
    @pl.when(pl.program_id(1) == 0)
    def _():
        state_ref[...] = jnp.zeros_like(state_ref)

    n_groups = o_ref.shape[1] // GROUP

    def chunk_step(j, carry):
        consts = _scan_consts()
        rows = pl.ds(pl.multiple_of(j * CHUNK, CHUNK), CHUNK)
        lanes = [slice(p * GROUP, (p + 1) * GROUP) for p in range(n_groups)]
        args = [(r_ref[rows, ln], k_ref[rows, ln], v_ref[rows, ln], lw_ref[rows, ln], a_ref[rows, ln],
                 g_ref[rows, ln], kkw_ref[:, ln], kaw_ref[:, ln], rkw_ref[:, ln], lnw_ref[:, ln],
                 lnb_ref[:, ln], state_ref[p]) for p, ln in enumerate(lanes)]
        results = _run_interleaved([_scan_chunk(*a, consts) for a in args])
        for p, (ln, (out, new_state)) in enumerate(zip(lanes, results)):
            state_ref[p] = new_state
            o_ref[rows, ln] = out.astype(o_ref.dtype)
        return carry

    lax.fori_loop(0, o_ref.shape[0] // CHUNK, chunk_step, 0)


def _rwkv_scan(rkv, lw, a, g, k_k, k_a, r_k, ln_w, ln_b):
    _, S, D = rkv.shape
    tr, tl = min(SCAN_ROWS, S), SCAN_LANES
    blk = lambda n: pl.BlockSpec((None, tr, tl), lambda p, i: (n, i, p))
    row = pl.BlockSpec((tr, tl), lambda p, i: (i, p))
    vec = pl.BlockSpec((1, tl), lambda p, i: (0, p))
    as_row = lambda w: w.reshape(1, D)
    return pl.pallas_call(
        _scan_body,
        grid=(D // tl, S // tr),
        in_specs=[blk(0), blk(1), blk(2), row, row, row, vec, vec, vec, vec, vec],
        out_specs=row,
        out_shape=jax.ShapeDtypeStruct((S, D), BF16),
        scratch_shapes=[pltpu.VMEM((tl // GROUP, GROUP, GROUP), F32)],
        compiler_params=_params("arbitrary", "arbitrary"),
        name="rwkv_scan",
    )(rkv, rkv, rkv, lw, a, g, as_row(k_k), as_row(k_a), as_row(r_k), as_row(ln_w), as_row(ln_b))


def _rwkv_layer(h, ln_w, mu, w_rkv, w_o, j, w0, w1, w2, a0, a1, a2, g1, g2, k_k, k_a, r_k, ln_gn_w, ln_gn_b):
    xs, lw, a, g = _rwkv_prep(h, ln_w, mu, w0, w1, w2, a0, a1, a2, g1, g2)
    rkv = _rkv_proj(xs, w_rkv, j)
    z = _rwkv_scan(rkv, lw, a, g, k_k, k_a, r_k, ln_gn_w, ln_gn_b)
    return _out_proj(z, w_o, j, h)


def kernel(x, mem, ln_mix_w, ln_xattn_w, ln_mem_w, xattn_wq, xattn_wkv, xattn_wo, ln_ffn_w, router_group_w, router_group_b, router_expert_w, router_expert_b, moe_w_gate_up, moe_w_down, pool_w, pool_scale, rwkv_mu, rwkv_w_rkv, rwkv_w_o, rwkv_w0, rwkv_w1, rwkv_w2, rwkv_a0, rwkv_a1, rwkv_a2, rwkv_g1, rwkv_g2, rwkv_k_k, rwkv_k_a, rwkv_r_k, rwkv_ln_w, rwkv_ln_b, ln_out_w):
    B, S, D = x.shape
    depth = ln_mix_w.shape[0]
    outs = []
    for b in range(B):
        h = x[b]
        for i in range(depth):
            j = i // 2
            if i % 2 == 0:
                h = _pool_layer(h, ln_mix_w[i], pool_w[j], pool_scale[j])
            else:
                h = _rwkv_layer(h, ln_mix_w[i], rwkv_mu[j], rwkv_w_rkv, rwkv_w_o, j, rwkv_w0[j], rwkv_w1[j],
                                rwkv_w2[j], rwkv_a0[j], rwkv_a1[j], rwkv_a2[j], rwkv_g1[j], rwkv_g2[j],
                                rwkv_k_k[j], rwkv_k_a[j], rwkv_r_k[j].reshape(D), rwkv_ln_w[j], rwkv_ln_b[j])
            kv = _mem_kv(mem[b], ln_mem_w[i], xattn_wkv, i)
            h = _xattn_layer(h, ln_xattn_w[i], xattn_wq, kv, xattn_wo, i)
            h = _moe_layer(h, ln_ffn_w[i], router_group_w[i], router_group_b[i], router_expert_w[i],
                           router_expert_b[i], moe_w_gate_up, moe_w_down, i, ln_out_w,
                           final_norm=(i == depth - 1))
        outs.append(h)
    return outs[0][None] if B == 1 else jnp.stack(outs, axis=0)
```

```python
import functools

import jax
import jax.numpy as jnp
from jax import lax
from jax.experimental import pallas as pl
from jax.experimental.pallas import tpu as pltpu

F32 = jnp.float32
BF16 = jnp.bfloat16

D_MODEL = 2048
POOL_WINDOWS = (2, 4, 8, 16)
POOL_GROUP_DIM = D_MODEL // len(POOL_WINDOWS)
MAX_WINDOW = max(POOL_WINDOWS)
RWKV_HEAD_DIM = 64
GN_EPS = 64e-5
XATTN_HEADS = 4
XATTN_HEAD_DIM = D_MODEL // XATTN_HEADS
N_GROUPS = 8
EXPERTS_PER_GROUP = 8
N_EXPERTS = N_GROUPS * EXPERTS_PER_GROUP
TOP_K = 2
D_EXPERT = D_MODEL // 8
ROW_BLOCK = 128
RMS_EPS = 1e-6

LANES = 128
SUBLANES = 8
VMEM_LIMIT = 56 * 1024 * 1024

CHUNK = 64
GROUP_HEADS = 4
GROUP = GROUP_HEADS * RWKV_HEAD_DIM
SLAB = D_MODEL // LANES
SLAB_PITCH = SLAB + SUBLANES
ISSUE_UNROLL = 8
WEIGHT_STAGE_ROWS = 256
SCAN_ROWS = 256
SCAN_LANES = 2048

NN = (((1,), (0,)), ((), ()))
NT = (((1,), (1,)), ((), ()))


def _params(*sem):
    return pltpu.CompilerParams(dimension_semantics=sem, vmem_limit_bytes=VMEM_LIMIT)


def _rms(x, w):
    return x * lax.rsqrt(jnp.mean(x * x, axis=-1, keepdims=True) + RMS_EPS) * w


def _dot(a, b, dims=NN):
    return lax.dot_general(a, b, dims, preferred_element_type=F32)


def _split(x):
    hi = x.astype(BF16)
    return hi, (x - hi.astype(F32)).astype(BF16)


def _store_slabs(ref, x):
    for j in range(SLAB):
        ref[pl.ds(j, x.shape[0], stride=SLAB), :] = x[:, j * LANES:(j + 1) * LANES]


def _load_slabs(ref, pitch=SLAB):
    n = ref.shape[0] // pitch
    return jnp.concatenate([ref[pl.ds(j, n, stride=pitch), :] for j in range(SLAB)], axis=-1)


def _slab(ref, row, pitch=SLAB):
    return ref.at[pl.ds(pl.multiple_of(row * pitch, SUBLANES), SLAB), :]


def _rkv_proj_body(x_ref, w_ref, o_ref, wb_ref):
    @pl.when(pl.program_id(2) == 0)
    def _():
        wb_ref[...] = w_ref[...].astype(BF16)

    o_ref[...] = _dot(x_ref[...], wb_ref[...])


def _rkv_proj(xs, w_rkv, j, tm=1024, tn=1024):
    _, M, K = xs.shape
    B, N = w_rkv.shape[1], w_rkv.shape[3]
    return pl.pallas_call(
        _rkv_proj_body,
        grid=(B, N // tn, M // tm),
        in_specs=[pl.BlockSpec((None, tm, K), lambda b, n, i: (b, i, 0)),
                  pl.BlockSpec((None, None, K, tn), lambda b, n, i: (j, b, 0, n))],
        out_specs=pl.BlockSpec((None, tm, tn), lambda b, n, i: (b, i, n)),
        out_shape=jax.ShapeDtypeStruct((B, M, N), F32),
        scratch_shapes=[pltpu.VMEM((K, tn), BF16)],
        compiler_params=_params("arbitrary", "arbitrary", "arbitrary"),
        name="rwkv_rkv_proj",
    )(xs, w_rkv)


def _out_proj_body(x_ref, w_ref, r_ref, o_ref, wb_ref):
    @pl.when(pl.program_id(1) == 0)
    def _():
        wb_ref[...] = w_ref[...].astype(BF16)

    o_ref[...] = r_ref[...] + _dot(x_ref[...], wb_ref[...])


def _out_proj(x, w_o, j, res, tm=1024, tn=1024):
    M, K = x.shape
    N = w_o.shape[2]
    return pl.pallas_call(
        _out_proj_body,
        grid=(N // tn, M // tm),
        in_specs=[pl.BlockSpec((tm, K), lambda n, i: (i, 0)),
                  pl.BlockSpec((None, K, tn), lambda n, i: (j, 0, n)),
                  pl.BlockSpec((tm, tn), lambda n, i: (i, n))],
        out_specs=pl.BlockSpec((tm, tn), lambda n, i: (i, n)),
        out_shape=jax.ShapeDtypeStruct((M, N), F32),
        scratch_shapes=[pltpu.VMEM((K, tn), BF16)],
        compiler_params=_params("arbitrary", "arbitrary"),
        name="rwkv_out_proj",
    )(x, w_o, res)


def _pool_body(h_ref, halo_ref, lnw_ref, pw_ref, ps_ref, o_ref):
    i = pl.program_id(0)
    ts = h_ref.shape[0]
    h = h_ref[...]
    lnw = lnw_ref[...]
    x = _rms(h, lnw)
    xh = _rms(halo_ref[...], lnw) * (i > 0).astype(F32)
    xe = jnp.concatenate([xh, x], axis=0)
    t = i * ts + lax.broadcasted_iota(jnp.int32, (ts, 1), 0)
    outs = []
    for g, win in enumerate(POOL_WINDOWS):
        sl = slice(g * POOL_GROUP_DIM, (g + 1) * POOL_GROUP_DIM)
        acc = xe[:, sl]
        span = 1
        while span < win:
            acc = acc + pltpu.roll(acc, span, axis=0)
            span *= 2
        cnt = jnp.minimum(t + 1, win).astype(F32)
        pooled = acc[MAX_WINDOW:, :] / cnt - x[:, sl]
        outs.append(_dot(pooled.astype(BF16), pw_ref[g]))
    y = jnp.concatenate(outs, axis=-1)
    o_ref[...] = h + y * ps_ref[...]


def _pool_layer(h, ln_w, pool_w, pool_scale, ts=512):
    S, D = h.shape
    G = pool_w.shape[0]
    halo_blocks = ts // MAX_WINDOW
    return pl.pallas_call(
        _pool_body,
        grid=(S // ts,),
        in_specs=[pl.BlockSpec((ts, D), lambda i: (i, 0)),
                  pl.BlockSpec((MAX_WINDOW, D), lambda i: (jnp.maximum(i * halo_blocks - 1, 0), 0)),
                  pl.BlockSpec((1, D), lambda i: (0, 0)),
                  pl.BlockSpec((G, POOL_GROUP_DIM, POOL_GROUP_DIM), lambda i: (0, 0, 0)),
                  pl.BlockSpec((1, D), lambda i: (0, 0))],
        out_specs=pl.BlockSpec((ts, D), lambda i: (i, 0)),
        out_shape=jax.ShapeDtypeStruct((S, D), F32),
        compiler_params=_params("arbitrary"),
        name="pool_layer",
    )(h, h, ln_w.reshape(1, D), pool_w.astype(BF16), pool_scale.reshape(1, D))


def _memkv_body(mem_ref, lnw_ref, w_ref, o_ref):
    mn = _rms(mem_ref[...], lnw_ref[...])
    o_ref[...] = _dot(mn.astype(BF16), w_ref[...].astype(BF16)).astype(o_ref.dtype)


def _mem_kv(mem, ln_w, wkv, layer, tn=1024):
    M, D = mem.shape
    N = wkv.shape[2]
    return pl.pallas_call(
        _memkv_body,
        grid=(N // tn,),
        in_specs=[pl.BlockSpec((M, D), lambda n: (0, 0)),
                  pl.BlockSpec((1, D), lambda n: (0, 0)),
                  pl.BlockSpec((None, D, tn), lambda n: (layer, 0, n))],
        out_specs=pl.BlockSpec((M, tn), lambda n: (0, n)),
        out_shape=jax.ShapeDtypeStruct((M, N), BF16),
        compiler_params=_params("arbitrary"),
        name="xattn_mem_kv",
    )(mem, ln_w.reshape(1, D), wkv)


def _load_weight_bf16(w_hbm, dst, stage, sem):
    rows = stage.shape[1]
    n_chunks = w_hbm.shape[0] // rows

    def chunk_copy(c):
        return pltpu.make_async_copy(w_hbm.at[pl.ds(c * rows, rows), :], stage.at[c % 2], sem.at[c % 2])

    chunk_copy(0).start()
    for c in range(n_chunks):
        if c + 1 < n_chunks:
            chunk_copy(c + 1).start()
        chunk_copy(c).wait()
        dst[pl.ds(c * rows, rows), :] = stage[c % 2].astype(BF16)


def _xattn_body(layer, h_ref, lnw_ref, k_ref, v_ref, wq_hbm, wo_hbm, o_ref, wq_ref, wo_ref, stage, sem):
    @pl.when(pl.program_id(0) == 0)
    def _():
        _load_weight_bf16(wq_hbm.at[layer], wq_ref, stage, sem)
        _load_weight_bf16(wo_hbm.at[layer], wo_ref, stage, sem)

    h = h_ref[...]
    hn = _rms(h, lnw_ref[...]).astype(BF16)
    q = _dot(hn, wq_ref[...])
    scale = XATTN_HEAD_DIM ** -0.5
    heads = []
    for hd in range(XATTN_HEADS):
        sl = slice(hd * XATTN_HEAD_DIM, (hd + 1) * XATTN_HEAD_DIM)
        s = _dot(q[:, sl].astype(BF16), k_ref[:, sl], NT) * scale
        s = s - jnp.max(s, axis=-1, keepdims=True)
        e = jnp.exp(s)
        p = e / jnp.sum(e, axis=-1, keepdims=True)
        heads.append(_dot(p.astype(BF16), v_ref[:, sl]))
    o = jnp.concatenate(heads, axis=-1).astype(BF16)
    o_ref[...] = h + _dot(o, wo_ref[...])


def _xattn_layer(h, ln_w, wq, kv, wo, layer, ts=512):
    S, D = h.shape
    M = kv.shape[0]
    return pl.pallas_call(
        functools.partial(_xattn_body, layer),
        grid=(S // ts,),
        in_specs=[pl.BlockSpec((ts, D), lambda i: (i, 0)),
                  pl.BlockSpec((1, D), lambda i: (0, 0)),
                  pl.BlockSpec((M, D), lambda i: (0, 0)),
                  pl.BlockSpec((M, D), lambda i: (0, 1)),
                  pl.BlockSpec(memory_space=pl.ANY),
                  pl.BlockSpec(memory_space=pl.ANY)],
        out_specs=pl.BlockSpec((ts, D), lambda i: (i, 0)),
        out_shape=jax.ShapeDtypeStruct((S, D), F32),
        scratch_shapes=[pltpu.VMEM((D, D), BF16), pltpu.VMEM((D, D), BF16),
                        pltpu.VMEM((2, WEIGHT_STAGE_ROWS, D), F32), pltpu.SemaphoreType.DMA((2,))],
        compiler_params=_params("arbitrary"),
        name="xattn_layer",
    )(h, ln_w.reshape(1, D), kv, kv, wq, wo)


def _router_body(h_ref, lnw_ref, whi_ref, wlo_ref, b_ref, hn_ref, r_ref, rt_ref, cnt_ref):
    i = pl.program_id(0)

    @pl.when(i == 0)
    def _():
        cnt_ref[...] = jnp.zeros_like(cnt_ref)

    hn = _rms(h_ref[...], lnw_ref[...])
    _store_slabs(hn_ref, hn)
    xh, xl = _split(hn)
    whi = whi_ref[...]
    lg = _dot(xh, whi) + (_dot(xh, wlo_ref[...]) + _dot(xl, whi)) + b_ref[...]
    ts = lg.shape[0]
    lane = lax.broadcasted_iota(jnp.int32, (ts, LANES), 1)
    lanef = lane.astype(F32)
    neg = jnp.float32(-jnp.inf)
    gl = jnp.where(lane < N_GROUPS, lg, neg)
    gmax = jnp.max(gl, axis=-1, keepdims=True)
    pg_top = 1.0 / jnp.sum(jnp.exp(gl - gmax), axis=-1, keepdims=True)
    g_idx = jnp.min(jnp.where(gl == gmax, lanef, float(LANES)), axis=-1, keepdims=True)
    lo_lane = g_idx * EXPERTS_PER_GROUP + N_GROUPS
    el = jnp.where(lanef >= lo_lane, jnp.where(lanef < lo_lane + EXPERTS_PER_GROUP, lg, neg), neg)
    m1 = jnp.max(el, axis=-1, keepdims=True)
    i1 = jnp.min(jnp.where(el == m1, lanef, float(LANES)), axis=-1, keepdims=True)
    el2 = jnp.where(lanef == i1, neg, el)
    m2 = jnp.max(el2, axis=-1, keepdims=True)
    i2 = jnp.min(jnp.where(el2 == m2, lanef, float(LANES)), axis=-1, keepdims=True)
    e2 = jnp.exp(m2 - m1)
    g1 = pg_top / (1.0 + e2)
    g2 = pg_top * e2 / (1.0 + e2)
    hit1 = (lanef == i1).astype(F32)
    hit2 = (lanef == i2).astype(F32)
    hits = hit1 + hit2
    earlier = (lax.broadcasted_iota(jnp.int32, (ts, ts), 1)
               < lax.broadcasted_iota(jnp.int32, (ts, ts), 0)).astype(F32).astype(BF16)
    before = _dot(earlier, hits.astype(BF16)) + cnt_ref[...]
    r1 = jnp.sum(before * hit1, axis=-1, keepdims=True)
    r2 = jnp.sum(before * hit2, axis=-1, keepdims=True)
    cnt_ref[...] += jnp.sum(hits, axis=0, keepdims=True)
    out = jnp.where(lane == 0, g1, 0.0)
    out = jnp.where(lane == 1, g2, out)
    out = jnp.where(lane == 2, i1 - N_GROUPS, out)
    out = jnp.where(lane == 3, i2 - N_GROUPS, out)
    out = jnp.where(lane == 4, r1, out)
    out = jnp.where(lane == 5, r2, out)
    r_ref[...] = out
    rt_ref[...] = out.T[:SUBLANES, :]


def _router(h, ln_w, wg, bg, we, be, ts=512):
    S, D = h.shape
    unused = LANES - N_GROUPS - N_EXPERTS
    w = jnp.concatenate([wg, we, jnp.zeros((D, unused), F32)], axis=1)
    b = jnp.concatenate([bg, be, jnp.zeros((unused,), F32)]).reshape(1, LANES)
    whi, wlo = _split(w)
    return pl.pallas_call(
        _router_body,
        grid=(S // ts,),
        in_specs=[pl.BlockSpec((ts, D), lambda i: (i, 0)),
                  pl.BlockSpec((1, D), lambda i: (0, 0)),
                  pl.BlockSpec((D, LANES), lambda i: (0, 0)),
                  pl.BlockSpec((D, LANES), lambda i: (0, 0)),
                  pl.BlockSpec((1, LANES), lambda i: (0, 0))],
        out_specs=[pl.BlockSpec((ts * SLAB, LANES), lambda i: (i, 0)),
                   pl.BlockSpec((ts, LANES), lambda i: (i, 0)),
                   pl.BlockSpec((SUBLANES, ts), lambda i: (0, i)),
                   pl.BlockSpec((1, LANES), lambda i: (0, 0))],
        out_shape=[jax.ShapeDtypeStruct((S * SLAB, LANES), F32), jax.ShapeDtypeStruct((S, LANES), F32),
                   jax.ShapeDtypeStruct((SUBLANES, S), F32), jax.ShapeDtypeStruct((1, LANES), F32)],
        compiler_params=_params("arbitrary"),
        name="moe_router",
    )(h, ln_w.reshape(1, D), whi, wlo, b)


def _dispatch_body(pos_ref, fill_lo_ref, fill_hi_ref, nblk_ref, hn_ref, xs_hbm, stage, zeros, sem, zsem):
    i = pl.program_id(0)
    n = pl.num_programs(0)
    tb = hn_ref.shape[0] // SLAB
    n_tok = n * tb
    slot = i % 2

    def wait_step(s):
        for _ in range(TOP_K):
            pltpu.make_async_copy(stage.at[s], xs_hbm.at[pl.ds(0, tb * SLAB), :], sem.at[s]).wait()

    @pl.when(i >= 2)
    def _():
        wait_step(slot)

    stage[slot] = hn_ref[...]

    def issue(r, carry):
        for k in range(TOP_K):
            pltpu.make_async_copy(_slab(stage.at[slot], r), _slab(xs_hbm, pos_ref[k * n_tok + i * tb + r]),
                                  sem.at[slot]).start(priority=k % 2)
        return carry
    lax.fori_loop(0, tb, issue, 0, unroll=ISSUE_UNROLL)

    @pl.when(jnp.logical_and(i == n - 1, n >= 2))
    def _():
        wait_step(1 - slot)

    @pl.when(i == n - 1)
    def _():
        zeros[...] = jnp.zeros_like(zeros)

        def zero_row(row):
            return pltpu.make_async_copy(zeros.at[pl.ds(0, SLAB), :], _slab(xs_hbm, row), zsem)

        def zero_block(b):
            return pltpu.make_async_copy(zeros, _slab_block(xs_hbm, b), zsem)

        def per_expert(e, carry):
            lo, hi = fill_lo_ref[e], fill_hi_ref[e]
            lax.fori_loop(lo, hi, lambda row, c: (zero_row(row).start(), c)[1], 0)
            lax.fori_loop(lo, hi, lambda row, c: (zero_row(row).wait(), c)[1], 0)
            return carry
        lax.fori_loop(0, N_EXPERTS, per_expert, 0)
        n_blocks = xs_hbm.shape[0] // (ROW_BLOCK * SLAB)
        lax.fori_loop(nblk_ref[0], n_blocks, lambda b, c: (zero_block(b).start(), c)[1], 0)
        lax.fori_loop(nblk_ref[0], n_blocks, lambda b, c: (zero_block(b).wait(), c)[1], 0)
        wait_step(slot)


def _slab_block(ref, block):
    rows = ROW_BLOCK * SLAB
    return ref.at[pl.ds(pl.multiple_of(block * rows, rows), rows), :]


def _dispatch(pos, fill_lo, fill_hi, n_used, hn, n_rows, tb=512):
    S = hn.shape[0] // SLAB
    return pl.pallas_call(
        _dispatch_body,
        grid_spec=pltpu.PrefetchScalarGridSpec(
            num_scalar_prefetch=4,
            grid=(S // tb,),
            in_specs=[pl.BlockSpec((tb * SLAB, LANES), lambda i, *_: (i, 0))],
            out_specs=pl.BlockSpec(memory_space=pl.ANY),
            scratch_shapes=[pltpu.VMEM((2, tb * SLAB, LANES), F32), pltpu.VMEM((ROW_BLOCK * SLAB, LANES), F32),
                            pltpu.SemaphoreType.DMA((2,)), pltpu.SemaphoreType.DMA(())]),
        out_shape=jax.ShapeDtypeStruct((n_rows * SLAB, LANES), F32),
        compiler_params=_params("arbitrary"),
        name="moe_dispatch",
    )(pos, fill_lo, fill_hi, n_used, hn)


def _experts_body(layer, nblk_ref, be_ref, fresh_ref, wslot_ref, next_e_ref, x_ref, wgu_hbm, wd_hbm, o_ref,
                  wgu_f, wd_f, wgu_b, wd_b, wsem):
    i = pl.program_id(0)
    n_used = nblk_ref[0]

    def weight_copies(e, slot):
        return (pltpu.make_async_copy(wgu_hbm.at[layer, e], wgu_f.at[slot], wsem.at[0, slot]),
                pltpu.make_async_copy(wd_hbm.at[layer, e], wd_f.at[slot], wsem.at[1, slot]))

    @pl.when(i == 0)
    def _():
        for cp in weight_copies(be_ref[0], 0):
            cp.start()

    @pl.when(jnp.logical_and(i < n_used, fresh_ref[i] == 1))
    def _():
        slot = wslot_ref[i]
        for cp in weight_copies(be_ref[i], slot):
            cp.wait()
        wgu_b[...] = wgu_f[slot].astype(BF16)
        wd_b[...] = wd_f[slot].astype(BF16)

        @pl.when(next_e_ref[i] >= 0)
        def _():
            for cp in weight_copies(next_e_ref[i], 1 - slot):
                cp.start()

    @pl.when(i < n_used)
    def _():
        gu = _dot(_load_slabs(x_ref).astype(BF16), wgu_b[...])
        gg = gu[:, :D_EXPERT]
        uu = gu[:, D_EXPERT:]
        act = (gg * (1.0 / (1.0 + jnp.exp(-gg))) * uu).astype(BF16)
        _store_slabs(o_ref, _dot(act, wd_b[...]))

    @pl.when(i >= n_used)
    def _():
        o_ref[...] = jnp.zeros_like(o_ref)


def _experts(n_used, block_e, xs, w_gu, w_down, layer):
    rows = ROW_BLOCK * SLAB
    n_blocks = xs.shape[0] // rows
    D = w_gu.shape[2]
    F2 = w_gu.shape[3]
    idx = jnp.arange(n_blocks, dtype=jnp.int32)
    used = idx < n_used[0]
    fresh = jnp.logical_and(used, jnp.concatenate([jnp.ones((1,), bool), block_e[1:] != block_e[:-1]]))
    run = jnp.cumsum(fresh.astype(jnp.int32)) - 1
    run_start = jnp.where(fresh, idx, n_blocks)
    next_start = lax.cummin(jnp.concatenate([run_start[1:], jnp.full((1,), n_blocks, jnp.int32)]), reverse=True)
    next_e = jnp.where(next_start < n_blocks, block_e[jnp.minimum(next_start, n_blocks - 1)], -1)
    return pl.pallas_call(
        functools.partial(_experts_body, layer),
        grid_spec=pltpu.PrefetchScalarGridSpec(
            num_scalar_prefetch=5,
            grid=(n_blocks,),
            in_specs=[pl.BlockSpec((rows, LANES), lambda i, nb, *_: (jnp.minimum(i, nb[0] - 1), 0)),
                      pl.BlockSpec(memory_space=pl.ANY),
                      pl.BlockSpec(memory_space=pl.ANY)],
            out_specs=pl.BlockSpec((rows, LANES), lambda i, *_: (i, 0)),
            scratch_shapes=[pltpu.VMEM((2, D, F2), F32), pltpu.VMEM((2, F2 // 2, D), F32),
                            pltpu.VMEM((D, F2), BF16), pltpu.VMEM((F2 // 2, D), BF16),
                            pltpu.SemaphoreType.DMA((2, 2))]),
        out_shape=jax.ShapeDtypeStruct(xs.shape, F32),
        compiler_params=_params("arbitrary"),
        name="moe_experts",
    )(n_used, block_e, fresh.astype(jnp.int32), (run % 2).astype(jnp.int32), next_e.astype(jnp.int32),
      xs, w_gu, w_down)


def _start_pair_gather(pos_ref, first_tok, n_tok, yb_hbm, ybuf, slot, sem, tb):
    def issue(r, carry):
        for k in range(TOP_K):
            pltpu.make_async_copy(_slab(yb_hbm, pos_ref[k * n_tok + first_tok + r]), _slab(ybuf.at[slot, k], r, SLAB_PITCH),
                                  sem.at[slot]).start(priority=k % 2)
        return carry
    lax.fori_loop(0, tb, issue, 0, unroll=ISSUE_UNROLL)


def _wait_pair_gather(yb_hbm, ybuf, slot, sem, tb):
    for k in range(TOP_K):
        pltpu.make_async_copy(yb_hbm.at[pl.ds(0, tb * SLAB), :], ybuf.at[slot, k, pl.ds(0, tb * SLAB), :],
                              sem.at[slot]).wait()


def _combine_body(final_norm, pos_ref, yb_hbm, h_ref, r_ref, lnw_ref, o_ref, ybuf, sem):
    i = pl.program_id(0)
    n = pl.num_programs(0)
    tb = h_ref.shape[0]
    slot = i % 2

    @pl.when(i == 0)
    def _():
        _start_pair_gather(pos_ref, 0, n * tb, yb_hbm, ybuf, 0, sem, tb)

    @pl.when(i + 1 < n)
    def _():
        _start_pair_gather(pos_ref, (i + 1) * tb, n * tb, yb_hbm, ybuf, 1 - slot, sem, tb)

    _wait_pair_gather(yb_hbm, ybuf, slot, sem, tb)
    out = h_ref[...]
    for k in range(TOP_K):
        out = out + _load_slabs(ybuf.at[slot, k], SLAB_PITCH) * r_ref[:, k:k + 1]
    if final_norm:
        out = _rms(out, lnw_ref[...])
    o_ref[...] = out


def _combine(pos, yb, h, routed, ln_out_w, final_norm, tb=256):
    S, D = h.shape
    return pl.pallas_call(
        functools.partial(_combine_body, final_norm),
        grid_spec=pltpu.PrefetchScalarGridSpec(
            num_scalar_prefetch=1,
            grid=(S // tb,),
            in_specs=[pl.BlockSpec(memory_space=pl.ANY),
                      pl.BlockSpec((tb, D), lambda i, pos: (i, 0)),
                      pl.BlockSpec((tb, LANES), lambda i, pos: (i, 0)),
                      pl.BlockSpec((1, D), lambda i, pos: (0, 0))],
            out_specs=pl.BlockSpec((tb, D), lambda i, pos: (i, 0)),
            scratch_shapes=[pltpu.VMEM((2, TOP_K, tb * SLAB_PITCH, LANES), F32), pltpu.SemaphoreType.DMA((2,))]),
        out_shape=jax.ShapeDtypeStruct((S, D), F32),
        compiler_params=_params("arbitrary"),
        name="moe_combine",
    )(pos, yb, h, routed, ln_out_w.reshape(1, D))


def _moe_layer(h, ln_w, wg, bg, we, be, w_gu, w_down, layer, ln_out_w, final_norm):
    T, D = h.shape
    hn, routed, routed_t, counts = _router(h, ln_w, wg, bg, we, be)
    A = T * TOP_K
    n_blocks = -(-A // ROW_BLOCK) + N_EXPERTS
    n_rows = n_blocks * ROW_BLOCK
    counts = counts[0, N_GROUPS:N_GROUPS + N_EXPERTS].astype(jnp.int32)
    padded = (counts + ROW_BLOCK - 1) // ROW_BLOCK * ROW_BLOCK
    pad_end = jnp.cumsum(padded)
    pad_start = pad_end - padded
    expert_idx = routed_t[2:2 + TOP_K].astype(jnp.int32)
    slot = routed_t[4:4 + TOP_K].astype(jnp.int32)
    is_e = expert_idx[..., None] == jnp.arange(N_EXPERTS, dtype=jnp.int32)
    pos = (jnp.sum(jnp.where(is_e, pad_start, 0), axis=-1) + slot).reshape(A)
    block_start = jnp.arange(n_blocks, dtype=jnp.int32) * ROW_BLOCK
    block_e = jnp.minimum(jnp.sum(pad_end[None, :] <= block_start[:, None], axis=1), N_EXPERTS - 1).astype(jnp.int32)
    n_used = (pad_end[-1] // ROW_BLOCK).astype(jnp.int32).reshape(1)
    xs = _dispatch(pos, pad_start + counts, pad_end, n_used, hn, n_rows)
    yb = _experts(n_used, block_e, xs, w_gu, w_down, layer)
    return _combine(pos, yb, h, routed, ln_out_w, final_norm)


def _sigmoid(x):
    return 1.0 / (1.0 + jnp.exp(-x))


def _rwkv_prep_body(h_ref, halo_ref, lnw_ref, mu_ref, w0_ref, w1_ref, w2_ref, a0_ref, a1_ref, a2_ref, g1_ref,
                    g2_ref, xs_ref, lw_ref, a_ref, g_ref):
    i = pl.program_id(0)
    lnw = lnw_ref[...]
    hn = _rms(h_ref[...], lnw)
    ts = hn.shape[0]
    last = _rms(halo_ref[...], lnw)[SUBLANES - 1:SUBLANES, :] * (i > 0).astype(F32)
    row = lax.broadcasted_iota(jnp.int32, (ts, 1), 0)
    prev = jnp.where(row == 0, last, pltpu.roll(hn, 1, axis=0))
    xx = prev - hn
    mix = lambda n: (hn + xx * mu_ref[n:n + 1, :]).astype(BF16)
    n_out = xs_ref.shape[0]
    for n in range(n_out):
        xs_ref[n] = mix(n)
    z = w0_ref[...] + _dot(jnp.tanh(_dot(mix(n_out), w1_ref[...])).astype(BF16), w2_ref[...])
    u = -z
    softplus = jnp.maximum(u, 0.0) + jnp.log1p(jnp.exp(-jnp.abs(u)))
    lw_ref[...] = -jnp.exp(-softplus - 0.5)
    a_ref[...] = _sigmoid(a0_ref[...] + _dot(_dot(mix(n_out + 1), a1_ref[...]).astype(BF16), a2_ref[...]))
    g_ref[...] = _dot(_sigmoid(_dot(mix(n_out + 2), g1_ref[...])).astype(BF16), g2_ref[...])


def _rwkv_prep(h, ln_w, mu, w0, w1, w2, a0, a1, a2, g1, g2, ts=256):
    S, D = h.shape
    n_mix = mu.shape[0]
    n_out = n_mix - 3
    halo_blocks = ts // SUBLANES

    def pad_lora(wa, wb):
        r = wa.shape[1]
        rp = -(-r // LANES) * LANES
        return (jnp.pad(wa, ((0, 0), (0, rp - r))).astype(BF16), jnp.pad(wb, ((0, rp - r), (0, 0))).astype(BF16))

    w1p, w2p = pad_lora(w1, w2)
    a1p, a2p = pad_lora(a1, a2)
    g1p, g2p = pad_lora(g1, g2)
    full = lambda arr: pl.BlockSpec(arr.shape, lambda i: (0,) * arr.ndim)
    vec = pl.BlockSpec((1, D), lambda i: (0, 0))
    row = pl.BlockSpec((ts, D), lambda i: (i, 0))
    return pl.pallas_call(
        _rwkv_prep_body,
        grid=(S // ts,),
        in_specs=[row,
                  pl.BlockSpec((SUBLANES, D), lambda i: (jnp.maximum(i * halo_blocks - 1, 0), 0)),
                  vec, full(mu), vec, full(w1p), full(w2p), vec, full(a1p), full(a2p), full(g1p), full(g2p)],
        out_specs=[pl.BlockSpec((n_out, ts, D), lambda i: (0, i, 0)), row, row, row],
        out_shape=[jax.ShapeDtypeStruct((n_out, S, D), BF16)] + [jax.ShapeDtypeStruct((S, D), F32)] * 3,
        compiler_params=_params("arbitrary"),
        name="rwkv_prep",
    )(h, h, ln_w.reshape(1, D), mu, w0.reshape(1, D), w1p, w2p, a0.reshape(1, D), a1p, a2p, g1p, g2p)


def _scan_chunk(r, k, v, lw, a, g, kkw, kaw, rkw, lnw, lnb, state, c):
    masks, seg_f, seg_b, tri, strict, incl, eye = c
    bf = lambda x: x.astype(BF16)
    stack = lambda xb: jnp.concatenate([xb * m for m in masks], axis=0)
    rows = lambda *xs: jnp.concatenate(xs, axis=0)
    cols = lambda *xs: jnp.concatenate(xs, axis=1)

    kk = k * kkw
    k2 = k * (1.0 + (a - 1.0) * kaw)
    sums = _segsum(rows(kk * kk, r * k2 * rkw), seg_b)
    cw = _cumsum_rows(tri, lw)
    yield
    kk = kk / jnp.maximum(jnp.sqrt(sums[:CHUNK]), 1e-12)
    bonus = sums[CHUNK:] * v
    bv = kk * a
    cl = cw[CHUNK - 1:CHUNK, :]
    at = bf(-kk * jnp.exp(cw - lw))
    rt = bf(r * jnp.exp(cw))
    dinv = jnp.exp(-cw)
    drem = jnp.exp(cl - cw)
    vb = bf(v)
    vs = stack(vb)
    gram = _dot(rows(at, rt), rows(stack(bf(bv * dinv)), stack(bf(k2 * dinv))), NT)
    bk_t = bf(rows(bv * drem, k2 * drem).T)
    decay_rows = jnp.broadcast_to(jnp.exp(cl), (GROUP, GROUP)).T
    yield
    a_ab = jnp.where(strict, gram[:CHUNK, :GROUP], 0.0)
    a_ak = bf(jnp.where(strict, gram[:CHUNK, GROUP:], 0.0))
    a_rb = bf(jnp.where(incl, gram[CHUNK:, :GROUP], 0.0))
    a_rk = bf(jnp.where(incl, gram[CHUNK:, GROUP:], 0.0))
    inv = eye + a_ab
    pw = _dot(bf(a_ab), stack(bf(a_ab)))
    xy = _dot(rows(cols(at, a_ak), cols(rt, a_rk)), rows(bf(state), vs))
    yield
    span = 2
    while span < CHUNK // 2:
        pwb = bf(pw)
        both = _dot(rows(bf(inv), pwb), stack(pwb))
        yield
        inv = inv + both[:CHUNK]
        pw = both[CHUNK:]
        span *= 2
    inv = inv + _dot(bf(inv), stack(bf(pw)))
    yield
    ub = bf(_dot(bf(inv), stack(bf(xy[:CHUNK]))))
    yield
    y = xy[CHUNK:] + _dot(a_rb, stack(ub))
    new_state = state * decay_rows + seg_f * _dot(bk_t, rows(ub, vb))
    yield
    inv_n = 1.0 / RWKV_HEAD_DIM
    mean = _segsum(y, seg_b) * inv_n
    yield
    yc = y - mean
    var = _segsum(yc * yc, seg_b) * inv_n
    yield
    yn = yc * lax.rsqrt(var + GN_EPS) * lnw + lnb
    return ((yn + bonus) * g), new_state


def _run_interleaved(gens):
    results = [None] * len(gens)
    live = list(range(len(gens)))
    while live:
        for i in list(live):
            try:
                next(gens[i])
            except StopIteration as done:
                results[i] = done.value
                live.remove(i)
    return results


def _segsum(x, seg_b):
    return _dot(x.astype(BF16), seg_b)


def _cumsum_rows(tri, x):
    hi = x.astype(BF16)
    r1 = x - hi.astype(F32)
    mid = r1.astype(BF16)
    lo = (r1 - mid.astype(F32)).astype(BF16)
    return _dot(jnp.concatenate([tri, tri, tri], axis=1), jnp.concatenate([hi, mid, lo], axis=0))


def _scan_consts():
    shift = RWKV_HEAD_DIM.bit_length() - 1
    lane_head = lax.broadcasted_iota(jnp.int32, (1, GROUP), 1) >> shift
    masks = [(lane_head == h).astype(F32).astype(BF16) for h in range(GROUP_HEADS)]
    ri = lax.broadcasted_iota(jnp.int32, (GROUP, GROUP), 0) >> shift
    ci = lax.broadcasted_iota(jnp.int32, (GROUP, GROUP), 1) >> shift
    seg_f = (ri == ci).astype(F32)
    tri = (lax.broadcasted_iota(jnp.int32, (CHUNK, CHUNK), 1)
           <= lax.broadcasted_iota(jnp.int32, (CHUNK, CHUNK), 0)).astype(F32).astype(BF16)
    t = lax.broadcasted_iota(jnp.int32, (CHUNK, GROUP), 0)
    s = lax.broadcasted_iota(jnp.int32, (CHUNK, GROUP), 1) & (RWKV_HEAD_DIM - 1)
    return (masks, seg_f, seg_f.astype(BF16), tri, s < t, s <= t, (s == t).astype(F32))


def _scan_body(r_ref, k_ref, v_ref, lw_ref, a_ref, g_ref, kkw_ref, kaw_ref, rkw_ref, lnw_ref, lnb_ref,
               o_ref, state_ref):
    @pl.when(pl.program_id(1) == 0)
    def _():
        state_ref[...] = jnp.zeros_like(state_ref)

    n_groups = o_ref.shape[1] // GROUP

    def chunk_step(j, carry):
        consts = _scan_consts()
        rows = pl.ds(pl.multiple_of(j * CHUNK, CHUNK), CHUNK)
        lanes = [slice(p * GROUP, (p + 1) * GROUP) for p in range(n_groups)]
        args = [(r_ref[rows, ln], k_ref[rows, ln], v_ref[rows, ln], lw_ref[rows, ln], a_ref[rows, ln],
                 g_ref[rows, ln], kkw_ref[:, ln], kaw_ref[:, ln], rkw_ref[:, ln], lnw_ref[:, ln],
                 lnb_ref[:, ln], state_ref[p]) for p, ln in enumerate(lanes)]
        results = _run_interleaved([_scan_chunk(*a, consts) for a in args])
        for p, (ln, (out, new_state)) in enumerate(zip(lanes, results)):
            state_ref[p] = new_state
            o_ref[rows, ln] = out.astype(o_ref.dtype)
        return carry

    lax.fori_loop(0, o_ref.shape[0] // CHUNK, chunk_step, 0)


def _rwkv_scan(rkv, lw, a, g, k_k, k_a, r_k, ln_w, ln_b):
    _, S, D = rkv.shape
    tr, tl = min(SCAN_ROWS, S), SCAN_LANES
    blk = lambda n: pl.BlockSpec((None, tr, tl), lambda p, i: (n, i, p))
    row = pl.BlockSpec((tr, tl), lambda p, i: (i, p))
    vec = pl.BlockSpec((1, tl), lambda p, i: (0, p))
    as_row = lambda w: w.reshape(1, D)
    return pl.pallas_call(
        _scan_body,
        grid=(D // tl, S // tr),
        in_specs=[blk(0), blk(1), blk(2), row, row, row, vec, vec, vec, vec, vec],
        out_specs=row,
        out_shape=jax.ShapeDtypeStruct((S, D), BF16),
        scratch_shapes=[pltpu.VMEM((tl // GROUP, GROUP, GROUP), F32)],
        compiler_params=_params("arbitrary", "arbitrary"),
        name="rwkv_scan",
    )(rkv, rkv, rkv, lw, a, g, as_row(k_k), as_row(k_a), as_row(r_k), as_row(ln_w), as_row(ln_b))


def _rwkv_layer(h, ln_w, mu, w_rkv, w_o, j, w0, w1, w2, a0, a1, a2, g1, g2, k_k, k_a, r_k, ln_gn_w, ln_gn_b):
    xs, lw, a, g = _rwkv_prep(h, ln_w, mu, w0, w1, w2, a0, a1, a2, g1, g2)
    rkv = _rkv_proj(xs, w_rkv, j)
    z = _rwkv_scan(rkv, lw, a, g, k_k, k_a, r_k, ln_gn_w, ln_gn_b)
    return _out_proj(z, w_o, j, h)


def kernel(x, mem, ln_mix_w, ln_xattn_w, ln_mem_w, xattn_wq, xattn_wkv, xattn_wo, ln_ffn_w, router_group_w, router_group_b, router_expert_w, router_expert_b, moe_w_gate_up, moe_w_down, pool_w, pool_scale, rwkv_mu, rwkv_w_rkv, rwkv_w_o, rwkv_w0, rwkv_w1, rwkv_w2, rwkv_a0, rwkv_a1, rwkv_a2, rwkv_g1, rwkv_g2, rwkv_k_k, rwkv_k_a, rwkv_r_k, rwkv_ln_w, rwkv_ln_b, ln_out_w):
    B, S, D = x.shape
    depth = ln_mix_w.shape[0]
    outs = []
    for b in range(B):
        h = x[b]
        for i in range(depth):
            j = i // 2
            if i % 2 == 0:
                h = _pool_layer(h, ln_mix_w[i], pool_w[j], pool_scale[j])
            else:
                h = _rwkv_layer(h, ln_mix_w[i], rwkv_mu[j], rwkv_w_rkv, rwkv_w_o, j, rwkv_w0[j], rwkv_w1[j],
                                rwkv_w2[j], rwkv_a0[j], rwkv_a1[j], rwkv_a2[j], rwkv_g1[j], rwkv_g2[j],
                                rwkv_k_k[j], rwkv_k_a[j], rwkv_r_k[j].reshape(D), rwkv_ln_w[j], rwkv_ln_b[j])
            kv = _mem_kv(mem[b], ln_mem_w[i], xattn_wkv, i)
            h = _xattn_layer(h, ln_xattn_w[i], xattn_wq, kv, xattn_wo, i)
            h = _moe_layer(h, ln_ffn_w[i], router_group_w[i], router_group_b[i], router_expert_w[i],
                           router_expert_b[i], moe_w_gate_up, moe_w_down, i, ln_out_w,
                           final_norm=(i == depth - 1))
        outs.append(h)
    return outs[0][None] if B == 1 else jnp.stack(outs, axis=0)
```

```python
import functools

import jax
import jax.numpy as jnp
from jax import lax
from jax.experimental import pallas as pl
from jax.experimental.pallas import tpu as pltpu

F32 = jnp.float32
BF16 = jnp.bfloat16

D_MODEL = 2048
POOL_WINDOWS = (2, 4, 8, 16)
POOL_GROUP_DIM = D_MODEL // len(POOL_WINDOWS)
MAX_WINDOW = max(POOL_WINDOWS)
RWKV_HEAD_DIM = 64
GN_EPS = 64e-5
XATTN_HEADS = 4
XATTN_HEAD_DIM = D_MODEL // XATTN_HEADS
N_GROUPS = 8
EXPERTS_PER_GROUP = 8
N_EXPERTS = N_GROUPS * EXPERTS_PER_GROUP
TOP_K = 2
D_EXPERT = D_MODEL // 8
ROW_BLOCK = 128
RMS_EPS = 1e-6

LANES = 128
SUBLANES = 8
VMEM_LIMIT = 56 * 1024 * 1024

CHUNK = 64
GROUP_HEADS = 4
GROUP = GROUP_HEADS * RWKV_HEAD_DIM
SLAB = D_MODEL // LANES
SLAB_PITCH = SLAB + SUBLANES
ISSUE_UNROLL = 16
WEIGHT_STAGE_ROWS = 256
SCAN_ROWS = 256
SCAN_LANES = 2048

NN = (((1,), (0,)), ((), ()))
NT = (((1,), (1,)), ((), ()))


def _params(*sem):
    return pltpu.CompilerParams(dimension_semantics=sem, vmem_limit_bytes=VMEM_LIMIT)


def _rms(x, w):
    return x * lax.rsqrt(jnp.mean(x * x, axis=-1, keepdims=True) + RMS_EPS) * w


def _dot(a, b, dims=NN):
    return lax.dot_general(a, b, dims, preferred_element_type=F32)


def _split(x):
    hi = x.astype(BF16)
    return hi, (x - hi.astype(F32)).astype(BF16)


def _store_slabs(ref, x):
    for j in range(SLAB):
        ref[pl.ds(j, x.shape[0], stride=SLAB), :] = x[:, j * LANES:(j + 1) * LANES]


def _load_slabs(ref, pitch=SLAB):
    n = ref.shape[0] // pitch
    return jnp.concatenate([ref[pl.ds(j, n, stride=pitch), :] for j in range(SLAB)], axis=-1)


def _slab(ref, row, pitch=SLAB):
    return ref.at[pl.ds(pl.multiple_of(row * pitch, SUBLANES), SLAB), :]


def _rkv_proj_body(x_ref, w_ref, o_ref, wb_ref):
    @pl.when(pl.program_id(2) == 0)
    def _():
        wb_ref[...] = w_ref[...].astype(BF16)

    o_ref[...] = _dot(x_ref[...], wb_ref[...])


def _rkv_proj(xs, w_rkv, j, tm=1024, tn=1024):
    _, M, K = xs.shape
    B, N = w_rkv.shape[1], w_rkv.shape[3]
    return pl.pallas_call(
        _rkv_proj_body,
        grid=(B, N // tn, M // tm),
        in_specs=[pl.BlockSpec((None, tm, K), lambda b, n, i: (b, i, 0)),
                  pl.BlockSpec((None, None, K, tn), lambda b, n, i: (j, b, 0, n))],
        out_specs=pl.BlockSpec((None, tm, tn), lambda b, n, i: (b, i, n)),
        out_shape=jax.ShapeDtypeStruct((B, M, N), F32),
        scratch_shapes=[pltpu.VMEM((K, tn), BF16)],
        compiler_params=_params("arbitrary", "arbitrary", "arbitrary"),
        name="rwkv_rkv_proj",
    )(xs, w_rkv)


def _out_proj_body(x_ref, w_ref, r_ref, o_ref, wb_ref):
    @pl.when(pl.program_id(1) == 0)
    def _():
        wb_ref[...] = w_ref[...].astype(BF16)

    o_ref[...] = r_ref[...] + _dot(x_ref[...], wb_ref[...])


def _out_proj(x, w_o, j, res, tm=1024, tn=1024):
    M, K = x.shape
    N = w_o.shape[2]
    return pl.pallas_call(
        _out_proj_body,
        grid=(N // tn, M // tm),
        in_specs=[pl.BlockSpec((tm, K), lambda n, i: (i, 0)),
                  pl.BlockSpec((None, K, tn), lambda n, i: (j, 0, n)),
                  pl.BlockSpec((tm, tn), lambda n, i: (i, n))],
        out_specs=pl.BlockSpec((tm, tn), lambda n, i: (i, n)),
        out_shape=jax.ShapeDtypeStruct((M, N), F32),
        scratch_shapes=[pltpu.VMEM((K, tn), BF16)],
        compiler_params=_params("arbitrary", "arbitrary"),
        name="rwkv_out_proj",
    )(x, w_o, res)


def _pool_body(h_ref, halo_ref, lnw_ref, pw_ref, ps_ref, o_ref):
    i = pl.program_id(0)
    ts = h_ref.shape[0]
    h = h_ref[...]
    lnw = lnw_ref[...]
    x = _rms(h, lnw)
    xh = _rms(halo_ref[...], lnw) * (i > 0).astype(F32)
    xe = jnp.concatenate([xh, x], axis=0)
    t = i * ts + lax.broadcasted_iota(jnp.int32, (ts, 1), 0)
    outs = []
    for g, win in enumerate(POOL_WINDOWS):
        sl = slice(g * POOL_GROUP_DIM, (g + 1) * POOL_GROUP_DIM)
        acc = xe[:, sl]
        span = 1
        while span < win:
            acc = acc + pltpu.roll(acc, span, axis=0)
            span *= 2
        cnt = jnp.minimum(t + 1, win).astype(F32)
        pooled = acc[MAX_WINDOW:, :] / cnt - x[:, sl]
        outs.append(_dot(pooled.astype(BF16), pw_ref[g]))
    y = jnp.concatenate(outs, axis=-1)
    o_ref[...] = h + y * ps_ref[...]


def _pool_layer(h, ln_w, pool_w, pool_scale, ts=512):
    S, D = h.shape
    G = pool_w.shape[0]
    halo_blocks = ts // MAX_WINDOW
    return pl.pallas_call(
        _pool_body,
        grid=(S // ts,),
        in_specs=[pl.BlockSpec((ts, D), lambda i: (i, 0)),
                  pl.BlockSpec((MAX_WINDOW, D), lambda i: (jnp.maximum(i * halo_blocks - 1, 0), 0)),
                  pl.BlockSpec((1, D), lambda i: (0, 0)),
                  pl.BlockSpec((G, POOL_GROUP_DIM, POOL_GROUP_DIM), lambda i: (0, 0, 0)),
                  pl.BlockSpec((1, D), lambda i: (0, 0))],
        out_specs=pl.BlockSpec((ts, D), lambda i: (i, 0)),
        out_shape=jax.ShapeDtypeStruct((S, D), F32),
        compiler_params=_params("arbitrary"),
        name="pool_layer",
    )(h, h, ln_w.reshape(1, D), pool_w.astype(BF16), pool_scale.reshape(1, D))


def _memkv_body(mem_ref, lnw_ref, w_ref, o_ref):
    mn = _rms(mem_ref[...], lnw_ref[...])
    o_ref[...] = _dot(mn.astype(BF16), w_ref[...].astype(BF16)).astype(o_ref.dtype)


def _mem_kv(mem, ln_w, wkv, layer, tn=1024):
    M, D = mem.shape
    N = wkv.shape[2]
    return pl.pallas_call(
        _memkv_body,
        grid=(N // tn,),
        in_specs=[pl.BlockSpec((M, D), lambda n: (0, 0)),
                  pl.BlockSpec((1, D), lambda n: (0, 0)),
                  pl.BlockSpec((None, D, tn), lambda n: (layer, 0, n))],
        out_specs=pl.BlockSpec((M, tn), lambda n: (0, n)),
        out_shape=jax.ShapeDtypeStruct((M, N), BF16),
        compiler_params=_params("arbitrary"),
        name="xattn_mem_kv",
    )(mem, ln_w.reshape(1, D), wkv)


def _load_weight_bf16(w_hbm, dst, stage, sem):
    rows = stage.shape[1]
    n_chunks = w_hbm.shape[0] // rows

    def chunk_copy(c):
        return pltpu.make_async_copy(w_hbm.at[pl.ds(c * rows, rows), :], stage.at[c % 2], sem.at[c % 2])

    chunk_copy(0).start()
    for c in range(n_chunks):
        if c + 1 < n_chunks:
            chunk_copy(c + 1).start()
        chunk_copy(c).wait()
        dst[pl.ds(c * rows, rows), :] = stage[c % 2].astype(BF16)


def _xattn_body(layer, h_ref, lnw_ref, k_ref, v_ref, wq_hbm, wo_hbm, o_ref, wq_ref, wo_ref, stage, sem):
    @pl.when(pl.program_id(0) == 0)
    def _():
        _load_weight_bf16(wq_hbm.at[layer], wq_ref, stage, sem)
        _load_weight_bf16(wo_hbm.at[layer], wo_ref, stage, sem)

    h = h_ref[...]
    hn = _rms(h, lnw_ref[...]).astype(BF16)
    q = _dot(hn, wq_ref[...])
    scale = XATTN_HEAD_DIM ** -0.5
    heads = []
    for hd in range(XATTN_HEADS):
        sl = slice(hd * XATTN_HEAD_DIM, (hd + 1) * XATTN_HEAD_DIM)
        s = _dot(q[:, sl].astype(BF16), k_ref[:, sl], NT) * scale
        s = s - jnp.max(s, axis=-1, keepdims=True)
        e = jnp.exp(s)
        p = e / jnp.sum(e, axis=-1, keepdims=True)
        heads.append(_dot(p.astype(BF16), v_ref[:, sl]))
    o = jnp.concatenate(heads, axis=-1).astype(BF16)
    o_ref[...] = h + _dot(o, wo_ref[...])


def _xattn_layer(h, ln_w, wq, kv, wo, layer, ts=512):
    S, D = h.shape
    M = kv.shape[0]
    return pl.pallas_call(
        functools.partial(_xattn_body, layer),
        grid=(S // ts,),
        in_specs=[pl.BlockSpec((ts, D), lambda i: (i, 0)),
                  pl.BlockSpec((1, D), lambda i: (0, 0)),
                  pl.BlockSpec((M, D), lambda i: (0, 0)),
                  pl.BlockSpec((M, D), lambda i: (0, 1)),
                  pl.BlockSpec(memory_space=pl.ANY),
                  pl.BlockSpec(memory_space=pl.ANY)],
        out_specs=pl.BlockSpec((ts, D), lambda i: (i, 0)),
        out_shape=jax.ShapeDtypeStruct((S, D), F32),
        scratch_shapes=[pltpu.VMEM((D, D), BF16), pltpu.VMEM((D, D), BF16),
                        pltpu.VMEM((2, WEIGHT_STAGE_ROWS, D), F32), pltpu.SemaphoreType.DMA((2,))],
        compiler_params=_params("arbitrary"),
        name="xattn_layer",
    )(h, ln_w.reshape(1, D), kv, kv, wq, wo)


def _router_body(h_ref, lnw_ref, whi_ref, wlo_ref, b_ref, hn_ref, r_ref, rt_ref, cnt_ref):
    i = pl.program_id(0)

    @pl.when(i == 0)
    def _():
        cnt_ref[...] = jnp.zeros_like(cnt_ref)

    hn = _rms(h_ref[...], lnw_ref[...])
    _store_slabs(hn_ref, hn)
    xh, xl = _split(hn)
    whi = whi_ref[...]
    lg = _dot(xh, whi) + (_dot(xh, wlo_ref[...]) + _dot(xl, whi)) + b_ref[...]
    ts = lg.shape[0]
    lane = lax.broadcasted_iota(jnp.int32, (ts, LANES), 1)
    lanef = lane.astype(F32)
    neg = jnp.float32(-jnp.inf)
    gl = jnp.where(lane < N_GROUPS, lg, neg)
    gmax = jnp.max(gl, axis=-1, keepdims=True)
    pg_top = 1.0 / jnp.sum(jnp.exp(gl - gmax), axis=-1, keepdims=True)
    g_idx = jnp.min(jnp.where(gl == gmax, lanef, float(LANES)), axis=-1, keepdims=True)
    lo_lane = g_idx * EXPERTS_PER_GROUP + N_GROUPS
    el = jnp.where(lanef >= lo_lane, jnp.where(lanef < lo_lane + EXPERTS_PER_GROUP, lg, neg), neg)
    m1 = jnp.max(el, axis=-1, keepdims=True)
    i1 = jnp.min(jnp.where(el == m1, lanef, float(LANES)), axis=-1, keepdims=True)
    el2 = jnp.where(lanef == i1, neg, el)
    m2 = jnp.max(el2, axis=-1, keepdims=True)
    i2 = jnp.min(jnp.where(el2 == m2, lanef, float(LANES)), axis=-1, keepdims=True)
    e2 = jnp.exp(m2 - m1)
    g1 = pg_top / (1.0 + e2)
    g2 = pg_top * e2 / (1.0 + e2)
    hit1 = (lanef == i1).astype(F32)
    hit2 = (lanef == i2).astype(F32)
    hits = hit1 + hit2
    earlier = (lax.broadcasted_iota(jnp.int32, (ts, ts), 1)
               < lax.broadcasted_iota(jnp.int32, (ts, ts), 0)).astype(F32).astype(BF16)
    before = _dot(earlier, hits.astype(BF16)) + cnt_ref[...]
    r1 = jnp.sum(before * hit1, axis=-1, keepdims=True)
    r2 = jnp.sum(before * hit2, axis=-1, keepdims=True)
    cnt_ref[...] += jnp.sum(hits, axis=0, keepdims=True)
    out = jnp.where(lane == 0, g1, 0.0)
    out = jnp.where(lane == 1, g2, out)
    out = jnp.where(lane == 2, i1 - N_GROUPS, out)
    out = jnp.where(lane == 3, i2 - N_GROUPS, out)
    out = jnp.where(lane == 4, r1, out)
    out = jnp.where(lane == 5, r2, out)
    r_ref[...] = out
    rt_ref[...] = out.T[:SUBLANES, :]


def _router(h, ln_w, wg, bg, we, be, ts=512):
    S, D = h.shape
    unused = LANES - N_GROUPS - N_EXPERTS
    w = jnp.concatenate([wg, we, jnp.zeros((D, unused), F32)], axis=1)
    b = jnp.concatenate([bg, be, jnp.zeros((unused,), F32)]).reshape(1, LANES)
    whi, wlo = _split(w)
    return pl.pallas_call(
        _router_body,
        grid=(S // ts,),
        in_specs=[pl.BlockSpec((ts, D), lambda i: (i, 0)),
                  pl.BlockSpec((1, D), lambda i: (0, 0)),
                  pl.BlockSpec((D, LANES), lambda i: (0, 0)),
                  pl.BlockSpec((D, LANES), lambda i: (0, 0)),
                  pl.BlockSpec((1, LANES), lambda i: (0, 0))],
        out_specs=[pl.BlockSpec((ts * SLAB, LANES), lambda i: (i, 0)),
                   pl.BlockSpec((ts, LANES), lambda i: (i, 0)),
                   pl.BlockSpec((SUBLANES, ts), lambda i: (0, i)),
                   pl.BlockSpec((1, LANES), lambda i: (0, 0))],
        out_shape=[jax.ShapeDtypeStruct((S * SLAB, LANES), F32), jax.ShapeDtypeStruct((S, LANES), F32),
                   jax.ShapeDtypeStruct((SUBLANES, S), F32), jax.ShapeDtypeStruct((1, LANES), F32)],
        compiler_params=_params("arbitrary"),
        name="moe_router",
    )(h, ln_w.reshape(1, D), whi, wlo, b)


def _dispatch_body(pos_ref, fill_lo_ref, fill_hi_ref, nblk_ref, hn_ref, xs_hbm, stage, zeros, sem, zsem):
    i = pl.program_id(0)
    n = pl.num_programs(0)
    tb = hn_ref.shape[0] // SLAB
    n_tok = n * tb
    slot = i % 2

    def wait_step(s):
        for _ in range(TOP_K):
            pltpu.make_async_copy(stage.at[s], xs_hbm.at[pl.ds(0, tb * SLAB), :], sem.at[s]).wait()

    @pl.when(i >= 2)
    def _():
        wait_step(slot)

    stage[slot] = hn_ref[...]

    def issue(r, carry):
        for k in range(TOP_K):
            pltpu.make_async_copy(_slab(stage.at[slot], r), _slab(xs_hbm, pos_ref[k * n_tok + i * tb + r]),
                                  sem.at[slot]).start(priority=k % 2)
        return carry
    lax.fori_loop(0, tb, issue, 0, unroll=ISSUE_UNROLL)

    @pl.when(jnp.logical_and(i == n - 1, n >= 2))
    def _():
        wait_step(1 - slot)

    @pl.when(i == n - 1)
    def _():
        zeros[...] = jnp.zeros_like(zeros)

        def zero_row(row):
            return pltpu.make_async_copy(zeros.at[pl.ds(0, SLAB), :], _slab(xs_hbm, row), zsem)

        def zero_block(b):
            return pltpu.make_async_copy(zeros, _slab_block(xs_hbm, b), zsem)

        def per_expert(e, carry):
            lo, hi = fill_lo_ref[e], fill_hi_ref[e]
            lax.fori_loop(lo, hi, lambda row, c: (zero_row(row).start(), c)[1], 0)
            lax.fori_loop(lo, hi, lambda row, c: (zero_row(row).wait(), c)[1], 0)
            return carry
        lax.fori_loop(0, N_EXPERTS, per_expert, 0)
        n_blocks = xs_hbm.shape[0] // (ROW_BLOCK * SLAB)
        lax.fori_loop(nblk_ref[0], n_blocks, lambda b, c: (zero_block(b).start(), c)[1], 0)
        lax.fori_loop(nblk_ref[0], n_blocks, lambda b, c: (zero_block(b).wait(), c)[1], 0)
        wait_step(slot)


def _slab_block(ref, block):
    rows = ROW_BLOCK * SLAB
    return ref.at[pl.ds(pl.multiple_of(block * rows, rows), rows), :]


def _dispatch(pos, fill_lo, fill_hi, n_used, hn, n_rows, tb=512):
    S = hn.shape[0] // SLAB
    return pl.pallas_call(
        _dispatch_body,
        grid_spec=pltpu.PrefetchScalarGridSpec(
            num_scalar_prefetch=4,
            grid=(S // tb,),
            in_specs=[pl.BlockSpec((tb * SLAB, LANES), lambda i, *_: (i, 0))],
            out_specs=pl.BlockSpec(memory_space=pl.ANY),
            scratch_shapes=[pltpu.VMEM((2, tb * SLAB, LANES), F32), pltpu.VMEM((ROW_BLOCK * SLAB, LANES), F32),
                            pltpu.SemaphoreType.DMA((2,)), pltpu.SemaphoreType.DMA(())]),
        out_shape=jax.ShapeDtypeStruct((n_rows * SLAB, LANES), F32),
        compiler_params=_params("arbitrary"),
        name="moe_dispatch",
    )(pos, fill_lo, fill_hi, n_used, hn)


def _experts_body(layer, nblk_ref, be_ref, fresh_ref, wslot_ref, next_e_ref, x_ref, wgu_hbm, wd_hbm, o_ref,
                  wgu_f, wd_f, wgu_b, wd_b, wsem):
    i = pl.program_id(0)
    n_used = nblk_ref[0]

    def weight_copies(e, slot):
        return (pltpu.make_async_copy(wgu_hbm.at[layer, e], wgu_f.at[slot], wsem.at[0, slot]),
                pltpu.make_async_copy(wd_hbm.at[layer, e], wd_f.at[slot], wsem.at[1, slot]))

    @pl.when(i == 0)
    def _():
        for cp in weight_copies(be_ref[0], 0):
            cp.start()

    @pl.when(jnp.logical_and(i < n_used, fresh_ref[i] == 1))
    def _():
        slot = wslot_ref[i]
        for cp in weight_copies(be_ref[i], slot):
            cp.wait()
        wgu_b[...] = wgu_f[slot].astype(BF16)
        wd_b[...] = wd_f[slot].astype(BF16)

        @pl.when(next_e_ref[i] >= 0)
        def _():
            for cp in weight_copies(next_e_ref[i], 1 - slot):
                cp.start()

    @pl.when(i < n_used)
    def _():
        gu = _dot(_load_slabs(x_ref).astype(BF16), wgu_b[...])
        gg = gu[:, :D_EXPERT]
        uu = gu[:, D_EXPERT:]
        act = (gg * (1.0 / (1.0 + jnp.exp(-gg))) * uu).astype(BF16)
        _store_slabs(o_ref, _dot(act, wd_b[...]))

    @pl.when(i >= n_used)
    def _():
        o_ref[...] = jnp.zeros_like(o_ref)


def _experts(n_used, block_e, xs, w_gu, w_down, layer):
    rows = ROW_BLOCK * SLAB
    n_blocks = xs.shape[0] // rows
    D = w_gu.shape[2]
    F2 = w_gu.shape[3]
    idx = jnp.arange(n_blocks, dtype=jnp.int32)
    used = idx < n_used[0]
    fresh = jnp.logical_and(used, jnp.concatenate([jnp.ones((1,), bool), block_e[1:] != block_e[:-1]]))
    run = jnp.cumsum(fresh.astype(jnp.int32)) - 1
    run_start = jnp.where(fresh, idx, n_blocks)
    next_start = lax.cummin(jnp.concatenate([run_start[1:], jnp.full((1,), n_blocks, jnp.int32)]), reverse=True)
    next_e = jnp.where(next_start < n_blocks, block_e[jnp.minimum(next_start, n_blocks - 1)], -1)
    return pl.pallas_call(
        functools.partial(_experts_body, layer),
        grid_spec=pltpu.PrefetchScalarGridSpec(
            num_scalar_prefetch=5,
            grid=(n_blocks,),
            in_specs=[pl.BlockSpec((rows, LANES), lambda i, nb, *_: (jnp.minimum(i, nb[0] - 1), 0)),
                      pl.BlockSpec(memory_space=pl.ANY),
                      pl.BlockSpec(memory_space=pl.ANY)],
            out_specs=pl.BlockSpec((rows, LANES), lambda i, *_: (i, 0)),
            scratch_shapes=[pltpu.VMEM((2, D, F2), F32), pltpu.VMEM((2, F2 // 2, D), F32),
                            pltpu.VMEM((D, F2), BF16), pltpu.VMEM((F2 // 2, D), BF16),
                            pltpu.SemaphoreType.DMA((2, 2))]),
        out_shape=jax.ShapeDtypeStruct(xs.shape, F32),
        compiler_params=_params("arbitrary"),
        name="moe_experts",
    )(n_used, block_e, fresh.astype(jnp.int32), (run % 2).astype(jnp.int32), next_e.astype(jnp.int32),
      xs, w_gu, w_down)


def _start_pair_gather(pos_ref, first_tok, n_tok, yb_hbm, ybuf, slot, sem, tb):
    def issue(r, carry):
        for k in range(TOP_K):
            pltpu.make_async_copy(_slab(yb_hbm, pos_ref[k * n_tok + first_tok + r]), _slab(ybuf.at[slot, k], r, SLAB_PITCH),
                                  sem.at[slot]).start(priority=k % 2)
        return carry
    lax.fori_loop(0, tb, issue, 0, unroll=ISSUE_UNROLL)


def _wait_pair_gather(yb_hbm, ybuf, slot, sem, tb):
    for k in range(TOP_K):
        pltpu.make_async_copy(yb_hbm.at[pl.ds(0, tb * SLAB), :], ybuf.at[slot, k, pl.ds(0, tb * SLAB), :],
                              sem.at[slot]).wait()


def _combine_body(final_norm, pos_ref, yb_hbm, h_ref, r_ref, lnw_ref, o_ref, ybuf, sem):
    i = pl.program_id(0)
    n = pl.num_programs(0)
    tb = h_ref.shape[0]
    slot = i % 2

    @pl.when(i == 0)
    def _():
        _start_pair_gather(pos_ref, 0, n * tb, yb_hbm, ybuf, 0, sem, tb)

    @pl.when(i + 1 < n)
    def _():
        _start_pair_gather(pos_ref, (i + 1) * tb, n * tb, yb_hbm, ybuf, 1 - slot, sem, tb)

    _wait_pair_gather(yb_hbm, ybuf, slot, sem, tb)
    out = h_ref[...]
    for k in range(TOP_K):
        out = out + _load_slabs(ybuf.at[slot, k], SLAB_PITCH) * r_ref[:, k:k + 1]
    if final_norm:
        out = _rms(out, lnw_ref[...])
    o_ref[...] = out


def _combine(pos, yb, h, routed, ln_out_w, final_norm, tb=256):
    S, D = h.shape
    return pl.pallas_call(
        functools.partial(_combine_body, final_norm),
        grid_spec=pltpu.PrefetchScalarGridSpec(
            num_scalar_prefetch=1,
            grid=(S // tb,),
            in_specs=[pl.BlockSpec(memory_space=pl.ANY),
                      pl.BlockSpec((tb, D), lambda i, pos: (i, 0)),
                      pl.BlockSpec((tb, LANES), lambda i, pos: (i, 0)),
                      pl.BlockSpec((1, D), lambda i, pos: (0, 0))],
            out_specs=pl.BlockSpec((tb, D), lambda i, pos: (i, 0)),
            scratch_shapes=[pltpu.VMEM((2, TOP_K, tb * SLAB_PITCH, LANES), F32), pltpu.SemaphoreType.DMA((2,))]),
        out_shape=jax.ShapeDtypeStruct((S, D), F32),
        compiler_params=_params("arbitrary"),
        name="moe_combine",
    )(pos, yb, h, routed, ln_out_w.reshape(1, D))


def _moe_layer(h, ln_w, wg, bg, we, be, w_gu, w_down, layer, ln_out_w, final_norm):
    T, D = h.shape
    hn, routed, routed_t, counts = _router(h, ln_w, wg, bg, we, be)
    A = T * TOP_K
    n_blocks = -(-A // ROW_BLOCK) + N_EXPERTS
    n_rows = n_blocks * ROW_BLOCK
    counts = counts[0, N_GROUPS:N_GROUPS + N_EXPERTS].astype(jnp.int32)
    padded = (counts + ROW_BLOCK - 1) // ROW_BLOCK * ROW_BLOCK
    pad_end = jnp.cumsum(padded)
    pad_start = pad_end - padded
    expert_idx = routed_t[2:2 + TOP_K].astype(jnp.int32)
    slot = routed_t[4:4 + TOP_K].astype(jnp.int32)
    is_e = expert_idx[..., None] == jnp.arange(N_EXPERTS, dtype=jnp.int32)
    pos = (jnp.sum(jnp.where(is_e, pad_start, 0), axis=-1) + slot).reshape(A)
    block_start = jnp.arange(n_blocks, dtype=jnp.int32) * ROW_BLOCK
    block_e = jnp.minimum(jnp.sum(pad_end[None, :] <= block_start[:, None], axis=1), N_EXPERTS - 1).astype(jnp.int32)
    n_used = (pad_end[-1] // ROW_BLOCK).astype(jnp.int32).reshape(1)
    xs = _dispatch(pos, pad_start + counts, pad_end, n_used, hn, n_rows)
    yb = _experts(n_used, block_e, xs, w_gu, w_down, layer)
    return _combine(pos, yb, h, routed, ln_out_w, final_norm)


def _sigmoid(x):
    return 1.0 / (1.0 + jnp.exp(-x))


def _rwkv_prep_body(h_ref, halo_ref, lnw_ref, mu_ref, w0_ref, w1_ref, w2_ref, a0_ref, a1_ref, a2_ref, g1_ref,
                    g2_ref, xs_ref, lw_ref, a_ref, g_ref):
    i = pl.program_id(0)
    lnw = lnw_ref[...]
    hn = _rms(h_ref[...], lnw)
    ts = hn.shape[0]
    last = _rms(halo_ref[...], lnw)[SUBLANES - 1:SUBLANES, :] * (i > 0).astype(F32)
    row = lax.broadcasted_iota(jnp.int32, (ts, 1), 0)
    prev = jnp.where(row == 0, last, pltpu.roll(hn, 1, axis=0))
    xx = prev - hn
    mix = lambda n: (hn + xx * mu_ref[n:n + 1, :]).astype(BF16)
    n_out = xs_ref.shape[0]
    for n in range(n_out):
        xs_ref[n] = mix(n)
    z = w0_ref[...] + _dot(jnp.tanh(_dot(mix(n_out), w1_ref[...])).astype(BF16), w2_ref[...])
    u = -z
    softplus = jnp.maximum(u, 0.0) + jnp.log1p(jnp.exp(-jnp.abs(u)))
    lw_ref[...] = -jnp.exp(-softplus - 0.5)
    a_ref[...] = _sigmoid(a0_ref[...] + _dot(_dot(mix(n_out + 1), a1_ref[...]).astype(BF16), a2_ref[...]))
    g_ref[...] = _dot(_sigmoid(_dot(mix(n_out + 2), g1_ref[...])).astype(BF16), g2_ref[...])


def _rwkv_prep(h, ln_w, mu, w0, w1, w2, a0, a1, a2, g1, g2, ts=256):
    S, D = h.shape
    n_mix = mu.shape[0]
    n_out = n_mix - 3
    halo_blocks = ts // SUBLANES

    def pad_lora(wa, wb):
        r = wa.shape[1]
        rp = -(-r // LANES) * LANES
        return (jnp.pad(wa, ((0, 0), (0, rp - r))).astype(BF16), jnp.pad(wb, ((0, rp - r), (0, 0))).astype(BF16))

    w1p, w2p = pad_lora(w1, w2)
    a1p, a2p = pad_lora(a1, a2)
    g1p, g2p = pad_lora(g1, g2)
    full = lambda arr: pl.BlockSpec(arr.shape, lambda i: (0,) * arr.ndim)
    vec = pl.BlockSpec((1, D), lambda i: (0, 0))
    row = pl.BlockSpec((ts, D), lambda i: (i, 0))
    return pl.pallas_call(
        _rwkv_prep_body,
        grid=(S // ts,),
        in_specs=[row,
                  pl.BlockSpec((SUBLANES, D), lambda i: (jnp.maximum(i * halo_blocks - 1, 0), 0)),
                  vec, full(mu), vec, full(w1p), full(w2p), vec, full(a1p), full(a2p), full(g1p), full(g2p)],
        out_specs=[pl.BlockSpec((n_out, ts, D), lambda i: (0, i, 0)), row, row, row],
        out_shape=[jax.ShapeDtypeStruct((n_out, S, D), BF16)] + [jax.ShapeDtypeStruct((S, D), F32)] * 3,
        compiler_params=_params("arbitrary"),
        name="rwkv_prep",
    )(h, h, ln_w.reshape(1, D), mu, w0.reshape(1, D), w1p, w2p, a0.reshape(1, D), a1p, a2p, g1p, g2p)


def _scan_chunk(r, k, v, lw, a, g, kkw, kaw, rkw, lnw, lnb, state, c):
    masks, seg_f, seg_b, tri, strict, incl, eye = c
    bf = lambda x: x.astype(BF16)
    stack = lambda xb: jnp.concatenate([xb * m for m in masks], axis=0)
    rows = lambda *xs: jnp.concatenate(xs, axis=0)
    cols = lambda *xs: jnp.concatenate(xs, axis=1)

    kk = k * kkw
    k2 = k * (1.0 + (a - 1.0) * kaw)
    sums = _segsum(rows(kk * kk, r * k2 * rkw), seg_b)
    cw = _cumsum_rows(tri, lw)
    yield
    kk = kk / jnp.maximum(jnp.sqrt(sums[:CHUNK]), 1e-12)
    bonus = sums[CHUNK:] * v
    bv = kk * a
    cl = cw[CHUNK - 1:CHUNK, :]
    at = bf(-kk * jnp.exp(cw - lw))
    rt = bf(r * jnp.exp(cw))
    dinv = jnp.exp(-cw)
    drem = jnp.exp(cl - cw)
    vb = bf(v)
    vs = stack(vb)
    gram = _dot(rows(at, rt), rows(stack(bf(bv * dinv)), stack(bf(k2 * dinv))), NT)
    bk_t = bf(rows(bv * drem, k2 * drem).T)
    decay_rows = jnp.broadcast_to(jnp.exp(cl), (GROUP, GROUP)).T
    yield
    a_ab = jnp.where(strict, gram[:CHUNK, :GROUP], 0.0)
    a_ak = bf(jnp.where(strict, gram[:CHUNK, GROUP:], 0.0))
    a_rb = bf(jnp.where(incl, gram[CHUNK:, :GROUP], 0.0))
    a_rk = bf(jnp.where(incl, gram[CHUNK:, GROUP:], 0.0))
    inv = eye + a_ab
    pw = _dot(bf(a_ab), stack(bf(a_ab)))
    xy = _dot(rows(cols(at, a_ak), cols(rt, a_rk)), rows(bf(state), vs))
    yield
    span = 2
    while span < CHUNK // 2:
        pwb = bf(pw)
        both = _dot(rows(bf(inv), pwb), stack(pwb))
        yield
        inv = inv + both[:CHUNK]
        pw = both[CHUNK:]
        span *= 2
    inv = inv + _dot(bf(inv), stack(bf(pw)))
    yield
    ub = bf(_dot(bf(inv), stack(bf(xy[:CHUNK]))))
    yield
    y = xy[CHUNK:] + _dot(a_rb, stack(ub))
    new_state = state * decay_rows + seg_f * _dot(bk_t, rows(ub, vb))
    yield
    inv_n = 1.0 / RWKV_HEAD_DIM
    mean = _segsum(y, seg_b) * inv_n
    yield
    yc = y - mean
    var = _segsum(yc * yc, seg_b) * inv_n
    yield
    yn = yc * lax.rsqrt(var + GN_EPS) * lnw + lnb
    return ((yn + bonus) * g), new_state


def _run_interleaved(gens):
    results = [None] * len(gens)
    live = list(range(len(gens)))
    while live:
        for i in list(live):
            try:
                next(gens[i])
            except StopIteration as done:
                results[i] = done.value
                live.remove(i)
    return results


def _segsum(x, seg_b):
    return _dot(x.astype(BF16), seg_b)


def _cumsum_rows(tri, x):
    hi = x.astype(BF16)
    r1 = x - hi.astype(F32)
    mid = r1.astype(BF16)
    lo = (r1 - mid.astype(F32)).astype(BF16)
    return _dot(jnp.concatenate([tri, tri, tri], axis=1), jnp.concatenate([hi, mid, lo], axis=0))


def _scan_consts():
    shift = RWKV_HEAD_DIM.bit_length() - 1
    lane_head = lax.broadcasted_iota(jnp.int32, (1, GROUP), 1) >> shift
    masks = [(lane_head == h).astype(F32).astype(BF16) for h in range(GROUP_HEADS)]
    ri = lax.broadcasted_iota(jnp.int32, (GROUP, GROUP), 0) >> shift
    ci = lax.broadcasted_iota(jnp.int32, (GROUP, GROUP), 1) >> shift
    seg_f = (ri == ci).astype(F32)
    tri = (lax.broadcasted_iota(jnp.int32, (CHUNK, CHUNK), 1)
           <= lax.broadcasted_iota(jnp.int32, (CHUNK, CHUNK), 0)).astype(F32).astype(BF16)
    t = lax.broadcasted_iota(jnp.int32, (CHUNK, GROUP), 0)
    s = lax.broadcasted_iota(jnp.int32, (CHUNK, GROUP), 1) & (RWKV_HEAD_DIM - 1)
    return (masks, seg_f, seg_f.astype(BF16), tri, s < t, s <= t, (s == t).astype(F32))


def _scan_body(r_ref, k_ref, v_ref, lw_ref, a_ref, g_ref, kkw_ref, kaw_ref, rkw_ref, lnw_ref, lnb_ref,
               o_ref, state_ref):
    @pl.when(pl.program_id(1) == 0)
    def _():
        state_ref[...] = jnp.zeros_like(state_ref)

    n_groups = o_ref.shape[1] // GROUP

    def chunk_step(j, carry):
        consts = _scan_consts()
        rows = pl.ds(pl.multiple_of(j * CHUNK, CHUNK), CHUNK)
        lanes = [slice(p * GROUP, (p + 1) * GROUP) for p in range(n_groups)]
        args = [(r_ref[rows, ln], k_ref[rows, ln], v_ref[rows, ln], lw_ref[rows, ln], a_ref[rows, ln],
                 g_ref[rows, ln], kkw_ref[:, ln], kaw_ref[:, ln], rkw_ref[:, ln], lnw_ref[:, ln],
                 lnb_ref[:, ln], state_ref[p]) for p, ln in enumerate(lanes)]
        results = _run_interleaved([_scan_chunk(*a, consts) for a in args])
        for p, (ln, (out, new_state)) in enumerate(zip(lanes, results)):
            state_ref[p] = new_state
            o_ref[rows, ln] = out.astype(o_ref.dtype)
        return carry

    lax.fori_loop(0, o_ref.shape[0] // CHUNK, chunk_step, 0)


def _rwkv_scan(rkv, lw, a, g, k_k, k_a, r_k, ln_w, ln_b):
    _, S, D = rkv.shape
    tr, tl = min(SCAN_ROWS, S), SCAN_LANES
    blk = lambda n: pl.BlockSpec((None, tr, tl), lambda p, i: (n, i, p))
    row = pl.BlockSpec((tr, tl), lambda p, i: (i, p))
    vec = pl.BlockSpec((1, tl), lambda p, i: (0, p))
    as_row = lambda w: w.reshape(1, D)
    return pl.pallas_call(
        _scan_body,
        grid=(D // tl, S // tr),
        in_specs=[blk(0), blk(1), blk(2), row, row, row, vec, vec, vec, vec, vec],
        out_specs=row,
        out_shape=jax.ShapeDtypeStruct((S, D), BF16),
        scratch_shapes=[pltpu.VMEM((tl // GROUP, GROUP, GROUP), F32)],
        compiler_params=_params("arbitrary", "arbitrary"),
        name="rwkv_scan",
    )(rkv, rkv, rkv, lw, a, g, as_row(k_k), as_row(k_a), as_row(r_k), as_row(ln_w), as_row(ln_b))


def _rwkv_layer(h, ln_w, mu, w_rkv, w_o, j, w0, w1, w2, a0, a1, a2, g1, g2, k_k, k_a, r_k, ln_gn_w, ln_gn_b):
    xs, lw, a, g = _rwkv_prep(h, ln_w, mu, w0, w1, w2, a0, a1, a2, g1, g2)
    rkv = _rkv_proj(xs, w_rkv, j)
    z = _rwkv_scan(rkv, lw, a, g, k_k, k_a, r_k, ln_gn_w, ln_gn_b)
    return _out_proj(z, w_o, j, h)


def kernel(x, mem, ln_mix_w, ln_xattn_w, ln_mem_w, xattn_wq, xattn_wkv, xattn_wo, ln_ffn_w, router_group_w, router_group_b, router_expert_w, router_expert_b, moe_w_gate_up, moe_w_down, pool_w, pool_scale, rwkv_mu, rwkv_w_rkv, rwkv_w_o, rwkv_w0, rwkv_w1, rwkv_w2, rwkv_a0, rwkv_a1, rwkv_a2, rwkv_g1, rwkv_g2, rwkv_k_k, rwkv_k_a, rwkv_r_k, rwkv_ln_w, rwkv_ln_b, ln_out_w):
    B, S, D = x.shape
    depth = ln_mix_w.shape[0]
    outs = []
    for b in range(B):
        h = x[b]
        for i in range(depth):
            j = i // 2
            if i % 2 == 0:
                h = _pool_layer(h, ln_mix_w[i], pool_w[j], pool_scale[j])
            else:
                h = _rwkv_layer(h, ln_mix_w[i], rwkv_mu[j], rwkv_w_rkv, rwkv_w_o, j, rwkv_w0[j], rwkv_w1[j],
                                rwkv_w2[j], rwkv_a0[j], rwkv_a1[j], rwkv_a2[j], rwkv_g1[j], rwkv_g2[j],
                                rwkv_k_k[j], rwkv_k_a[j], rwkv_r_k[j].reshape(D), rwkv_ln_w[j], rwkv_ln_b[j])
            kv = _mem_kv(mem[b], ln_mem_w[i], xattn_wkv, i)
            h = _xattn_layer(h, ln_xattn_w[i], xattn_wq, kv, xattn_wo, i)
            h = _moe_layer(h, ln_ffn_w[i], router_group_w[i], router_group_b[i], router_expert_w[i],
                           router_expert_b[i], moe_w_gate_up, moe_w_down, i, ln_out_w,
                           final_norm=(i == depth - 1))
        outs.append(h)
    return outs[0][None] if B == 1 else jnp.stack(outs, axis=0)
```
